```python
import jax, jax.numpy as jnp
from jax import lax
import numpy as np

D_MODEL = 2048
BATCH = 4
SEQ = 2048
DEPTH = 4
DEC_BATCH = 128
DEC_SEQ = 8
PAST_LEN = 16384
PAGE_SIZE = 128

N_MIXERS = 3
CHUNK = 128
A_GROUPS = 8
D_A = D_MODEL
A_GROUP_DIM = D_A // A_GROUPS
B_WIDTH = 3
C_WIDTH = 31
D_C = D_MODEL
D_FF = 4 * D_MODEL
N_LAYERS_A = (DEPTH + 2) // 3
N_LAYERS_B = (DEPTH + 1) // 3
N_LAYERS_C = DEPTH // 3
EPS = 1e-6

kernel_name = 'hybrid_gmlp_shortconv_conformer_decoder_step'


def rms_norm(x, g):
    xf = x.astype(jnp.float32)
    y = xf * lax.rsqrt(jnp.mean(xf * xf, axis=-1, keepdims=True) + EPS)
    return (y * g).astype(x.dtype)


def layer_norm(x, g, b):
    xf = x.astype(jnp.float32)
    mu = jnp.mean(xf, axis=-1, keepdims=True)
    var = jnp.mean(jnp.square(xf - mu), axis=-1, keepdims=True)
    return ((xf - mu) * lax.rsqrt(var + EPS) * g + b).astype(x.dtype)


def depthwise_conv(x_ext, w):
    return lax.conv_general_dilated(
        x_ext, w[:, None, :], window_strides=(1,), padding='VALID',
        dimension_numbers=('NWC', 'WIO', 'NWC'), feature_group_count=x_ext.shape[-1])


def chunk_mlp_mixer(h, w_in, ln_g, ln_b, w_s, b_s, w_out):
    bsz, t_len, _ = h.shape
    z = jax.nn.gelu(h @ w_in, approximate=False)
    u, v = jnp.split(z, 2, axis=-1)
    v = layer_norm(v, ln_g, ln_b)
    L = min(t_len, CHUNK)
    n_chunks = t_len // L
    causal = jnp.tril(jnp.ones((L, L), dtype=bool))
    ws = jnp.where(causal, w_s[:, :L, :L], 0)
    vc = v.reshape(bsz, n_chunks, L, A_GROUPS, A_GROUP_DIM)
    mixed = jnp.einsum('gts,bcsgd->bctgd', ws, vc) + b_s[:, :L].T[None, None, :, :, None]
    y = u * mixed.reshape(bsz, t_len, D_A)
    return y @ w_out, v


def short_conv_mixer(h, past, w_in, conv_w, w_out):
    b_gate, c_gate, hx = jnp.split(h @ w_in, 3, axis=-1)
    x_ext = jnp.concatenate([past, c_gate * hx], axis=1)
    y = b_gate * depthwise_conv(x_ext, conv_w)
    return y @ w_out, x_ext[:, -(B_WIDTH - 1):]


def conformer_conv_mixer(h, past, w_pw1, b_pw1, conv_w, conv_b, ln_g, ln_b, w_pw2, b_pw2):
    a, g = jnp.split(h @ w_pw1 + b_pw1, 2, axis=-1)
    glu = a * jax.nn.sigmoid(g)
    x_ext = jnp.concatenate([past, glu], axis=1)
    c = depthwise_conv(x_ext, conv_w) + conv_b
    c = jax.nn.silu(layer_norm(c, ln_g, ln_b))
    return c @ w_pw2 + b_pw2, x_ext[:, -(C_WIDTH - 1):]


def sq_relu_mlp(h, w_up, w_down):
    return jnp.square(jax.nn.relu(h @ w_up)) @ w_down


def _normal(k, shape, scale):
    return jax.random.normal(k, shape, jnp.float32) * scale


def setup_inputs(seed: int = 0) -> dict:
    key = jax.random.key(seed)
    ks = iter(jax.random.split(key, 32))
    D = D_MODEL
    return {
        'x_prompt': _normal(next(ks), (BATCH, SEQ, D), 1.0),
        'x_sample': _normal(next(ks), (DEC_BATCH, DEC_SEQ, D), 1.0),
        'state_b_conv': _normal(next(ks), (N_LAYERS_B, DEC_BATCH, B_WIDTH - 1, D), 1.0),
        'state_c_conv': _normal(next(ks), (N_LAYERS_C, DEC_BATCH, C_WIDTH - 1, D_C), 0.5),
        'norm_mix_g': 1.0 + _normal(next(ks), (DEPTH, D), 0.1),
        'norm_ffn_g': 1.0 + _normal(next(ks), (DEPTH, D), 0.1),
        'final_norm_g': 1.0 + _normal(next(ks), (D,), 0.1),
        'a_w_in': _normal(next(ks), (N_LAYERS_A, D, 2 * D_A), D ** -0.5),
        'a_ln_g': 1.0 + _normal(next(ks), (N_LAYERS_A, D_A), 0.1),
        'a_ln_b': _normal(next(ks), (N_LAYERS_A, D_A), 0.02),
        'a_w_s': _normal(next(ks), (N_LAYERS_A, A_GROUPS, CHUNK, CHUNK), CHUNK ** -0.5),
        'a_b_s': 1.0 + _normal(next(ks), (N_LAYERS_A, A_GROUPS, CHUNK), 0.1),
        'a_w_out': _normal(next(ks), (N_LAYERS_A, D_A, D), D_A ** -0.5),
        'b_w_in': _normal(next(ks), (N_LAYERS_B, D, 3 * D), D ** -0.5),
        'b_conv_w': _normal(next(ks), (N_LAYERS_B, B_WIDTH, D), B_WIDTH ** -0.5),
        'b_w_out': _normal(next(ks), (N_LAYERS_B, D, D), D ** -0.5),
        'c_w_pw1': _normal(next(ks), (N_LAYERS_C, D, 2 * D_C), D ** -0.5),
        'c_b_pw1': _normal(next(ks), (N_LAYERS_C, 2 * D_C), 0.02),
        'c_conv_w': _normal(next(ks), (N_LAYERS_C, C_WIDTH, D_C), C_WIDTH ** -0.5),
        'c_conv_b': _normal(next(ks), (N_LAYERS_C, D_C), 0.02),
        'c_ln_g': 1.0 + _normal(next(ks), (N_LAYERS_C, D_C), 0.1),
        'c_ln_b': _normal(next(ks), (N_LAYERS_C, D_C), 0.02),
        'c_w_pw2': _normal(next(ks), (N_LAYERS_C, D_C, D), D_C ** -0.5),
        'c_b_pw2': _normal(next(ks), (N_LAYERS_C, D), 0.02),
        'ffn_w_up': _normal(next(ks), (DEPTH, D, D_FF), D ** -0.5),
        'ffn_w_down': _normal(next(ks), (DEPTH, D_FF, D), D_FF ** -0.5),
    }


def reference(x_prompt, x_sample, state_b_conv, state_c_conv, norm_mix_g, norm_ffn_g, final_norm_g,
              a_w_in, a_ln_g, a_ln_b, a_w_s, a_b_s, a_w_out,
              b_w_in, b_conv_w, b_w_out,
              c_w_pw1, c_b_pw1, c_conv_w, c_conv_b, c_ln_g, c_ln_b, c_w_pw2, c_b_pw2,
              ffn_w_up, ffn_w_down):
    def trunk(x, past_b, past_c):
        a_v, b_buf, c_buf = [], [], []
        for i in range(DEPTH):
            j = i // N_MIXERS
            kind = i % N_MIXERS
            h = rms_norm(x, norm_mix_g[i])
            if kind == 0:
                y, v = chunk_mlp_mixer(h, a_w_in[j], a_ln_g[j], a_ln_b[j], a_w_s[j], a_b_s[j], a_w_out[j])
                a_v.append(v)
            elif kind == 1:
                y, buf = short_conv_mixer(h, past_b[j], b_w_in[j], b_conv_w[j], b_w_out[j])
                b_buf.append(buf)
            else:
                y, buf = conformer_conv_mixer(h, past_c[j], c_w_pw1[j], c_b_pw1[j], c_conv_w[j], c_conv_b[j],
                                              c_ln_g[j], c_ln_b[j], c_w_pw2[j], c_b_pw2[j])
                c_buf.append(buf)
            x = x + y
            x = x + sq_relu_mlp(rms_norm(x, norm_ffn_g[i]), ffn_w_up[i], ffn_w_down[i])
        return rms_norm(x, final_norm_g), jnp.stack(a_v), jnp.stack(b_buf), jnp.stack(c_buf)

    n_prompt = x_prompt.shape[0]
    zeros_b = jnp.zeros((N_LAYERS_B, n_prompt, B_WIDTH - 1, D_MODEL), x_prompt.dtype)
    zeros_c = jnp.zeros((N_LAYERS_C, n_prompt, C_WIDTH - 1, D_C), x_prompt.dtype)
    y_prompt, _, new_b_conv_prompt, new_c_conv_prompt = trunk(x_prompt, zeros_b, zeros_c)
    y_sample, new_a_v_sample, new_b_conv_sample, new_c_conv_sample = trunk(x_sample, state_b_conv, state_c_conv)
    return (y_prompt, y_sample, new_a_v_sample, new_b_conv_prompt, new_b_conv_sample,
            new_c_conv_prompt, new_c_conv_sample)
```

```python
import functools

import jax
import jax.numpy as jnp
from jax import lax
from jax.experimental import pallas as pl
from jax.experimental.pallas import tpu as pltpu

_F32 = jnp.float32
_BF16 = jnp.bfloat16
_EPS = 1e-6

_LANES = 128
_SUBLANES = 8
_VMEM_CAP_BYTES = 60000 * 1024
_VMEM_SLACK_BYTES = 12 * 1024 * 1024

_TM = 1024
_TM_MIX = 512
_TF = 512
_ROWS = 64
_CONV_HALO = 32


def _nbytes(shape, dtype):
    n = 1
    for s in shape:
        n *= s
    return n * jnp.dtype(dtype).itemsize


def _params(semantics, windows, scratch=0):
    limit = 2 * sum(_nbytes(s, d) for s, d in windows) + scratch + _VMEM_SLACK_BYTES
    return pltpu.CompilerParams(dimension_semantics=semantics, vmem_limit_bytes=min(limit, _VMEM_CAP_BYTES))


def _rms_rows(x, g):
    return x * lax.rsqrt(jnp.mean(x * x, axis=-1, keepdims=True) + _EPS) * g


def _ln_rows(x, g, b):
    xc = x - jnp.mean(x, axis=-1, keepdims=True)
    return xc * lax.rsqrt(jnp.mean(xc * xc, axis=-1, keepdims=True) + _EPS) * g + b


def _for_row_blocks(n_rows, fn):
    def body(r, carry):
        fn(pl.ds(pl.multiple_of(r * _ROWS, _ROWS), _ROWS))
        return carry
    lax.fori_loop(0, n_rows // _ROWS, body, 0)


def _dot(a, b):
    return jnp.dot(a, b, preferred_element_type=_F32)


def _norm_kernel(x_ref, g_ref, h_ref):
    _for_row_blocks(x_ref.shape[0], lambda rows: h_ref.__setitem__(
        (rows, slice(None)), _rms_rows(x_ref[rows, :], g_ref[...]).astype(h_ref.dtype)))


def _rms_norm(x, g):
    m, d = x.shape
    tm = _TM_MIX
    return pl.pallas_call(
        _norm_kernel,
        grid=(m // tm,),
        in_specs=[pl.BlockSpec((tm, d), lambda i: (i, 0)), pl.BlockSpec((1, d), lambda i: (0, 0))],
        out_specs=pl.BlockSpec((tm, d), lambda i: (i, 0)),
        out_shape=jax.ShapeDtypeStruct((m, d), _BF16),
        compiler_params=_params(("arbitrary",), [((tm, d), _F32), ((tm, d), _BF16)]),
        name="rms_norm",
    )(x, g.reshape(1, d))


def _inproj_gelu_kernel(h_ref, w_ref, z_ref):
    acc = _dot(h_ref[...], w_ref[...])
    z_ref[...] = (0.5 * acc * (1.0 + lax.erf(acc * 0.7071067811865476))).astype(z_ref.dtype)


def _inproj_gate3_kernel(h_ref, wb_ref, wc_ref, wx_ref, bg_ref, cx_ref):
    h = h_ref[...]
    bg_ref[...] = _dot(h, wb_ref[...]).astype(bg_ref.dtype)
    cx_ref[...] = _dot(h, wc_ref[...]) * _dot(h, wx_ref[...])


def _inproj_glu_kernel(h_ref, wa_ref, wg_ref, ba_ref, bgate_ref, glu_ref):
    h = h_ref[...]
    a = _dot(h, wa_ref[...]) + ba_ref[...]
    gate = _dot(h, wg_ref[...]) + bgate_ref[...]
    glu_ref[...] = a * jax.nn.sigmoid(gate)


def _inproj(body, h, w, bias, parts, tn, out_dtypes, name):
    m, k = h.shape
    n = w.shape[1] // parts
    nb = n // tn
    tm = _TM
    in_specs = [pl.BlockSpec((tm, k), lambda j, i: (i, 0))]
    operands = [h]
    for p in range(parts):
        in_specs.append(pl.BlockSpec((k, tn), functools.partial(lambda j, i, p: (0, j + p * nb), p=p)))
        operands.append(w)
    if bias is not None:
        for p in range(parts):
            in_specs.append(pl.BlockSpec((1, tn), functools.partial(lambda j, i, p: (0, j + p * nb), p=p)))
            operands.append(bias.reshape(1, parts * n))
    windows = [((tm, k), _BF16)] + [((k, tn), _BF16)] * parts + [((tm, tn), dt) for dt in out_dtypes]
    outs = pl.pallas_call(
        body,
        grid=(nb, m // tm),
        in_specs=in_specs,
        out_specs=[pl.BlockSpec((tm, tn), lambda j, i: (i, j)) for _ in out_dtypes],
        out_shape=[jax.ShapeDtypeStruct((m, n), dt) for dt in out_dtypes],
        compiler_params=_params(("arbitrary", "arbitrary"), windows),
        name=name,
    )(*operands)
    return outs


def _mix_gmlp_kernel(groups, chunk, u_ref, v_ref, lng_ref, lnb_ref, ws_ref, bias_ref, y_ref, vn_ref, vnb_ref):
    tm, d = y_ref.shape
    gd = d // groups

    def normalise(rows):
        vn = _ln_rows(v_ref[rows, :].astype(_F32), lng_ref[...], lnb_ref[...])
        vn_ref[rows, :] = vn
        vnb_ref[rows, :] = vn.astype(vnb_ref.dtype)
    _for_row_blocks(tm, normalise)

    def mix_chunk(c, carry):
        rows = pl.ds(pl.multiple_of(c * chunk, chunk), chunk)
        for g in range(groups):
            cols = slice(g * gd, (g + 1) * gd)
            mixed = _dot(ws_ref[0, g], vnb_ref[rows, cols]) + bias_ref[0, :, cols]
            y_ref[rows, cols] = (u_ref[rows, cols].astype(_F32) * mixed).astype(y_ref.dtype)
        return carry
    lax.fori_loop(0, tm // chunk, mix_chunk, 0)


def _mix_gmlp(z, ln_g, ln_b, ws_all, bias_all, n_prompt, n_sample):
    m, d2 = z.shape
    d = d2 // 2
    tm = _TM_MIX
    n_pt = n_prompt // tm
    groups, chunk = ws_all.shape[1], ws_all.shape[2]
    s_tiles = n_sample // tm
    windows = [((tm, d), _BF16)] * 3 + [((tm, d), _F32), ((1, groups, chunk, chunk), _BF16), ((1, chunk, d), _F32)]
    y, vn = pl.pallas_call(
        functools.partial(_mix_gmlp_kernel, groups, chunk),
        grid=(m // tm,),
        in_specs=[
            pl.BlockSpec((tm, d), lambda i: (i, 0)),
            pl.BlockSpec((tm, d), lambda i: (i, 1)),
            pl.BlockSpec((1, d), lambda i: (0, 0)),
            pl.BlockSpec((1, d), lambda i: (0, 0)),
            pl.BlockSpec((1, groups, chunk, chunk), lambda i: (jnp.where(i < n_pt, 0, 1), 0, 0, 0)),
            pl.BlockSpec((1, chunk, d), lambda i: (jnp.where(i < n_pt, 0, 1), 0, 0)),
        ],
        out_specs=[
            pl.BlockSpec((tm, d), lambda i: (i, 0)),
            pl.BlockSpec((tm, d), lambda i: (jnp.maximum(i - n_pt, 0), 0)),
        ],
        out_shape=[jax.ShapeDtypeStruct((m, d), _BF16), jax.ShapeDtypeStruct((s_tiles * tm, d), _F32)],
        scratch_shapes=[pltpu.VMEM((tm, d), _BF16)],
        compiler_params=_params(("arbitrary",), windows, _nbytes((tm, d), _BF16)),
        name="mix_gmlp",
    )(z, z, ln_g.reshape(1, d), ln_b.reshape(1, d), ws_all, bias_all)
    return y, vn


def _mix_conv3_kernel(n_pt, seq, dseq, bg_ref, cx_ref, p1_ref, p2_ref, cw_ref, y_ref, ext_ref):
    i = pl.program_id(0)
    tm, d = y_ref.shape
    halo = _SUBLANES
    is_sample = i >= n_pt

    @pl.when(i == 0)
    def _():
        ext_ref[0:halo, :] = jnp.zeros((halo, d), _F32)

    @pl.when(i > 0)
    def _():
        ext_ref[0:halo, :] = ext_ref[tm:tm + halo, :]

    ext_ref[halo:halo + tm, :] = cx_ref[...]
    seg = jnp.where(is_sample, dseq, seq)

    def conv(rows):
        r0 = rows.start
        pos = (i * tm + r0 + lax.broadcasted_iota(jnp.int32, (_ROWS, 1), 0)) & (seg - 1)
        win = ext_ref[pl.ds(r0, halo + _ROWS), :]
        cur = win[halo:]
        prev1 = jnp.where(pos >= 1, win[halo - 1:halo - 1 + _ROWS], jnp.where(is_sample, p1_ref[rows, :], 0.0))
        prev2 = jnp.where(pos >= 2, win[halo - 2:halo - 2 + _ROWS], jnp.where(is_sample, p2_ref[rows, :], 0.0))
        acc = cw_ref[0:1, :] * prev2 + cw_ref[1:2, :] * prev1 + cw_ref[2:3, :] * cur
        y_ref[rows, :] = (bg_ref[rows, :].astype(_F32) * acc).astype(y_ref.dtype)
    _for_row_blocks(tm, conv)


def _mix_conv3(bg, cx, p1, p2, conv_w, n_prompt, seq, dseq):
    m, d = cx.shape
    tm = _TM_MIX
    n_pt = n_prompt // tm
    windows = [((tm, d), _BF16)] * 2 + [((tm, d), _F32)] * 3
    return pl.pallas_call(
        functools.partial(_mix_conv3_kernel, n_pt, seq, dseq),
        grid=(m // tm,),
        in_specs=[
            pl.BlockSpec((tm, d), lambda i: (i, 0)),
            pl.BlockSpec((tm, d), lambda i: (i, 0)),
            pl.BlockSpec((tm, d), lambda i: (jnp.maximum(i - n_pt, 0), 0)),
            pl.BlockSpec((tm, d), lambda i: (jnp.maximum(i - n_pt, 0), 0)),
            pl.BlockSpec(conv_w.shape, lambda i: (0, 0)),
        ],
        out_specs=pl.BlockSpec((tm, d), lambda i: (i, 0)),
        out_shape=jax.ShapeDtypeStruct((m, d), _BF16),
        scratch_shapes=[pltpu.VMEM((tm + _SUBLANES, d), _F32)],
        compiler_params=_params(("arbitrary",), windows, _nbytes((tm + _SUBLANES, d), _F32)),
        name="mix_conv3",
    )(bg, cx, p1, p2, conv_w)


def _depthwise_taps(width, fill, slab_of, ext_ref, cw_ref, c_ref, n_blocks, in_row, out_row):
    d = c_ref.shape[1]

    def lane_block(l, carry):
        cols = pl.ds(pl.multiple_of(l * _LANES, _LANES), _LANES)
        fill(l, cols)
        slab = slab_of(l)
        w = [jnp.broadcast_to(cw_ref[k:k + 1, cols], (_SUBLANES, _LANES)) for k in range(width)]

        def row_block(b, carry2):
            base = in_row(b)
            acc = w[0] * ext_ref[slab, pl.ds(base, _SUBLANES), :]
            for k in range(1, width):
                acc = acc + w[k] * ext_ref[slab, pl.ds(base + k, _SUBLANES), :]
            c_ref[pl.ds(pl.multiple_of(out_row(b), _SUBLANES), _SUBLANES), cols] = acc
            return carry2
        lax.fori_loop(0, n_blocks, row_block, 0, unroll=2)
        return carry
    lax.fori_loop(0, d // _LANES, lane_block, 0)


def _conformer_tail(c_ref, cb_ref, lng_ref, lnb_ref, y_ref):
    def tail(rows):
        c = _ln_rows(c_ref[rows, :] + cb_ref[...], lng_ref[...], lnb_ref[...])
        y_ref[rows, :] = (c * jax.nn.sigmoid(c)).astype(y_ref.dtype)
    _for_row_blocks(y_ref.shape[0], tail)


def _mix_conv31_prompt_kernel(width, seq_tiles, glu_ref, cw_ref, cb_ref, lng_ref, lnb_ref, y_ref, ext_ref, c_ref):
    i = pl.program_id(0)
    tm, d = y_ref.shape
    halo = _CONV_HALO

    def fill(l, cols):
        @pl.when(lax.rem(i, seq_tiles) == 0)
        def _():
            ext_ref[l, 0:halo, :] = jnp.zeros((halo, _LANES), _F32)

        @pl.when(lax.rem(i, seq_tiles) != 0)
        def _():
            ext_ref[l, 0:halo, :] = ext_ref[l, tm:tm + halo, :]

        ext_ref[l, halo:halo + tm, :] = glu_ref[:, cols]

    lead = halo - (width - 1)
    _depthwise_taps(width, fill, lambda l: l, ext_ref, cw_ref, c_ref, tm // _SUBLANES,
                    lambda b: b * _SUBLANES + lead, lambda b: b * _SUBLANES)
    _conformer_tail(c_ref, cb_ref, lng_ref, lnb_ref, y_ref)


def _mix_conv31_sample_kernel(width, dseq, past_ref, cur_ref, cw_ref, cb_ref, lng_ref, lnb_ref, y_hbm_ref, y_ref,
                              ext_ref, c_ref):
    del y_hbm_ref
    halo = _CONV_HALO
    slab = halo + dseq
    n_seq = cur_ref.shape[0] // dseq

    def fill(l, cols):
        def one_sequence(s, carry):
            ext_ref[0, pl.ds(pl.multiple_of(s * slab, _SUBLANES), halo), :] = (
                past_ref[pl.ds(pl.multiple_of(s * halo, _SUBLANES), halo), cols])
            ext_ref[0, pl.ds(pl.multiple_of(s * slab + halo, _SUBLANES), dseq), :] = (
                cur_ref[pl.ds(pl.multiple_of(s * dseq, _SUBLANES), dseq), cols])
            return carry
        lax.fori_loop(0, n_seq, one_sequence, 0)

    lead = halo - (width - 1)
    _depthwise_taps(width, fill, lambda l: 0, ext_ref, cw_ref, c_ref, n_seq,
                    lambda s: s * slab + lead, lambda s: s * dseq)
    _conformer_tail(c_ref, cb_ref, lng_ref, lnb_ref, y_ref)


def _mix_conv31(glu, past, conv_w, conv_b, ln_g, ln_b, n_prompt, seq, dseq):
    m, d = glu.shape
    width = conv_w.shape[0]
    assert width - 1 <= _CONV_HALO and dseq == _SUBLANES
    tm = _TM_MIX
    small = [pl.BlockSpec(conv_w.shape, lambda i: (0, 0))] + [pl.BlockSpec((1, d), lambda i: (0, 0))] * 3
    small_ops = (conv_w, conv_b.reshape(1, d), ln_g.reshape(1, d), ln_b.reshape(1, d))
    ext_bytes = _nbytes((tm + _CONV_HALO, d), _F32) + _nbytes((tm, d), _F32)
    y = pl.pallas_call(
        functools.partial(_mix_conv31_prompt_kernel, width, seq // tm),
        grid=(n_prompt // tm,),
        in_specs=[pl.BlockSpec((tm, d), lambda i: (i, 0))] + small,
        out_specs=pl.BlockSpec((tm, d), lambda i: (i, 0)),
        out_shape=jax.ShapeDtypeStruct((m, d), _BF16),
        scratch_shapes=[pltpu.VMEM((d // _LANES, tm + _CONV_HALO, _LANES), _F32), pltpu.VMEM((tm, d), _F32)],
        compiler_params=_params(("arbitrary",), [((tm, d), _F32), ((tm, d), _BF16)], ext_bytes),
        name="mix_conv31_prompt",
    )(glu, *small_ops)

    n_sample = m - n_prompt
    ts = 256
    n_seq = ts // dseq
    slab = _CONV_HALO + dseq
    past2d = jnp.pad(past, ((0, 0), (_CONV_HALO - (width - 1), 0), (0, 0))).reshape(-1, d)
    s_bytes = _nbytes((n_seq * slab, _LANES), _F32) + _nbytes((ts, d), _F32)
    return pl.pallas_call(
        functools.partial(_mix_conv31_sample_kernel, width, dseq),
        grid=(n_sample // ts,),
        in_specs=[pl.BlockSpec((n_seq * _CONV_HALO, d), lambda i: (i, 0)),
                  pl.BlockSpec((ts, d), lambda i: (i + n_prompt // ts, 0))] + small
                 + [pl.BlockSpec(memory_space=pl.ANY)],
        out_specs=pl.BlockSpec((ts, d), lambda i: (i + n_prompt // ts, 0)),
        out_shape=jax.ShapeDtypeStruct((m, d), _BF16),
        scratch_shapes=[pltpu.VMEM((1, n_seq * slab, _LANES), _F32), pltpu.VMEM((ts, d), _F32)],
        input_output_aliases={6: 0},
        compiler_params=_params(("arbitrary",), [((n_seq * _CONV_HALO, d), _F32), ((ts, d), _F32), ((ts, d), _BF16)],
                                s_bytes),
        name="mix_conv31_sample",
    )(past2d, glu, *small_ops, y)


def _outproj_kernel(has_bias, y_ref, w_ref, *refs):
    if has_bias:
        b_ref, x_ref, g_ref, xo_ref, h_ref = refs
    else:
        x_ref, g_ref, xo_ref, h_ref = refs
    acc = _dot(y_ref[...], w_ref[...])
    if has_bias:
        acc = acc + b_ref[...]
    xo_ref[...] = x_ref[...] + acc
    _for_row_blocks(xo_ref.shape[0], lambda rows: h_ref.__setitem__(
        (rows, slice(None)), _rms_rows(xo_ref[rows, :], g_ref[...]).astype(h_ref.dtype)))


def _outproj(y, w, bias, x, g):
    m, k = y.shape
    n = w.shape[1]
    tm = _TM_MIX
    row = pl.BlockSpec((1, n), lambda i: (0, 0))
    in_specs = [pl.BlockSpec((tm, k), lambda i: (i, 0)), pl.BlockSpec((k, n), lambda i: (0, 0))]
    operands = [y, w]
    if bias is not None:
        in_specs.append(row)
        operands.append(bias.reshape(1, n))
    in_specs += [pl.BlockSpec((tm, n), lambda i: (i, 0)), row]
    operands += [x, g.reshape(1, n)]
    windows = [((tm, k), _BF16), ((k, n), _BF16), ((tm, n), _F32), ((tm, n), _F32), ((tm, n), _BF16)]
    return pl.pallas_call(
        functools.partial(_outproj_kernel, bias is not None),
        grid=(m // tm,),
        in_specs=in_specs,
        out_specs=[pl.BlockSpec((tm, n), lambda i: (i, 0)), pl.BlockSpec((tm, n), lambda i: (i, 0))],
        out_shape=[jax.ShapeDtypeStruct((m, n), _F32), jax.ShapeDtypeStruct((m, n), _BF16)],
        compiler_params=_params(("arbitrary",), windows),
        name="outproj",
    )(*operands)


def _ffn_kernel(final, h_ref, wu_ref, wd_ref, x_ref, g_ref, *refs):
    if final:
        yo_ref, acc_ref = refs
    else:
        xo_ref, hn_ref, acc_ref = refs
    f = pl.program_id(1)

    @pl.when(f == 0)
    def _():
        acc_ref[...] = jnp.zeros(acc_ref.shape, _F32)

    a = jnp.maximum(_dot(h_ref[...], wu_ref[...]), 0.0)
    acc_ref[...] += _dot((a * a).astype(_BF16), wd_ref[...])

    @pl.when(f == pl.num_programs(1) - 1)
    def _():
        def finish(rows):
            xn = x_ref[rows, :] + acc_ref[rows, :]
            normed = _rms_rows(xn, g_ref[...])
            if final:
                yo_ref[rows, :] = normed
            else:
                xo_ref[rows, :] = xn
                hn_ref[rows, :] = normed.astype(hn_ref.dtype)
        _for_row_blocks(x_ref.shape[0], finish)


def _ffn(h, w_up, w_down, x, g, final):
    m, d = x.shape
    dff = w_up.shape[1]
    tm, tf = _TM_MIX, _TF
    tile = pl.BlockSpec((tm, d), lambda i, f: (i, 0))
    in_specs = [tile, pl.BlockSpec((d, tf), lambda i, f: (0, f)), pl.BlockSpec((tf, d), lambda i, f: (f, 0)), tile,
                pl.BlockSpec((1, d), lambda i, f: (0, 0))]
    windows = [((tm, d), _BF16), ((d, tf), _BF16), ((tf, d), _BF16), ((tm, d), _F32)]
    if final:
        out_specs = tile
        out_shape = jax.ShapeDtypeStruct((m, d), _F32)
        windows += [((tm, d), _F32)]
    else:
        out_specs = [tile, tile]
        out_shape = [jax.ShapeDtypeStruct((m, d), _F32), jax.ShapeDtypeStruct((m, d), _BF16)]
        windows += [((tm, d), _F32), ((tm, d), _BF16)]
    return pl.pallas_call(
        functools.partial(_ffn_kernel, final),
        grid=(m // tm, dff // tf),
        in_specs=in_specs,
        out_specs=out_specs,
        out_shape=out_shape,
        scratch_shapes=[pltpu.VMEM((tm, d), _F32)],
        compiler_params=_params(("arbitrary", "arbitrary"), windows, _nbytes((tm, d), _F32)),
        name="ffn_final" if final else "ffn",
    )(h, w_up, w_down, x, g.reshape(1, d))


def _gating_operands(w_s, b_s, dseq, d):
    groups, chunk, _ = w_s.shape
    t = jnp.arange(chunk)
    causal = t[:, None] >= t[None, :]
    same_seq = (t[:, None] // dseq) == (t[None, :] // dseq)
    reps = chunk // dseq
    ws_prompt = jnp.where(causal, w_s, 0)
    ws_sample = jnp.where(causal & same_seq, jnp.tile(w_s[:, :dseq, :dseq], (1, reps, reps)), 0)
    gd = d // groups
    bias_prompt = jnp.repeat(b_s.T, gd, axis=1)
    bias_sample = jnp.repeat(jnp.tile(b_s[:, :dseq].T, (reps, 1)), gd, axis=1)
    return jnp.stack([ws_prompt, ws_sample]).astype(_BF16), jnp.stack([bias_prompt, bias_sample])


def _conv_state(past, cur):
    keep = past.shape[1]
    return jnp.concatenate([past, cur], axis=1)[:, -keep:]


def kernel(x_prompt, x_sample, state_b_conv, state_c_conv, norm_mix_g, norm_ffn_g, final_norm_g, a_w_in, a_ln_g, a_ln_b, a_w_s, a_b_s, a_w_out, b_w_in, b_conv_w, b_w_out, c_w_pw1, c_b_pw1, c_conv_w, c_conv_b, c_ln_g, c_ln_b, c_w_pw2, c_b_pw2, ffn_w_up, ffn_w_down):
    bsz, seq, d = x_prompt.shape
    dbsz, dseq, _ = x_sample.shape
    n_p, n_s = bsz * seq, dbsz * dseq
    depth = norm_mix_g.shape[0]
    chunk = a_w_s.shape[-1]
    assert seq % chunk == 0 and chunk % dseq == 0 and seq & (seq - 1) == 0 and dseq & (dseq - 1) == 0
    assert b_conv_w.shape[1] == 3 and seq % _TM_MIX == 0 and n_s % _TM_MIX == 0 and (n_p + n_s) % _TM == 0

    x = jnp.concatenate([x_prompt.reshape(n_p, d), x_sample.reshape(n_s, d)], axis=0)
    h = _rms_norm(x, norm_mix_g[0])
    a_v, b_prompt, b_sample, c_prompt, c_sample = [], [], [], [], []
    for i in range(depth):
        j, kind = divmod(i, 3)
        g_ffn = norm_ffn_g[i]
        if kind == 0:
            (z,) = _inproj(_inproj_gelu_kernel, h, a_w_in[j].astype(_BF16), None, 1, 1024, [_BF16], "inproj_gelu")
            ws_all, bias_all = _gating_operands(a_w_s[j], a_b_s[j], dseq, d)
            y, vn = _mix_gmlp(z, a_ln_g[j], a_ln_b[j], ws_all, bias_all, n_p, n_s)
            a_v.append(vn.reshape(dbsz, dseq, d))
            x, h = _outproj(y, a_w_out[j].astype(_BF16), None, x, g_ffn)
        elif kind == 1:
            bg, cx = _inproj(_inproj_gate3_kernel, h, b_w_in[j].astype(_BF16), None, 3, 512, [_BF16, _F32],
                             "inproj_gate3")
            past = state_b_conv[j]
            p2 = jnp.pad(past, ((0, 0), (0, dseq - 2), (0, 0))).reshape(n_s, d)
            p1 = jnp.pad(past[:, 1:], ((0, 0), (0, dseq - 1), (0, 0))).reshape(n_s, d)
            y = _mix_conv3(bg, cx, p1, p2, b_conv_w[j], n_p, seq, dseq)
            b_prompt.append(cx[:n_p].reshape(bsz, seq, d)[:, -past.shape[1]:])
            b_sample.append(_conv_state(past, cx[n_p:].reshape(dbsz, dseq, d)))
            x, h = _outproj(y, b_w_out[j].astype(_BF16), None, x, g_ffn)
        else:
            (glu,) = _inproj(_inproj_glu_kernel, h, c_w_pw1[j].astype(_BF16), c_b_pw1[j], 2, 512, [_F32],
                             "inproj_glu")
            past = state_c_conv[j]
            y = _mix_conv31(glu, past, c_conv_w[j], c_conv_b[j], c_ln_g[j], c_ln_b[j], n_p, seq, dseq)
            c_prompt.append(glu[:n_p].reshape(bsz, seq, d)[:, -past.shape[1]:])
            c_sample.append(_conv_state(past, glu[n_p:].reshape(dbsz, dseq, d)))
            x, h = _outproj(y, c_w_pw2[j].astype(_BF16), c_b_pw2[j], x, g_ffn)
        w_up, w_down = ffn_w_up[i].astype(_BF16), ffn_w_down[i].astype(_BF16)
        if i + 1 < depth:
            x, h = _ffn(h, w_up, w_down, x, norm_mix_g[i + 1], final=False)
        else:
            y_all = _ffn(h, w_up, w_down, x, final_norm_g, final=True)
    return (y_all[:n_p].reshape(bsz, seq, d), y_all[n_p:].reshape(dbsz, dseq, d), jnp.stack(a_v),
            jnp.stack(b_prompt), jnp.stack(b_sample), jnp.stack(c_prompt), jnp.stack(c_sample))
```

```python
import functools

import jax
import jax.numpy as jnp
from jax import lax
from jax.experimental import pallas as pl
from jax.experimental.pallas import tpu as pltpu

_F32 = jnp.float32
_BF16 = jnp.bfloat16
_EPS = 1e-6

_LANES = 128
_SUBLANES = 8
_VMEM_CAP_BYTES = 60000 * 1024
_VMEM_SLACK_BYTES = 12 * 1024 * 1024

_TM = 1024
_TM_MIX = 512
_TN_OUT = 1024
_TF = 512
_ROWS = 64
_CONV_HALO = 32


def _nbytes(shape, dtype):
    n = 1
    for s in shape:
        n *= s
    return n * jnp.dtype(dtype).itemsize


def _params(semantics, windows, scratch=0):
    limit = 2 * sum(_nbytes(s, d) for s, d in windows) + scratch + _VMEM_SLACK_BYTES
    return pltpu.CompilerParams(dimension_semantics=semantics, vmem_limit_bytes=min(limit, _VMEM_CAP_BYTES))


def _rms_rows(x, g):
    return x * lax.rsqrt(jnp.mean(x * x, axis=-1, keepdims=True) + _EPS) * g


def _ln_rows(x, g, b):
    xc = x - jnp.mean(x, axis=-1, keepdims=True)
    return xc * lax.rsqrt(jnp.mean(xc * xc, axis=-1, keepdims=True) + _EPS) * g + b


def _for_row_blocks(n_rows, fn):
    def body(r, carry):
        fn(pl.ds(pl.multiple_of(r * _ROWS, _ROWS), _ROWS))
        return carry
    lax.fori_loop(0, n_rows // _ROWS, body, 0)


def _dot(a, b):
    return jnp.dot(a, b, preferred_element_type=_F32)


def _gelu_epilogue(accs, out_ref):
    (acc,) = accs
    out_ref[...] = (0.5 * acc * (1.0 + lax.erf(acc * 0.7071067811865476))).astype(out_ref.dtype)


def _plain_epilogue(accs, out_ref):
    (acc,) = accs
    out_ref[...] = acc.astype(out_ref.dtype)


def _product_epilogue(accs, out_ref):
    out_ref[...] = (accs[0] * accs[1]).astype(out_ref.dtype)


def _glu_epilogue(accs, out_ref):
    out_ref[...] = (accs[0] * jax.nn.sigmoid(accs[1])).astype(out_ref.dtype)


def _inproj_kernel(epilogue, parts, has_bias, x_ref, g_ref, *refs):
    w_refs, refs = refs[:parts], refs[parts:]
    b_refs = refs[:parts] if has_bias else ()
    out_ref, h_ref = refs[len(b_refs):]

    @pl.when(pl.program_id(1) == 0)
    def _():
        def norm(rows):
            h_ref[rows, :] = _rms_rows(x_ref[rows, :], g_ref[...]).astype(h_ref.dtype)
        _for_row_blocks(h_ref.shape[0], norm)

    h = h_ref[...]
    accs = [_dot(h, w_ref[...].astype(_BF16)) for w_ref in w_refs]
    if has_bias:
        accs = [acc + b_ref[...] for acc, b_ref in zip(accs, b_refs)]
    epilogue(accs, out_ref)


def _inproj(epilogue, x, g, g_layer, w, bias, layer, first_part, parts, n, tn, out_dtype, name):
    m, k = x.shape
    nb = n // tn
    tm = _TM

    def part_block(p):
        return lambda i, j: (layer, 0, (first_part + p) * nb + j)
    in_specs = [pl.BlockSpec((tm, k), lambda i, j: (i, 0)), pl.BlockSpec((None, 1, k), lambda i, j: (g_layer, 0, 0))]
    operands = [x, g.reshape(g.shape[0], 1, k)]
    for p in range(parts):
        in_specs.append(pl.BlockSpec((None, k, tn), part_block(p)))
        operands.append(w)
    if bias is not None:
        for p in range(parts):
            in_specs.append(pl.BlockSpec((None, 1, tn), part_block(p)))
            operands.append(bias.reshape(bias.shape[0], 1, -1))
    windows = [((tm, k), _F32)] + [((k, tn), _F32)] * parts + [((tm, tn), out_dtype)]
    return pl.pallas_call(
        functools.partial(_inproj_kernel, epilogue, parts, bias is not None),
        grid=(m // tm, nb),
        in_specs=in_specs,
        out_specs=pl.BlockSpec((tm, tn), lambda i, j: (i, j)),
        out_shape=jax.ShapeDtypeStruct((m, n), out_dtype),
        scratch_shapes=[pltpu.VMEM((tm, k), _BF16)],
        compiler_params=_params(("arbitrary", "arbitrary"), windows, _nbytes((tm, k), _BF16)),
        name=name,
    )(*operands)


def _mix_gmlp_kernel(groups, chunk, u_ref, v_ref, lng_ref, lnb_ref, ws_ref, bias_ref, y_ref, vn_ref, vnb_ref):
    tm, d = y_ref.shape
    gd = d // groups

    def normalise(rows):
        vn = _ln_rows(v_ref[rows, :].astype(_F32), lng_ref[...], lnb_ref[...])
        vn_ref[rows, :] = vn
        vnb_ref[rows, :] = vn.astype(vnb_ref.dtype)
    _for_row_blocks(tm, normalise)

    def mix_chunk(c, carry):
        rows = pl.ds(pl.multiple_of(c * chunk, chunk), chunk)
        for g in range(groups):
            cols = slice(g * gd, (g + 1) * gd)
            mixed = _dot(ws_ref[0, g], vnb_ref[rows, cols]) + bias_ref[0, :, cols]
            y_ref[rows, cols] = (u_ref[rows, cols].astype(_F32) * mixed).astype(y_ref.dtype)
        return carry
    lax.fori_loop(0, tm // chunk, mix_chunk, 0)


def _mix_gmlp(z, ln_g, ln_b, ws_all, bias_all, n_prompt, n_sample):
    m, d2 = z.shape
    d = d2 // 2
    tm = _TM_MIX
    n_pt = n_prompt // tm
    groups, chunk = ws_all.shape[1], ws_all.shape[2]
    s_tiles = n_sample // tm
    windows = [((tm, d), _BF16)] * 3 + [((tm, d), _F32), ((1, groups, chunk, chunk), _BF16), ((1, chunk, d), _F32)]
    y, vn = pl.pallas_call(
        functools.partial(_mix_gmlp_kernel, groups, chunk),
        grid=(m // tm,),
        in_specs=[
            pl.BlockSpec((tm, d), lambda i: (i, 0)),
            pl.BlockSpec((tm, d), lambda i: (i, 1)),
            pl.BlockSpec((1, d), lambda i: (0, 0)),
            pl.BlockSpec((1, d), lambda i: (0, 0)),
            pl.BlockSpec((1, groups, chunk, chunk), lambda i: (jnp.where(i < n_pt, 0, 1), 0, 0, 0)),
            pl.BlockSpec((1, chunk, d), lambda i: (jnp.where(i < n_pt, 0, 1), 0, 0)),
        ],
        out_specs=[
            pl.BlockSpec((tm, d), lambda i: (i, 0)),
            pl.BlockSpec((tm, d), lambda i: (jnp.maximum(i - n_pt, 0), 0)),
        ],
        out_shape=[jax.ShapeDtypeStruct((m, d), _BF16), jax.ShapeDtypeStruct((s_tiles * tm, d), _F32)],
        scratch_shapes=[pltpu.VMEM((tm, d), _BF16)],
        compiler_params=_params(("arbitrary",), windows, _nbytes((tm, d), _BF16)),
        name="mix_gmlp",
    )(z, z, ln_g.reshape(1, d), ln_b.reshape(1, d), ws_all, bias_all)
    return y, vn


def _mix_conv3_kernel(n_pt, seq, dseq, bg_ref, cx_ref, p1_ref, p2_ref, cw_ref, y_ref, ext_ref):
    i = pl.program_id(0)
    tm, d = y_ref.shape
    halo = _SUBLANES
    is_sample = i >= n_pt

    @pl.when(i == 0)
    def _():
        ext_ref[0:halo, :] = jnp.zeros((halo, d), _F32)

    @pl.when(i > 0)
    def _():
        ext_ref[0:halo, :] = ext_ref[tm:tm + halo, :]

    ext_ref[halo:halo + tm, :] = cx_ref[...]
    seg = jnp.where(is_sample, dseq, seq)

    def conv(rows):
        r0 = rows.start
        pos = (i * tm + r0 + lax.broadcasted_iota(jnp.int32, (_ROWS, 1), 0)) & (seg - 1)
        win = ext_ref[pl.ds(r0, halo + _ROWS), :]
        cur = win[halo:]
        prev1 = jnp.where(pos >= 1, win[halo - 1:halo - 1 + _ROWS], jnp.where(is_sample, p1_ref[rows, :], 0.0))
        prev2 = jnp.where(pos >= 2, win[halo - 2:halo - 2 + _ROWS], jnp.where(is_sample, p2_ref[rows, :], 0.0))
        acc = cw_ref[0:1, :] * prev2 + cw_ref[1:2, :] * prev1 + cw_ref[2:3, :] * cur
        y_ref[rows, :] = (bg_ref[rows, :].astype(_F32) * acc).astype(y_ref.dtype)
    _for_row_blocks(tm, conv)


def _mix_conv3(bg, cx, p1, p2, conv_w, n_prompt, seq, dseq):
    m, d = cx.shape
    tm = _TM_MIX
    n_pt = n_prompt // tm
    windows = [((tm, d), _BF16)] * 2 + [((tm, d), _F32)] * 3
    return pl.pallas_call(
        functools.partial(_mix_conv3_kernel, n_pt, seq, dseq),
        grid=(m // tm,),
        in_specs=[
            pl.BlockSpec((tm, d), lambda i: (i, 0)),
            pl.BlockSpec((tm, d), lambda i: (i, 0)),
            pl.BlockSpec((tm, d), lambda i: (jnp.maximum(i - n_pt, 0), 0)),
            pl.BlockSpec((tm, d), lambda i: (jnp.maximum(i - n_pt, 0), 0)),
            pl.BlockSpec(conv_w.shape, lambda i: (0, 0)),
        ],
        out_specs=pl.BlockSpec((tm, d), lambda i: (i, 0)),
        out_shape=jax.ShapeDtypeStruct((m, d), _BF16),
        scratch_shapes=[pltpu.VMEM((tm + _SUBLANES, d), _F32)],
        compiler_params=_params(("arbitrary",), windows, _nbytes((tm + _SUBLANES, d), _F32)),
        name="mix_conv3",
    )(bg, cx, p1, p2, conv_w)


def _depthwise_taps(width, fill, slab_of, ext_ref, cw_ref, c_ref, n_blocks, in_row, out_row):
    d = c_ref.shape[1]

    def lane_block(l, carry):
        cols = pl.ds(pl.multiple_of(l * _LANES, _LANES), _LANES)
        fill(l, cols)
        slab = slab_of(l)
        w = [jnp.broadcast_to(cw_ref[k:k + 1, cols], (_SUBLANES, _LANES)) for k in range(width)]

        def row_block(b, carry2):
            base = in_row(b)
            acc = w[0] * ext_ref[slab, pl.ds(base, _SUBLANES), :]
            for k in range(1, width):
                acc = acc + w[k] * ext_ref[slab, pl.ds(base + k, _SUBLANES), :]
            c_ref[pl.ds(pl.multiple_of(out_row(b), _SUBLANES), _SUBLANES), cols] = acc
            return carry2
        lax.fori_loop(0, n_blocks, row_block, 0, unroll=2)
        return carry
    lax.fori_loop(0, d // _LANES, lane_block, 0)


def _conformer_tail(c_ref, cb_ref, lng_ref, lnb_ref, y_ref):
    def tail(rows):
        c = _ln_rows(c_ref[rows, :] + cb_ref[...], lng_ref[...], lnb_ref[...])
        y_ref[rows, :] = (c * jax.nn.sigmoid(c)).astype(y_ref.dtype)
    _for_row_blocks(y_ref.shape[0], tail)


def _mix_conv31_prompt_kernel(width, seq_tiles, glu_ref, cw_ref, cb_ref, lng_ref, lnb_ref, y_ref, ext_ref, c_ref):
    i = pl.program_id(0)
    tm, d = y_ref.shape
    halo = _CONV_HALO

    def fill(l, cols):
        @pl.when(lax.rem(i, seq_tiles) == 0)
        def _():
            ext_ref[l, 0:halo, :] = jnp.zeros((halo, _LANES), _F32)

        @pl.when(lax.rem(i, seq_tiles) != 0)
        def _():
            ext_ref[l, 0:halo, :] = ext_ref[l, tm:tm + halo, :]

        ext_ref[l, halo:halo + tm, :] = glu_ref[:, cols]

    lead = halo - (width - 1)
    _depthwise_taps(width, fill, lambda l: l, ext_ref, cw_ref, c_ref, tm // _SUBLANES,
                    lambda b: b * _SUBLANES + lead, lambda b: b * _SUBLANES)
    _conformer_tail(c_ref, cb_ref, lng_ref, lnb_ref, y_ref)


def _mix_conv31_sample_kernel(width, dseq, past_ref, cur_ref, cw_ref, cb_ref, lng_ref, lnb_ref, y_hbm_ref, y_ref,
                              ext_ref, c_ref):
    del y_hbm_ref
    halo = _CONV_HALO
    slab = halo + dseq
    n_seq = cur_ref.shape[0] // dseq

    def fill(l, cols):
        def one_sequence(s, carry):
            ext_ref[0, pl.ds(pl.multiple_of(s * slab, _SUBLANES), halo), :] = (
                past_ref[pl.ds(pl.multiple_of(s * halo, _SUBLANES), halo), cols])
            ext_ref[0, pl.ds(pl.multiple_of(s * slab + halo, _SUBLANES), dseq), :] = (
                cur_ref[pl.ds(pl.multiple_of(s * dseq, _SUBLANES), dseq), cols])
            return carry
        lax.fori_loop(0, n_seq, one_sequence, 0)

    lead = halo - (width - 1)
    _depthwise_taps(width, fill, lambda l: 0, ext_ref, cw_ref, c_ref, n_seq,
                    lambda s: s * slab + lead, lambda s: s * dseq)
    _conformer_tail(c_ref, cb_ref, lng_ref, lnb_ref, y_ref)


def _mix_conv31(glu, past, conv_w, conv_b, ln_g, ln_b, n_prompt, seq, dseq):
    m, d = glu.shape
    width = conv_w.shape[0]
    assert width - 1 <= _CONV_HALO and dseq == _SUBLANES
    tm = _TM_MIX
    small = [pl.BlockSpec(conv_w.shape, lambda i: (0, 0))] + [pl.BlockSpec((1, d), lambda i: (0, 0))] * 3
    small_ops = (conv_w, conv_b.reshape(1, d), ln_g.reshape(1, d), ln_b.reshape(1, d))
    ext_bytes = _nbytes((tm + _CONV_HALO, d), _F32) + _nbytes((tm, d), _F32)
    y = pl.pallas_call(
        functools.partial(_mix_conv31_prompt_kernel, width, seq // tm),
        grid=(n_prompt // tm,),
        in_specs=[pl.BlockSpec((tm, d), lambda i: (i, 0))] + small,
        out_specs=pl.BlockSpec((tm, d), lambda i: (i, 0)),
        out_shape=jax.ShapeDtypeStruct((m, d), _BF16),
        scratch_shapes=[pltpu.VMEM((d // _LANES, tm + _CONV_HALO, _LANES), _F32), pltpu.VMEM((tm, d), _F32)],
        compiler_params=_params(("arbitrary",), [((tm, d), _F32), ((tm, d), _BF16)], ext_bytes),
        name="mix_conv31_prompt",
    )(glu, *small_ops)

    n_sample = m - n_prompt
    ts = 256
    n_seq = ts // dseq
    slab = _CONV_HALO + dseq
    past2d = jnp.pad(past, ((0, 0), (_CONV_HALO - (width - 1), 0), (0, 0))).reshape(-1, d)
    s_bytes = _nbytes((n_seq * slab, _LANES), _F32) + _nbytes((ts, d), _F32)
    return pl.pallas_call(
        functools.partial(_mix_conv31_sample_kernel, width, dseq),
        grid=(n_sample // ts,),
        in_specs=[pl.BlockSpec((n_seq * _CONV_HALO, d), lambda i: (i, 0)),
                  pl.BlockSpec((ts, d), lambda i: (i + n_prompt // ts, 0))] + small
                 + [pl.BlockSpec(memory_space=pl.ANY)],
        out_specs=pl.BlockSpec((ts, d), lambda i: (i + n_prompt // ts, 0)),
        out_shape=jax.ShapeDtypeStruct((m, d), _BF16),
        scratch_shapes=[pltpu.VMEM((1, n_seq * slab, _LANES), _F32), pltpu.VMEM((ts, d), _F32)],
        input_output_aliases={6: 0},
        compiler_params=_params(("arbitrary",), [((n_seq * _CONV_HALO, d), _F32), ((ts, d), _F32), ((ts, d), _BF16)],
                                s_bytes),
        name="mix_conv31_sample",
    )(past2d, glu, *small_ops, y)


def _outproj_kernel(has_bias, y_ref, w_ref, *refs):
    if has_bias:
        b_ref, x_ref, xo_ref = refs
    else:
        x_ref, xo_ref = refs
    acc = _dot(y_ref[...], w_ref[...].astype(_BF16))
    if has_bias:
        acc = acc + b_ref[...]
    xo_ref[...] = x_ref[...] + acc


def _outproj(y, layer, w, bias, x):
    m, k = y.shape
    n = w.shape[2]
    tm, tn = _TM, _TN_OUT
    in_specs = [pl.BlockSpec((tm, k), lambda i, j: (i, 0)), pl.BlockSpec((None, k, tn), lambda i, j: (layer, 0, j))]
    operands = [y, w]
    if bias is not None:
        in_specs.append(pl.BlockSpec((None, 1, tn), lambda i, j: (layer, 0, j)))
        operands.append(bias.reshape(bias.shape[0], 1, n))
    in_specs.append(pl.BlockSpec((tm, tn), lambda i, j: (i, j)))
    operands.append(x)
    windows = [((tm, k), _BF16), ((k, tn), _F32), ((tm, tn), _F32), ((tm, tn), _F32)]
    return pl.pallas_call(
        functools.partial(_outproj_kernel, bias is not None),
        grid=(m // tm, n // tn),
        in_specs=in_specs,
        out_specs=pl.BlockSpec((tm, tn), lambda i, j: (i, j)),
        out_shape=jax.ShapeDtypeStruct((m, n), _F32),
        compiler_params=_params(("arbitrary", "arbitrary"), windows),
        name="outproj",
    )(*operands)


def _ffn_kernel(final, x_ref, g_ref, wu_ref, wd_ref, *refs):
    if final:
        gf_ref, o_ref, h_ref = refs
    else:
        o_ref, h_ref = refs
    f = pl.program_id(1)

    @pl.when(f == 0)
    def _():
        def start(rows):
            x = x_ref[rows, :]
            o_ref[rows, :] = x
            h_ref[rows, :] = _rms_rows(x, g_ref[...]).astype(h_ref.dtype)
        _for_row_blocks(o_ref.shape[0], start)

    a = jnp.maximum(_dot(h_ref[...], wu_ref[...].astype(_BF16)), 0.0)
    o_ref[...] += _dot((a * a).astype(_BF16), wd_ref[...].astype(_BF16))

    if final:
        @pl.when(f == pl.num_programs(1) - 1)
        def _():
            def finish(rows):
                o_ref[rows, :] = _rms_rows(o_ref[rows, :], gf_ref[...])
            _for_row_blocks(o_ref.shape[0], finish)


def _ffn(x, g, layer, w_up, w_down, row0, rows, g_final=None):
    d = x.shape[1]
    dff = w_up.shape[2]
    tm, tf = _TM, _TF
    t0 = row0 // tm
    in_specs = [pl.BlockSpec((tm, d), lambda i, f: (i + t0, 0)),
                pl.BlockSpec((None, 1, d), lambda i, f: (layer, 0, 0)),
                pl.BlockSpec((None, d, tf), lambda i, f: (layer, 0, f)),
                pl.BlockSpec((None, tf, d), lambda i, f: (layer, f, 0))]
    operands = [x, g.reshape(g.shape[0], 1, d), w_up, w_down]
    if g_final is not None:
        in_specs.append(pl.BlockSpec((1, d), lambda i, f: (0, 0)))
        operands.append(g_final.reshape(1, d))
    windows = [((tm, d), _F32), ((d, tf), _F32), ((tf, d), _F32), ((tm, d), _F32)]
    return pl.pallas_call(
        functools.partial(_ffn_kernel, g_final is not None),
        grid=(rows // tm, dff // tf),
        in_specs=in_specs,
        out_specs=pl.BlockSpec((tm, d), lambda i, f: (i, 0)),
        out_shape=jax.ShapeDtypeStruct((rows, d), _F32),
        scratch_shapes=[pltpu.VMEM((tm, d), _BF16)],
        compiler_params=_params(("arbitrary", "arbitrary"), windows, _nbytes((tm, d), _BF16)),
        name="ffn",
    )(*operands)


def _gating_operands(w_s, b_s, dseq, d):
    groups, chunk, _ = w_s.shape
    t = jnp.arange(chunk)
    causal = t[:, None] >= t[None, :]
    same_seq = (t[:, None] // dseq) == (t[None, :] // dseq)
    reps = chunk // dseq
    ws_prompt = jnp.where(causal, w_s, 0)
    ws_sample = jnp.where(causal & same_seq, jnp.tile(w_s[:, :dseq, :dseq], (1, reps, reps)), 0)
    gd = d // groups
    bias_prompt = jnp.repeat(b_s.T, gd, axis=1)
    bias_sample = jnp.repeat(jnp.tile(b_s[:, :dseq].T, (reps, 1)), gd, axis=1)
    return jnp.stack([ws_prompt, ws_sample]).astype(_BF16), jnp.stack([bias_prompt, bias_sample])


def _conv_states(past, t, bsz, seq, dbsz, dseq):
    keep, d = past.shape[1], t.shape[1]
    n_p = bsz * seq
    prompt = jnp.stack([lax.slice_in_dim(t, (b + 1) * seq - keep, (b + 1) * seq, axis=0) for b in range(bsz)])
    cur = lax.slice_in_dim(t, n_p, n_p + dbsz * dseq, axis=0).reshape(dbsz, dseq, d)
    sample = jnp.concatenate([past, cur], axis=1)[:, -keep:]
    return prompt, sample


def kernel(x_prompt, x_sample, state_b_conv, state_c_conv, norm_mix_g, norm_ffn_g, final_norm_g, a_w_in, a_ln_g, a_ln_b, a_w_s, a_b_s, a_w_out, b_w_in, b_conv_w, b_w_out, c_w_pw1, c_b_pw1, c_conv_w, c_conv_b, c_ln_g, c_ln_b, c_w_pw2, c_b_pw2, ffn_w_up, ffn_w_down):
    bsz, seq, d = x_prompt.shape
    dbsz, dseq, _ = x_sample.shape
    n_p, n_s = bsz * seq, dbsz * dseq
    depth = norm_mix_g.shape[0]
    chunk = a_w_s.shape[-1]
    assert seq % chunk == 0 and chunk % dseq == 0 and seq & (seq - 1) == 0 and dseq & (dseq - 1) == 0
    assert b_conv_w.shape[1] == 3 and seq % _TM_MIX == 0 and n_s % _TM_MIX == 0
    assert n_p % _TM == 0 and n_s % _TM == 0

    x = jnp.concatenate([x_prompt.reshape(n_p, d), x_sample.reshape(n_s, d)], axis=0)
    a_v, b_prompt, b_sample, c_prompt, c_sample = [], [], [], [], []
    for i in range(depth):
        j, kind = divmod(i, 3)
        if kind == 0:
            z = _inproj(_gelu_epilogue, x, norm_mix_g, i, a_w_in, None, j, 0, 1, 2 * d, 1024, _BF16, "inproj_gelu")
            ws_all, bias_all = _gating_operands(a_w_s[j], a_b_s[j], dseq, d)
            y, vn = _mix_gmlp(z, a_ln_g[j], a_ln_b[j], ws_all, bias_all, n_p, n_s)
            a_v.append(vn.reshape(dbsz, dseq, d))
            x = _outproj(y, j, a_w_out, None, x)
        elif kind == 1:
            bg = _inproj(_plain_epilogue, x, norm_mix_g, i, b_w_in, None, j, 0, 1, d, 1024, _BF16, "inproj_gate")
            cx = _inproj(_product_epilogue, x, norm_mix_g, i, b_w_in, None, j, 1, 2, d, 512, _F32, "inproj_product")
            past = state_b_conv[j]
            p2 = jnp.pad(past, ((0, 0), (0, dseq - 2), (0, 0))).reshape(n_s, d)
            p1 = jnp.pad(past[:, 1:], ((0, 0), (0, dseq - 1), (0, 0))).reshape(n_s, d)
            y = _mix_conv3(bg, cx, p1, p2, b_conv_w[j], n_p, seq, dseq)
            prompt_state, sample_state = _conv_states(past, cx, bsz, seq, dbsz, dseq)
            b_prompt.append(prompt_state)
            b_sample.append(sample_state)
            x = _outproj(y, j, b_w_out, None, x)
        else:
            glu = _inproj(_glu_epilogue, x, norm_mix_g, i, c_w_pw1, c_b_pw1, j, 0, 2, d, 512, _F32, "inproj_glu")
            past = state_c_conv[j]
            y = _mix_conv31(glu, past, c_conv_w[j], c_conv_b[j], c_ln_g[j], c_ln_b[j], n_p, seq, dseq)
            prompt_state, sample_state = _conv_states(past, glu, bsz, seq, dbsz, dseq)
            c_prompt.append(prompt_state)
            c_sample.append(sample_state)
            x = _outproj(y, j, c_w_pw2, c_b_pw2, x)
        if i + 1 < depth:
            x = _ffn(x, norm_ffn_g, i, ffn_w_up, ffn_w_down, 0, n_p + n_s)
        else:
            y_prompt = _ffn(x, norm_ffn_g, i, ffn_w_up, ffn_w_down, 0, n_p, final_norm_g)
            y_sample = _ffn(x, norm_ffn_g, i, ffn_w_up, ffn_w_down, n_p, n_s, final_norm_g)
    return (y_prompt.reshape(bsz, seq, d), y_sample.reshape(dbsz, dseq, d), jnp.stack(a_v),
            jnp.stack(b_prompt), jnp.stack(b_sample), jnp.stack(c_prompt), jnp.stack(c_sample))
```

```python
import functools

import jax
import jax.numpy as jnp
from jax import lax
from jax.experimental import pallas as pl
from jax.experimental.pallas import tpu as pltpu

_F32 = jnp.float32
_BF16 = jnp.bfloat16
_EPS = 1e-6

_LANES = 128
_SUBLANES = 8
_VMEM_CAP_BYTES = 60000 * 1024
_VMEM_SLACK_BYTES = 12 * 1024 * 1024

_TM = 1024
_TM_MIX = 512
_TN_OUT = 1024
_TF = 512
_TM_CONV = 256
_ROWS = 16
_CONV_HALO = 32
_TAP_CHAINS = 4


def _nbytes(shape, dtype):
    n = 1
    for s in shape:
        n *= s
    return n * jnp.dtype(dtype).itemsize


def _params(semantics, windows, scratch=0):
    limit = 2 * sum(_nbytes(s, d) for s, d in windows) + scratch + _VMEM_SLACK_BYTES
    return pltpu.CompilerParams(dimension_semantics=semantics, vmem_limit_bytes=min(limit, _VMEM_CAP_BYTES))


def _rms_rows(x, g):
    return x * lax.rsqrt(jnp.mean(x * x, axis=-1, keepdims=True) + _EPS) * g


def _ln_rows(x, g, b):
    xc = x - jnp.mean(x, axis=-1, keepdims=True)
    return xc * lax.rsqrt(jnp.mean(xc * xc, axis=-1, keepdims=True) + _EPS) * g + b


def _for_row_blocks(n_rows, fn):
    def body(r, carry):
        fn(pl.ds(pl.multiple_of(r * _ROWS, _ROWS), _ROWS))
        return carry
    lax.fori_loop(0, n_rows // _ROWS, body, 0, unroll=4)


def _dot(a, b):
    return jnp.dot(a, b, preferred_element_type=_F32)


def _gelu_epilogue(accs, out_ref):
    (acc,) = accs
    out_ref[...] = (0.5 * acc * (1.0 + lax.erf(acc * 0.7071067811865476))).astype(out_ref.dtype)


def _plain_epilogue(accs, out_ref):
    (acc,) = accs
    out_ref[...] = acc.astype(out_ref.dtype)


def _product_epilogue(accs, out_ref):
    out_ref[...] = (accs[0] * accs[1]).astype(out_ref.dtype)


def _glu_epilogue(accs, out_ref):
    out_ref[...] = (accs[0] * jax.nn.sigmoid(accs[1])).astype(out_ref.dtype)


def _resident_weight_block(layer, first_block, nb):
    return lambda i, j: (layer, 0, first_block + jnp.where(i == 0, j, nb - 1))


def _resident_weights(w_refs, wbf_refs):
    j = pl.program_id(1)

    @pl.when(pl.program_id(0) == 0)
    def _():
        for w_ref, wbf_ref in zip(w_refs, wbf_refs):
            wbf_ref[j] = w_ref[...].astype(wbf_ref.dtype)
    return [wbf_ref[j] for wbf_ref in wbf_refs]


def _inproj_kernel(epilogue, parts, has_bias, x_ref, g_ref, *refs):
    w_refs, refs = refs[:parts], refs[parts:]
    b_refs, refs = (refs[:parts], refs[parts:]) if has_bias else ((), refs)
    out_ref, h_ref = refs[0], refs[1]
    wbf_refs = refs[2:]

    @pl.when(pl.program_id(1) == 0)
    def _():
        def norm(rows):
            h_ref[rows, :] = _rms_rows(x_ref[rows, :], g_ref[...]).astype(h_ref.dtype)
        _for_row_blocks(h_ref.shape[0], norm)

    h = h_ref[...]
    accs = [_dot(h, w) for w in _resident_weights(w_refs, wbf_refs)]
    if has_bias:
        accs = [acc + b_ref[...] for acc, b_ref in zip(accs, b_refs)]
    epilogue(accs, out_ref)


def _inproj(epilogue, x, g, g_layer, w, bias, layer, first_part, parts, n, tm, tn, out_dtype, name):
    m, k = x.shape
    nb = n // tn
    in_specs = [pl.BlockSpec((tm, k), lambda i, j: (i, 0)), pl.BlockSpec((None, 1, k), lambda i, j: (g_layer, 0, 0))]
    operands = [x, g.reshape(g.shape[0], 1, k)]
    for p in range(parts):
        in_specs.append(pl.BlockSpec((None, k, tn), _resident_weight_block(layer, (first_part + p) * nb, nb)))
        operands.append(w)
    if bias is not None:
        for p in range(parts):
            in_specs.append(pl.BlockSpec((None, 1, tn), functools.partial(
                lambda i, j, p: (layer, 0, (first_part + p) * nb + j), p=p)))
            operands.append(bias.reshape(bias.shape[0], 1, -1))
    windows = [((tm, k), _F32)] + [((k, tn), _F32)] * parts + [((tm, tn), out_dtype)]
    scratch = [((tm, k), _BF16)] + [((nb, k, tn), _BF16)] * parts
    return pl.pallas_call(
        functools.partial(_inproj_kernel, epilogue, parts, bias is not None),
        grid=(m // tm, nb),
        in_specs=in_specs,
        out_specs=pl.BlockSpec((tm, tn), lambda i, j: (i, j)),
        out_shape=jax.ShapeDtypeStruct((m, n), out_dtype),
        scratch_shapes=[pltpu.VMEM(s, dt) for s, dt in scratch],
        compiler_params=_params(("arbitrary", "arbitrary"), windows, sum(_nbytes(s, dt) for s, dt in scratch)),
        name=name,
    )(*operands)


def _mix_gmlp_kernel(groups, chunk, u_ref, v_ref, lng_ref, lnb_ref, ws_ref, bias_ref, y_ref, vn_ref, vnb_ref):
    tm, d = y_ref.shape
    gd = d // groups

    def normalise(rows):
        vn = _ln_rows(v_ref[rows, :].astype(_F32), lng_ref[...], lnb_ref[...])
        vn_ref[rows, :] = vn
        vnb_ref[rows, :] = vn.astype(vnb_ref.dtype)
    _for_row_blocks(tm, normalise)

    def mix_chunk(c, carry):
        rows = pl.ds(pl.multiple_of(c * chunk, chunk), chunk)
        for g in range(groups):
            cols = slice(g * gd, (g + 1) * gd)
            mixed = _dot(ws_ref[0, g], vnb_ref[rows, cols]) + bias_ref[0, :, cols]
            y_ref[rows, cols] = (u_ref[rows, cols].astype(_F32) * mixed).astype(y_ref.dtype)
        return carry
    lax.fori_loop(0, tm // chunk, mix_chunk, 0)


def _mix_gmlp(z, ln_g, ln_b, ws_all, bias_all, n_prompt, n_sample):
    m, d2 = z.shape
    d = d2 // 2
    tm = _TM_MIX
    n_pt = n_prompt // tm
    groups, chunk = ws_all.shape[1], ws_all.shape[2]
    s_tiles = n_sample // tm
    windows = [((tm, d), _BF16)] * 3 + [((tm, d), _F32), ((1, groups, chunk, chunk), _BF16), ((1, chunk, d), _F32)]
    y, vn = pl.pallas_call(
        functools.partial(_mix_gmlp_kernel, groups, chunk),
        grid=(m // tm,),
        in_specs=[
            pl.BlockSpec((tm, d), lambda i: (i, 0)),
            pl.BlockSpec((tm, d), lambda i: (i, 1)),
            pl.BlockSpec((1, d), lambda i: (0, 0)),
            pl.BlockSpec((1, d), lambda i: (0, 0)),
            pl.BlockSpec((1, groups, chunk, chunk), lambda i: (jnp.where(i < n_pt, 0, 1), 0, 0, 0)),
            pl.BlockSpec((1, chunk, d), lambda i: (jnp.where(i < n_pt, 0, 1), 0, 0)),
        ],
        out_specs=[
            pl.BlockSpec((tm, d), lambda i: (i, 0)),
            pl.BlockSpec((tm, d), lambda i: (jnp.maximum(i - n_pt, 0), 0)),
        ],
        out_shape=[jax.ShapeDtypeStruct((m, d), _BF16), jax.ShapeDtypeStruct((s_tiles * tm, d), _F32)],
        scratch_shapes=[pltpu.VMEM((tm, d), _BF16)],
        compiler_params=_params(("arbitrary",), windows, _nbytes((tm, d), _BF16)),
        name="mix_gmlp",
    )(z, z, ln_g.reshape(1, d), ln_b.reshape(1, d), ws_all, bias_all)
    return y, vn


def _mix_conv3_kernel(n_pt, seq, dseq, bg_ref, cx_ref, p1_ref, p2_ref, cw_ref, y_ref, ext_ref):
    i = pl.program_id(0)
    tm, d = y_ref.shape
    halo = _SUBLANES
    is_sample = i >= n_pt

    @pl.when(i == 0)
    def _():
        ext_ref[0:halo, :] = jnp.zeros((halo, d), _F32)

    @pl.when(i > 0)
    def _():
        ext_ref[0:halo, :] = ext_ref[tm:tm + halo, :]

    ext_ref[halo:halo + tm, :] = cx_ref[...]
    seg = jnp.where(is_sample, dseq, seq)

    def conv(rows):
        r0 = rows.start
        pos = (i * tm + r0 + lax.broadcasted_iota(jnp.int32, (_ROWS, 1), 0)) & (seg - 1)
        win = ext_ref[pl.ds(r0, halo + _ROWS), :]
        cur = win[halo:]
        prev1 = jnp.where(pos >= 1, win[halo - 1:halo - 1 + _ROWS], jnp.where(is_sample, p1_ref[rows, :], 0.0))
        prev2 = jnp.where(pos >= 2, win[halo - 2:halo - 2 + _ROWS], jnp.where(is_sample, p2_ref[rows, :], 0.0))
        acc = cw_ref[0:1, :] * prev2 + cw_ref[1:2, :] * prev1 + cw_ref[2:3, :] * cur
        y_ref[rows, :] = (bg_ref[rows, :].astype(_F32) * acc).astype(y_ref.dtype)
    _for_row_blocks(tm, conv)


def _mix_conv3(bg, cx, p1, p2, conv_w, n_prompt, seq, dseq):
    m, d = cx.shape
    tm = _TM_MIX
    n_pt = n_prompt // tm
    windows = [((tm, d), _BF16)] * 2 + [((tm, d), _F32)] * 3
    return pl.pallas_call(
        functools.partial(_mix_conv3_kernel, n_pt, seq, dseq),
        grid=(m // tm,),
        in_specs=[
            pl.BlockSpec((tm, d), lambda i: (i, 0)),
            pl.BlockSpec((tm, d), lambda i: (i, 0)),
            pl.BlockSpec((tm, d), lambda i: (jnp.maximum(i - n_pt, 0), 0)),
            pl.BlockSpec((tm, d), lambda i: (jnp.maximum(i - n_pt, 0), 0)),
            pl.BlockSpec(conv_w.shape, lambda i: (0, 0)),
        ],
        out_specs=pl.BlockSpec((tm, d), lambda i: (i, 0)),
        out_shape=jax.ShapeDtypeStruct((m, d), _BF16),
        scratch_shapes=[pltpu.VMEM((tm + _SUBLANES, d), _F32)],
        compiler_params=_params(("arbitrary",), windows, _nbytes((tm + _SUBLANES, d), _F32)),
        name="mix_conv3",
    )(bg, cx, p1, p2, conv_w)


def _depthwise_taps(width, fill, slab_of, ext_ref, cw_ref, c_ref, n_blocks, in_row, out_row):
    d = c_ref.shape[1]

    def lane_block(l, carry):
        cols = pl.ds(pl.multiple_of(l * _LANES, _LANES), _LANES)
        fill(l, cols)
        slab = slab_of(l)
        w = [jnp.broadcast_to(cw_ref[k:k + 1, cols], (_SUBLANES, _LANES)) for k in range(width)]

        def row_block(b, carry2):
            base = in_row(b)
            partial = [None] * _TAP_CHAINS
            for k in range(width):
                term = w[k] * ext_ref[slab, pl.ds(base + k, _SUBLANES), :]
                c = k % _TAP_CHAINS
                partial[c] = term if partial[c] is None else partial[c] + term
            while len(partial) > 1:
                partial = [a + b_ for a, b_ in zip(partial[0::2], partial[1::2])] + (
                    [partial[-1]] if len(partial) % 2 else [])
            c_ref[pl.ds(pl.multiple_of(out_row(b), _SUBLANES), _SUBLANES), cols] = partial[0]
            return carry2
        lax.fori_loop(0, n_blocks, row_block, 0, unroll=2)
        return carry
    lax.fori_loop(0, d // _LANES, lane_block, 0)


def _conformer_tail(c_ref, cb_ref, lng_ref, lnb_ref, y_ref):
    def tail(rows):
        c = _ln_rows(c_ref[rows, :] + cb_ref[...], lng_ref[...], lnb_ref[...])
        y_ref[rows, :] = (c * jax.nn.sigmoid(c)).astype(y_ref.dtype)
    _for_row_blocks(y_ref.shape[0], tail)


def _mix_conv31_kernel(width, n_pt, seq_tiles, dseq, glu_ref, past_ref, cw_ref, cb_ref, lng_ref, lnb_ref, y_ref,
                       ext_ref, sext_ref, c_ref):
    i = pl.program_id(0)
    tm, d = y_ref.shape
    halo = _CONV_HALO
    lead = halo - (width - 1)

    @pl.when(i < n_pt)
    def _():
        def fill(l, cols):
            @pl.when(lax.rem(i, seq_tiles) == 0)
            def _():
                ext_ref[l, 0:halo, :] = jnp.zeros((halo, _LANES), _F32)

            @pl.when(lax.rem(i, seq_tiles) != 0)
            def _():
                ext_ref[l, 0:halo, :] = ext_ref[l, tm:tm + halo, :]

            ext_ref[l, halo:halo + tm, :] = glu_ref[:, cols]

        _depthwise_taps(width, fill, lambda l: l, ext_ref, cw_ref, c_ref, tm // _SUBLANES,
                        lambda b: b * _SUBLANES + lead, lambda b: b * _SUBLANES)

    @pl.when(i >= n_pt)
    def _():
        slab = halo + dseq

        def fill(l, cols):
            def one_sequence(s, carry):
                sext_ref[0, pl.ds(pl.multiple_of(s * slab, _SUBLANES), halo), :] = (
                    past_ref[pl.ds(pl.multiple_of(s * halo, _SUBLANES), halo), cols])
                sext_ref[0, pl.ds(pl.multiple_of(s * slab + halo, _SUBLANES), dseq), :] = (
                    glu_ref[pl.ds(pl.multiple_of(s * dseq, _SUBLANES), dseq), cols])
                return carry
            lax.fori_loop(0, tm // dseq, one_sequence, 0)

        _depthwise_taps(width, fill, lambda l: 0, sext_ref, cw_ref, c_ref, tm // dseq,
                        lambda s: s * slab + lead, lambda s: s * dseq)

    _conformer_tail(c_ref, cb_ref, lng_ref, lnb_ref, y_ref)


def _mix_conv31(glu, past, conv_w, conv_b, ln_g, ln_b, n_prompt, seq, dseq):
    m, d = glu.shape
    width = conv_w.shape[0]
    assert width - 1 <= _CONV_HALO and dseq == _SUBLANES
    tm = _TM_CONV
    n_pt = n_prompt // tm
    n_seq = tm // dseq
    slab_rows = n_seq * (_CONV_HALO + dseq)
    past2d = jnp.pad(past, ((0, 0), (_CONV_HALO - (width - 1), 0), (0, 0))).reshape(-1, d)
    row = pl.BlockSpec((1, d), lambda i: (0, 0))
    scratch = [((d // _LANES, tm + _CONV_HALO, _LANES), _F32), ((1, slab_rows, _LANES), _F32), ((tm, d), _F32)]
    windows = [((tm, d), _F32), ((n_seq * _CONV_HALO, d), _F32), ((tm, d), _BF16)]
    return pl.pallas_call(
        functools.partial(_mix_conv31_kernel, width, n_pt, seq // tm, dseq),
        grid=(m // tm,),
        in_specs=[pl.BlockSpec((tm, d), lambda i: (i, 0)),
                  pl.BlockSpec((n_seq * _CONV_HALO, d), lambda i: (jnp.maximum(i - n_pt, 0), 0)),
                  pl.BlockSpec(conv_w.shape, lambda i: (0, 0)), row, row, row],
        out_specs=pl.BlockSpec((tm, d), lambda i: (i, 0)),
        out_shape=jax.ShapeDtypeStruct((m, d), _BF16),
        scratch_shapes=[pltpu.VMEM(s, dt) for s, dt in scratch],
        compiler_params=_params(("arbitrary",), windows, sum(_nbytes(s, dt) for s, dt in scratch)),
        name="mix_conv31",
    )(glu, past2d, conv_w, conv_b.reshape(1, d), ln_g.reshape(1, d), ln_b.reshape(1, d))


def _outproj_kernel(has_bias, y_ref, w_ref, *refs):
    if has_bias:
        b_ref, x_ref, xo_ref, wbf_ref = refs
    else:
        x_ref, xo_ref, wbf_ref = refs
    (w,) = _resident_weights([w_ref], [wbf_ref])
    acc = _dot(y_ref[...], w)
    if has_bias:
        acc = acc + b_ref[...]
    xo_ref[...] = x_ref[...] + acc


def _outproj(y, layer, w, bias, x):
    m, k = y.shape
    n = w.shape[2]
    tm, tn = _TM, _TN_OUT
    nb = n // tn
    in_specs = [pl.BlockSpec((tm, k), lambda i, j: (i, 0)),
                pl.BlockSpec((None, k, tn), _resident_weight_block(layer, 0, nb))]
    operands = [y, w]
    if bias is not None:
        in_specs.append(pl.BlockSpec((None, 1, tn), lambda i, j: (layer, 0, j)))
        operands.append(bias.reshape(bias.shape[0], 1, n))
    in_specs.append(pl.BlockSpec((tm, tn), lambda i, j: (i, j)))
    operands.append(x)
    windows = [((tm, k), _BF16), ((k, tn), _F32), ((tm, tn), _F32), ((tm, tn), _F32)]
    return pl.pallas_call(
        functools.partial(_outproj_kernel, bias is not None),
        grid=(m // tm, nb),
        in_specs=in_specs,
        out_specs=pl.BlockSpec((tm, tn), lambda i, j: (i, j)),
        out_shape=jax.ShapeDtypeStruct((m, n), _F32),
        scratch_shapes=[pltpu.VMEM((nb, k, tn), _BF16)],
        compiler_params=_params(("arbitrary", "arbitrary"), windows, _nbytes((nb, k, tn), _BF16)),
        name="outproj",
    )(*operands)


def _ffn_kernel(final, x_ref, g_ref, wu_ref, wd_ref, *refs):
    if final:
        gf_ref, o_ref, h_ref = refs
    else:
        o_ref, h_ref = refs
    f = pl.program_id(1)

    @pl.when(f == 0)
    def _():
        def start(rows):
            x = x_ref[rows, :]
            o_ref[rows, :] = x
            h_ref[rows, :] = _rms_rows(x, g_ref[...]).astype(h_ref.dtype)
        _for_row_blocks(o_ref.shape[0], start)

    a = jnp.maximum(_dot(h_ref[...], wu_ref[...].astype(_BF16)), 0.0)
    o_ref[...] += _dot((a * a).astype(_BF16), wd_ref[...].astype(_BF16))

    if final:
        @pl.when(f == pl.num_programs(1) - 1)
        def _():
            def finish(rows):
                o_ref[rows, :] = _rms_rows(o_ref[rows, :], gf_ref[...])
            _for_row_blocks(o_ref.shape[0], finish)


def _ffn(x, g, layer, w_up, w_down, row0, rows, g_final=None):
    d = x.shape[1]
    dff = w_up.shape[2]
    tm, tf = _TM, _TF
    t0 = row0 // tm
    in_specs = [pl.BlockSpec((tm, d), lambda i, f: (i + t0, 0)),
                pl.BlockSpec((None, 1, d), lambda i, f: (layer, 0, 0)),
                pl.BlockSpec((None, d, tf), lambda i, f: (layer, 0, f)),
                pl.BlockSpec((None, tf, d), lambda i, f: (layer, f, 0))]
    operands = [x, g.reshape(g.shape[0], 1, d), w_up, w_down]
    if g_final is not None:
        in_specs.append(pl.BlockSpec((1, d), lambda i, f: (0, 0)))
        operands.append(g_final.reshape(1, d))
    windows = [((tm, d), _F32), ((d, tf), _F32), ((tf, d), _F32), ((tm, d), _F32)]
    return pl.pallas_call(
        functools.partial(_ffn_kernel, g_final is not None),
        grid=(rows // tm, dff // tf),
        in_specs=in_specs,
        out_specs=pl.BlockSpec((tm, d), lambda i, f: (i, 0)),
        out_shape=jax.ShapeDtypeStruct((rows, d), _F32),
        scratch_shapes=[pltpu.VMEM((tm, d), _BF16)],
        compiler_params=_params(("arbitrary", "arbitrary"), windows, _nbytes((tm, d), _BF16)),
        name="ffn",
    )(*operands)


def _gating_operands(w_s, b_s, dseq, d):
    groups, chunk, _ = w_s.shape
    t = jnp.arange(chunk)
    causal = t[:, None] >= t[None, :]
    same_seq = (t[:, None] // dseq) == (t[None, :] // dseq)
    reps = chunk // dseq
    ws_prompt = jnp.where(causal, w_s, 0)
    ws_sample = jnp.where(causal & same_seq, jnp.tile(w_s[:, :dseq, :dseq], (1, reps, reps)), 0)
    gd = d // groups
    bias_prompt = jnp.repeat(b_s.T, gd, axis=1)
    bias_sample = jnp.repeat(jnp.tile(b_s[:, :dseq].T, (reps, 1)), gd, axis=1)
    return jnp.stack([ws_prompt, ws_sample]).astype(_BF16), jnp.stack([bias_prompt, bias_sample])


def _conv_states(past, t, bsz, seq, dbsz, dseq):
    keep, d = past.shape[1], t.shape[1]
    n_p = bsz * seq
    prompt = jnp.stack([lax.slice_in_dim(t, (b + 1) * seq - keep, (b + 1) * seq, axis=0) for b in range(bsz)])
    cur = lax.slice_in_dim(t, n_p, n_p + dbsz * dseq, axis=0).reshape(dbsz, dseq, d)
    sample = jnp.concatenate([past, cur], axis=1)[:, -keep:]
    return prompt, sample


def kernel(x_prompt, x_sample, state_b_conv, state_c_conv, norm_mix_g, norm_ffn_g, final_norm_g, a_w_in, a_ln_g, a_ln_b, a_w_s, a_b_s, a_w_out, b_w_in, b_conv_w, b_w_out, c_w_pw1, c_b_pw1, c_conv_w, c_conv_b, c_ln_g, c_ln_b, c_w_pw2, c_b_pw2, ffn_w_up, ffn_w_down):
    bsz, seq, d = x_prompt.shape
    dbsz, dseq, _ = x_sample.shape
    n_p, n_s = bsz * seq, dbsz * dseq
    depth = norm_mix_g.shape[0]
    chunk = a_w_s.shape[-1]
    assert seq % chunk == 0 and chunk % dseq == 0 and seq & (seq - 1) == 0 and dseq & (dseq - 1) == 0
    assert b_conv_w.shape[1] == 3 and seq % _TM_MIX == 0 and n_s % _TM_MIX == 0
    assert n_p % _TM == 0 and n_s % _TM == 0

    x = jnp.concatenate([x_prompt.reshape(n_p, d), x_sample.reshape(n_s, d)], axis=0)
    a_v, b_prompt, b_sample, c_prompt, c_sample = [], [], [], [], []
    for i in range(depth):
        j, kind = divmod(i, 3)
        if kind == 0:
            z = _inproj(_gelu_epilogue, x, norm_mix_g, i, a_w_in, None, j, 0, 1, 2 * d, 768, 1024, _BF16,
                        "inproj_gelu")
            ws_all, bias_all = _gating_operands(a_w_s[j], a_b_s[j], dseq, d)
            y, vn = _mix_gmlp(z, a_ln_g[j], a_ln_b[j], ws_all, bias_all, n_p, n_s)
            a_v.append(vn.reshape(dbsz, dseq, d))
            x = _outproj(y, j, a_w_out, None, x)
        elif kind == 1:
            bg = _inproj(_plain_epilogue, x, norm_mix_g, i, b_w_in, None, j, 0, 1, d, 1024, 1024, _BF16,
                         "inproj_gate")
            cx = _inproj(_product_epilogue, x, norm_mix_g, i, b_w_in, None, j, 1, 2, d, 768, 512, _F32,
                         "inproj_product")
            past = state_b_conv[j]
            p2 = jnp.pad(past, ((0, 0), (0, dseq - 2), (0, 0))).reshape(n_s, d)
            p1 = jnp.pad(past[:, 1:], ((0, 0), (0, dseq - 1), (0, 0))).reshape(n_s, d)
            y = _mix_conv3(bg, cx, p1, p2, b_conv_w[j], n_p, seq, dseq)
            prompt_state, sample_state = _conv_states(past, cx, bsz, seq, dbsz, dseq)
            b_prompt.append(prompt_state)
            b_sample.append(sample_state)
            x = _outproj(y, j, b_w_out, None, x)
        else:
            glu = _inproj(_glu_epilogue, x, norm_mix_g, i, c_w_pw1, c_b_pw1, j, 0, 2, d, 768, 512, _F32,
                          "inproj_glu")
            past = state_c_conv[j]
            y = _mix_conv31(glu, past, c_conv_w[j], c_conv_b[j], c_ln_g[j], c_ln_b[j], n_p, seq, dseq)
            prompt_state, sample_state = _conv_states(past, glu, bsz, seq, dbsz, dseq)
            c_prompt.append(prompt_state)
            c_sample.append(sample_state)
            x = _outproj(y, j, c_w_pw2, c_b_pw2, x)
        if i + 1 < depth:
            x = _ffn(x, norm_ffn_g, i, ffn_w_up, ffn_w_down, 0, n_p + n_s)
        else:
            y_prompt = _ffn(x, norm_ffn_g, i, ffn_w_up, ffn_w_down, 0, n_p, final_norm_g)
            y_sample = _ffn(x, norm_ffn_g, i, ffn_w_up, ffn_w_down, n_p, n_s, final_norm_g)
    return (y_prompt.reshape(bsz, seq, d), y_sample.reshape(dbsz, dseq, d), jnp.stack(a_v),
            jnp.stack(b_prompt), jnp.stack(b_sample), jnp.stack(c_prompt), jnp.stack(c_sample))
```

```python
import functools

import jax
import jax.numpy as jnp
from jax import lax
from jax.experimental import pallas as pl
from jax.experimental.pallas import tpu as pltpu

_F32 = jnp.float32
_BF16 = jnp.bfloat16
_EPS = 1e-6

_LANES = 128
_SUBLANES = 8
_VMEM_CAP_BYTES = 60000 * 1024
_VMEM_SLACK_BYTES = 12 * 1024 * 1024

_TM = 1024
_TM_MIX = 512
_TN_OUT = 1024
_TF = 512
_TM_CONV = 256
_ROWS = 16
_CONV_HALO = 32
_TAP_CHAINS = 2


def _nbytes(shape, dtype):
    n = 1
    for s in shape:
        n *= s
    return n * jnp.dtype(dtype).itemsize


def _params(semantics, windows, scratch=0):
    limit = 2 * sum(_nbytes(s, d) for s, d in windows) + scratch + _VMEM_SLACK_BYTES
    return pltpu.CompilerParams(dimension_semantics=semantics, vmem_limit_bytes=min(limit, _VMEM_CAP_BYTES))


def _rms_rows(x, g):
    return x * lax.rsqrt(jnp.mean(x * x, axis=-1, keepdims=True) + _EPS) * g


def _ln_rows(x, g, b):
    xc = x - jnp.mean(x, axis=-1, keepdims=True)
    return xc * lax.rsqrt(jnp.mean(xc * xc, axis=-1, keepdims=True) + _EPS) * g + b


def _for_row_blocks(n_rows, fn):
    def body(r, carry):
        fn(pl.ds(pl.multiple_of(r * _ROWS, _ROWS), _ROWS))
        return carry
    lax.fori_loop(0, n_rows // _ROWS, body, 0, unroll=4)


def _dot(a, b):
    return jnp.dot(a, b, preferred_element_type=_F32)


def _gelu_epilogue(accs, out_ref):
    (acc,) = accs
    out_ref[...] = (0.5 * acc * (1.0 + lax.erf(acc * 0.7071067811865476))).astype(out_ref.dtype)


def _plain_epilogue(accs, out_ref):
    (acc,) = accs
    out_ref[...] = acc.astype(out_ref.dtype)


def _product_epilogue(accs, out_ref):
    out_ref[...] = (accs[0] * accs[1]).astype(out_ref.dtype)


def _glu_epilogue(accs, out_ref):
    out_ref[...] = (accs[0] * jax.nn.sigmoid(accs[1])).astype(out_ref.dtype)


def _resident_weight_block(layer, first_block, nb):
    return lambda i, j: (layer, 0, first_block + jnp.where(i == 0, j, nb - 1))


def _resident_weights(w_refs, wbf_refs):
    j = pl.program_id(1)

    @pl.when(pl.program_id(0) == 0)
    def _():
        for w_ref, wbf_ref in zip(w_refs, wbf_refs):
            wbf_ref[j] = w_ref[...].astype(wbf_ref.dtype)
    return [wbf_ref[j] for wbf_ref in wbf_refs]


def _inproj_kernel(epilogue, parts, has_bias, x_ref, g_ref, *refs):
    w_refs, refs = refs[:parts], refs[parts:]
    b_refs, refs = (refs[:parts], refs[parts:]) if has_bias else ((), refs)
    out_ref, h_ref = refs[0], refs[1]
    wbf_refs = refs[2:]

    @pl.when(pl.program_id(1) == 0)
    def _():
        def norm(rows):
            h_ref[rows, :] = _rms_rows(x_ref[rows, :], g_ref[...]).astype(h_ref.dtype)
        _for_row_blocks(h_ref.shape[0], norm)

    ws = _resident_weights(w_refs, wbf_refs)
    h = h_ref[...]
    accs = [_dot(h, w) for w in ws]
    if has_bias:
        accs = [acc + b_ref[...] for acc, b_ref in zip(accs, b_refs)]
    epilogue(accs, out_ref)


def _inproj(epilogue, x, g, g_layer, w, bias, layer, first_part, parts, n, tm, tn, out_dtype, name):
    m, k = x.shape
    nb = n // tn
    in_specs = [pl.BlockSpec((tm, k), lambda i, j: (i, 0)), pl.BlockSpec((None, 1, k), lambda i, j: (g_layer, 0, 0))]
    operands = [x, g.reshape(g.shape[0], 1, k)]
    for p in range(parts):
        in_specs.append(pl.BlockSpec((None, k, tn), _resident_weight_block(layer, (first_part + p) * nb, nb)))
        operands.append(w)
    if bias is not None:
        for p in range(parts):
            in_specs.append(pl.BlockSpec((None, 1, tn), functools.partial(
                lambda i, j, p: (layer, 0, (first_part + p) * nb + j), p=p)))
            operands.append(bias.reshape(bias.shape[0], 1, -1))
    windows = [((tm, k), _F32)] + [((k, tn), _F32)] * parts + [((tm, tn), out_dtype)]
    scratch = [((tm, k), _BF16)] + [((nb, k, tn), _BF16)] * parts
    return pl.pallas_call(
        functools.partial(_inproj_kernel, epilogue, parts, bias is not None),
        grid=(m // tm, nb),
        in_specs=in_specs,
        out_specs=pl.BlockSpec((tm, tn), lambda i, j: (i, j)),
        out_shape=jax.ShapeDtypeStruct((m, n), out_dtype),
        scratch_shapes=[pltpu.VMEM(s, dt) for s, dt in scratch],
        compiler_params=_params(("arbitrary", "arbitrary"), windows, sum(_nbytes(s, dt) for s, dt in scratch)),
        name=name,
    )(*operands)


def _mix_gmlp_kernel(groups, chunk, u_ref, v_ref, lng_ref, lnb_ref, ws_ref, bias_ref, y_ref, vn_ref, vnb_ref):
    tm, d = y_ref.shape
    gd = d // groups

    def normalise(rows):
        vn = _ln_rows(v_ref[rows, :].astype(_F32), lng_ref[...], lnb_ref[...])
        vn_ref[rows, :] = vn
        vnb_ref[rows, :] = vn.astype(vnb_ref.dtype)
    _for_row_blocks(tm, normalise)

    def mix_chunk(c, carry):
        rows = pl.ds(pl.multiple_of(c * chunk, chunk), chunk)
        for g in range(groups):
            cols = slice(g * gd, (g + 1) * gd)
            mixed = _dot(ws_ref[0, g], vnb_ref[rows, cols]) + bias_ref[0, :, cols]
            y_ref[rows, cols] = (u_ref[rows, cols].astype(_F32) * mixed).astype(y_ref.dtype)
        return carry
    lax.fori_loop(0, tm // chunk, mix_chunk, 0)


def _mix_gmlp(z, ln_g, ln_b, ws_all, bias_all, n_prompt, n_sample):
    m, d2 = z.shape
    d = d2 // 2
    tm = _TM_MIX
    n_pt = n_prompt // tm
    groups, chunk = ws_all.shape[1], ws_all.shape[2]
    s_tiles = n_sample // tm
    windows = [((tm, d), _BF16)] * 3 + [((tm, d), _F32), ((1, groups, chunk, chunk), _BF16), ((1, chunk, d), _F32)]
    y, vn = pl.pallas_call(
        functools.partial(_mix_gmlp_kernel, groups, chunk),
        grid=(m // tm,),
        in_specs=[
            pl.BlockSpec((tm, d), lambda i: (i, 0)),
            pl.BlockSpec((tm, d), lambda i: (i, 1)),
            pl.BlockSpec((1, d), lambda i: (0, 0)),
            pl.BlockSpec((1, d), lambda i: (0, 0)),
            pl.BlockSpec((1, groups, chunk, chunk), lambda i: (jnp.where(i < n_pt, 0, 1), 0, 0, 0)),
            pl.BlockSpec((1, chunk, d), lambda i: (jnp.where(i < n_pt, 0, 1), 0, 0)),
        ],
        out_specs=[
            pl.BlockSpec((tm, d), lambda i: (i, 0)),
            pl.BlockSpec((tm, d), lambda i: (jnp.maximum(i - n_pt, 0), 0)),
        ],
        out_shape=[jax.ShapeDtypeStruct((m, d), _BF16), jax.ShapeDtypeStruct((s_tiles * tm, d), _F32)],
        scratch_shapes=[pltpu.VMEM((tm, d), _BF16)],
        compiler_params=_params(("arbitrary",), windows, _nbytes((tm, d), _BF16)),
        name="mix_gmlp",
    )(z, z, ln_g.reshape(1, d), ln_b.reshape(1, d), ws_all, bias_all)
    return y, vn


def _mix_conv3_kernel(n_pt, seq, dseq, bg_ref, cx_ref, p1_ref, p2_ref, cw_ref, y_ref, ext_ref):
    i = pl.program_id(0)
    tm, d = y_ref.shape
    halo = _SUBLANES
    is_sample = i >= n_pt

    @pl.when(i == 0)
    def _():
        ext_ref[0:halo, :] = jnp.zeros((halo, d), _F32)

    @pl.when(i > 0)
    def _():
        ext_ref[0:halo, :] = ext_ref[tm:tm + halo, :]

    ext_ref[halo:halo + tm, :] = cx_ref[...]
    seg = jnp.where(is_sample, dseq, seq)

    def conv(rows):
        r0 = rows.start
        pos = (i * tm + r0 + lax.broadcasted_iota(jnp.int32, (_ROWS, 1), 0)) & (seg - 1)
        win = ext_ref[pl.ds(r0, halo + _ROWS), :]
        cur = win[halo:]
        prev1 = jnp.where(pos >= 1, win[halo - 1:halo - 1 + _ROWS], jnp.where(is_sample, p1_ref[rows, :], 0.0))
        prev2 = jnp.where(pos >= 2, win[halo - 2:halo - 2 + _ROWS], jnp.where(is_sample, p2_ref[rows, :], 0.0))
        acc = cw_ref[0:1, :] * prev2 + cw_ref[1:2, :] * prev1 + cw_ref[2:3, :] * cur
        y_ref[rows, :] = (bg_ref[rows, :].astype(_F32) * acc).astype(y_ref.dtype)
    _for_row_blocks(tm, conv)


def _mix_conv3(bg, cx, p1, p2, conv_w, n_prompt, seq, dseq):
    m, d = cx.shape
    tm = _TM_MIX
    n_pt = n_prompt // tm
    windows = [((tm, d), _BF16)] * 2 + [((tm, d), _F32)] * 3
    return pl.pallas_call(
        functools.partial(_mix_conv3_kernel, n_pt, seq, dseq),
        grid=(m // tm,),
        in_specs=[
            pl.BlockSpec((tm, d), lambda i: (i, 0)),
            pl.BlockSpec((tm, d), lambda i: (i, 0)),
            pl.BlockSpec((tm, d), lambda i: (jnp.maximum(i - n_pt, 0), 0)),
            pl.BlockSpec((tm, d), lambda i: (jnp.maximum(i - n_pt, 0), 0)),
            pl.BlockSpec(conv_w.shape, lambda i: (0, 0)),
        ],
        out_specs=pl.BlockSpec((tm, d), lambda i: (i, 0)),
        out_shape=jax.ShapeDtypeStruct((m, d), _BF16),
        scratch_shapes=[pltpu.VMEM((tm + _SUBLANES, d), _F32)],
        compiler_params=_params(("arbitrary",), windows, _nbytes((tm + _SUBLANES, d), _F32)),
        name="mix_conv3",
    )(bg, cx, p1, p2, conv_w)


def _depthwise_taps(width, fill, slab_of, ext_ref, cw_ref, c_ref, n_blocks, in_row, out_row):
    d = c_ref.shape[1]

    def lane_block(l, carry):
        cols = pl.ds(pl.multiple_of(l * _LANES, _LANES), _LANES)
        fill(l, cols)
        slab = slab_of(l)
        w = [jnp.broadcast_to(cw_ref[k:k + 1, cols], (_SUBLANES, _LANES)) for k in range(width)]

        def row_block(b, carry2):
            base = in_row(b)
            partial = [None] * _TAP_CHAINS
            for k in range(width):
                term = w[k] * ext_ref[slab, pl.ds(base + k, _SUBLANES), :]
                c = k % _TAP_CHAINS
                partial[c] = term if partial[c] is None else partial[c] + term
            while len(partial) > 1:
                partial = [a + b_ for a, b_ in zip(partial[0::2], partial[1::2])] + (
                    [partial[-1]] if len(partial) % 2 else [])
            c_ref[pl.ds(pl.multiple_of(out_row(b), _SUBLANES), _SUBLANES), cols] = partial[0]
            return carry2
        lax.fori_loop(0, n_blocks, row_block, 0, unroll=8)
        return carry
    lax.fori_loop(0, d // _LANES, lane_block, 0)


def _conformer_tail(c_ref, cb_ref, lng_ref, lnb_ref, y_ref):
    def tail(rows):
        c = _ln_rows(c_ref[rows, :] + cb_ref[...], lng_ref[...], lnb_ref[...])
        y_ref[rows, :] = (c * jax.nn.sigmoid(c)).astype(y_ref.dtype)
    _for_row_blocks(y_ref.shape[0], tail)


def _mix_conv31_kernel(width, n_pt, seq_tiles, dseq, glu_ref, past_ref, cw_ref, cb_ref, lng_ref, lnb_ref, y_ref,
                       ext_ref, sext_ref, c_ref):
    i = pl.program_id(0)
    tm, d = y_ref.shape
    halo = _CONV_HALO
    lead = halo - (width - 1)

    @pl.when(i < n_pt)
    def _():
        def fill(l, cols):
            @pl.when(lax.rem(i, seq_tiles) == 0)
            def _():
                ext_ref[l, 0:halo, :] = jnp.zeros((halo, _LANES), _F32)

            @pl.when(lax.rem(i, seq_tiles) != 0)
            def _():
                ext_ref[l, 0:halo, :] = ext_ref[l, tm:tm + halo, :]

            ext_ref[l, halo:halo + tm, :] = glu_ref[:, cols]

        _depthwise_taps(width, fill, lambda l: l, ext_ref, cw_ref, c_ref, tm // _SUBLANES,
                        lambda b: b * _SUBLANES + lead, lambda b: b * _SUBLANES)

    @pl.when(i >= n_pt)
    def _():
        slab = halo + dseq

        def fill(l, cols):
            def one_sequence(s, carry):
                sext_ref[0, pl.ds(s * slab + lead, width - 1), :] = past_ref[s, :, cols]
                sext_ref[0, pl.ds(pl.multiple_of(s * slab + halo, _SUBLANES), dseq), :] = (
                    glu_ref[pl.ds(pl.multiple_of(s * dseq, _SUBLANES), dseq), cols])
                return carry
            lax.fori_loop(0, tm // dseq, one_sequence, 0)

        _depthwise_taps(width, fill, lambda l: 0, sext_ref, cw_ref, c_ref, tm // dseq,
                        lambda s: s * slab + lead, lambda s: s * dseq)

    _conformer_tail(c_ref, cb_ref, lng_ref, lnb_ref, y_ref)


def _mix_conv31(glu, past, conv_w, conv_b, ln_g, ln_b, n_prompt, seq, dseq):
    m, d = glu.shape
    width = conv_w.shape[0]
    assert width - 1 <= _CONV_HALO and dseq == _SUBLANES
    tm = _TM_CONV
    n_pt = n_prompt // tm
    n_seq = tm // dseq
    slab_rows = n_seq * (_CONV_HALO + dseq)
    row = pl.BlockSpec((1, d), lambda i: (0, 0))
    scratch = [((d // _LANES, tm + _CONV_HALO, _LANES), _F32), ((1, slab_rows, _LANES), _F32), ((tm, d), _F32)]
    windows = [((tm, d), _F32), ((n_seq, _CONV_HALO, d), _F32), ((tm, d), _BF16)]
    return pl.pallas_call(
        functools.partial(_mix_conv31_kernel, width, n_pt, seq // tm, dseq),
        grid=(m // tm,),
        in_specs=[pl.BlockSpec((tm, d), lambda i: (i, 0)),
                  pl.BlockSpec((n_seq, width - 1, d), lambda i: (jnp.maximum(i - n_pt, 0), 0, 0)),
                  pl.BlockSpec(conv_w.shape, lambda i: (0, 0)), row, row, row],
        out_specs=pl.BlockSpec((tm, d), lambda i: (i, 0)),
        out_shape=jax.ShapeDtypeStruct((m, d), _BF16),
        scratch_shapes=[pltpu.VMEM(s, dt) for s, dt in scratch],
        compiler_params=_params(("arbitrary",), windows, sum(_nbytes(s, dt) for s, dt in scratch)),
        name="mix_conv31",
    )(glu, past, conv_w, conv_b.reshape(1, d), ln_g.reshape(1, d), ln_b.reshape(1, d))


def _outproj_kernel(has_bias, y_ref, w_ref, *refs):
    if has_bias:
        b_ref, x_ref, xo_ref, wbf_ref = refs
    else:
        x_ref, xo_ref, wbf_ref = refs
    (w,) = _resident_weights([w_ref], [wbf_ref])
    acc = _dot(y_ref[...], w)
    if has_bias:
        acc = acc + b_ref[...]
    xo_ref[...] = x_ref[...] + acc


def _outproj(y, layer, w, bias, x):
    m, k = y.shape
    n = w.shape[2]
    tm, tn = _TM, _TN_OUT
    nb = n // tn
    in_specs = [pl.BlockSpec((tm, k), lambda i, j: (i, 0)),
                pl.BlockSpec((None, k, tn), _resident_weight_block(layer, 0, nb))]
    operands = [y, w]
    if bias is not None:
        in_specs.append(pl.BlockSpec((None, 1, tn), lambda i, j: (layer, 0, j)))
        operands.append(bias.reshape(bias.shape[0], 1, n))
    in_specs.append(pl.BlockSpec((tm, tn), lambda i, j: (i, j)))
    operands.append(x)
    windows = [((tm, k), _BF16), ((k, tn), _F32), ((tm, tn), _F32), ((tm, tn), _F32)]
    return pl.pallas_call(
        functools.partial(_outproj_kernel, bias is not None),
        grid=(m // tm, nb),
        in_specs=in_specs,
        out_specs=pl.BlockSpec((tm, tn), lambda i, j: (i, j)),
        out_shape=jax.ShapeDtypeStruct((m, n), _F32),
        scratch_shapes=[pltpu.VMEM((nb, k, tn), _BF16)],
        compiler_params=_params(("arbitrary", "arbitrary"), windows, _nbytes((nb, k, tn), _BF16)),
        name="outproj",
    )(*operands)


def _ffn_kernel(final, x_ref, g_ref, wu_ref, wd_ref, *refs):
    if final:
        gf_ref, o_ref, h_ref = refs
    else:
        o_ref, h_ref = refs
    f = pl.program_id(1)

    @pl.when(f == 0)
    def _():
        def start(rows):
            x = x_ref[rows, :]
            o_ref[rows, :] = x
            h_ref[rows, :] = _rms_rows(x, g_ref[...]).astype(h_ref.dtype)
        _for_row_blocks(o_ref.shape[0], start)

    a = jnp.maximum(_dot(h_ref[...], wu_ref[...].astype(_BF16)), 0.0)
    o_ref[...] += _dot((a * a).astype(_BF16), wd_ref[...].astype(_BF16))

    if final:
        @pl.when(f == pl.num_programs(1) - 1)
        def _():
            def finish(rows):
                o_ref[rows, :] = _rms_rows(o_ref[rows, :], gf_ref[...])
            _for_row_blocks(o_ref.shape[0], finish)


def _ffn(x, g, layer, w_up, w_down, row0, rows, g_final=None):
    d = x.shape[1]
    dff = w_up.shape[2]
    tm, tf = _TM, _TF
    t0 = row0 // tm
    in_specs = [pl.BlockSpec((tm, d), lambda i, f: (i + t0, 0)),
                pl.BlockSpec((None, 1, d), lambda i, f: (layer, 0, 0)),
                pl.BlockSpec((None, d, tf), lambda i, f: (layer, 0, f)),
                pl.BlockSpec((None, tf, d), lambda i, f: (layer, f, 0))]
    operands = [x, g.reshape(g.shape[0], 1, d), w_up, w_down]
    if g_final is not None:
        in_specs.append(pl.BlockSpec((1, d), lambda i, f: (0, 0)))
        operands.append(g_final.reshape(1, d))
    windows = [((tm, d), _F32), ((d, tf), _F32), ((tf, d), _F32), ((tm, d), _F32)]
    return pl.pallas_call(
        functools.partial(_ffn_kernel, g_final is not None),
        grid=(rows // tm, dff // tf),
        in_specs=in_specs,
        out_specs=pl.BlockSpec((tm, d), lambda i, f: (i, 0)),
        out_shape=jax.ShapeDtypeStruct((rows, d), _F32),
        scratch_shapes=[pltpu.VMEM((tm, d), _BF16)],
        compiler_params=_params(("arbitrary", "arbitrary"), windows, _nbytes((tm, d), _BF16)),
        name="ffn",
    )(*operands)


def _gating_operands(w_s, b_s, dseq, d):
    groups, chunk, _ = w_s.shape
    t = jnp.arange(chunk)
    causal = t[:, None] >= t[None, :]
    same_seq = (t[:, None] // dseq) == (t[None, :] // dseq)
    reps = chunk // dseq
    ws_prompt = jnp.where(causal, w_s, 0)
    ws_sample = jnp.where(causal & same_seq, jnp.tile(w_s[:, :dseq, :dseq], (1, reps, reps)), 0)
    gd = d // groups
    bias_prompt = jnp.repeat(b_s.T, gd, axis=1)
    bias_sample = jnp.repeat(jnp.tile(b_s[:, :dseq].T, (reps, 1)), gd, axis=1)
    return jnp.stack([ws_prompt, ws_sample]).astype(_BF16), jnp.stack([bias_prompt, bias_sample])


def _conv_states(past, t, bsz, seq, dbsz, dseq):
    keep, d = past.shape[1], t.shape[1]
    n_p = bsz * seq
    prompt = jnp.stack([lax.slice_in_dim(t, (b + 1) * seq - keep, (b + 1) * seq, axis=0) for b in range(bsz)])
    cur = lax.slice_in_dim(t, n_p, n_p + dbsz * dseq, axis=0).reshape(dbsz, dseq, d)
    sample = jnp.concatenate([past, cur], axis=1)[:, -keep:]
    return prompt, sample


def kernel(x_prompt, x_sample, state_b_conv, state_c_conv, norm_mix_g, norm_ffn_g, final_norm_g, a_w_in, a_ln_g, a_ln_b, a_w_s, a_b_s, a_w_out, b_w_in, b_conv_w, b_w_out, c_w_pw1, c_b_pw1, c_conv_w, c_conv_b, c_ln_g, c_ln_b, c_w_pw2, c_b_pw2, ffn_w_up, ffn_w_down):
    bsz, seq, d = x_prompt.shape
    dbsz, dseq, _ = x_sample.shape
    n_p, n_s = bsz * seq, dbsz * dseq
    depth = norm_mix_g.shape[0]
    chunk = a_w_s.shape[-1]
    assert seq % chunk == 0 and chunk % dseq == 0 and seq & (seq - 1) == 0 and dseq & (dseq - 1) == 0
    assert b_conv_w.shape[1] == 3 and seq % _TM_MIX == 0 and n_s % _TM_MIX == 0
    assert n_p % _TM == 0 and n_s % _TM == 0

    x = jnp.concatenate([x_prompt.reshape(n_p, d), x_sample.reshape(n_s, d)], axis=0)
    a_v, b_prompt, b_sample, c_prompt, c_sample = [], [], [], [], []
    for i in range(depth):
        j, kind = divmod(i, 3)
        if kind == 0:
            z = _inproj(_gelu_epilogue, x, norm_mix_g, i, a_w_in, None, j, 0, 1, 2 * d, 768, 1024, _BF16,
                        "inproj_gelu")
            ws_all, bias_all = _gating_operands(a_w_s[j], a_b_s[j], dseq, d)
            y, vn = _mix_gmlp(z, a_ln_g[j], a_ln_b[j], ws_all, bias_all, n_p, n_s)
            a_v.append(vn.reshape(dbsz, dseq, d))
            x = _outproj(y, j, a_w_out, None, x)
        elif kind == 1:
            bg = _inproj(_plain_epilogue, x, norm_mix_g, i, b_w_in, None, j, 0, 1, d, 1024, 1024, _BF16,
                         "inproj_gate")
            cx = _inproj(_product_epilogue, x, norm_mix_g, i, b_w_in, None, j, 1, 2, d, 768, 512, _F32,
                         "inproj_product")
            past = state_b_conv[j]
            p2 = jnp.pad(past, ((0, 0), (0, dseq - 2), (0, 0))).reshape(n_s, d)
            p1 = jnp.pad(past[:, 1:], ((0, 0), (0, dseq - 1), (0, 0))).reshape(n_s, d)
            y = _mix_conv3(bg, cx, p1, p2, b_conv_w[j], n_p, seq, dseq)
            prompt_state, sample_state = _conv_states(past, cx, bsz, seq, dbsz, dseq)
            b_prompt.append(prompt_state)
            b_sample.append(sample_state)
            x = _outproj(y, j, b_w_out, None, x)
        else:
            glu = _inproj(_glu_epilogue, x, norm_mix_g, i, c_w_pw1, c_b_pw1, j, 0, 2, d, 768, 512, _F32,
                          "inproj_glu")
            past = state_c_conv[j]
            y = _mix_conv31(glu, past, c_conv_w[j], c_conv_b[j], c_ln_g[j], c_ln_b[j], n_p, seq, dseq)
            prompt_state, sample_state = _conv_states(past, glu, bsz, seq, dbsz, dseq)
            c_prompt.append(prompt_state)
            c_sample.append(sample_state)
            x = _outproj(y, j, c_w_pw2, c_b_pw2, x)
        if i + 1 < depth:
            x = _ffn(x, norm_ffn_g, i, ffn_w_up, ffn_w_down, 0, n_p + n_s)
        else:
            y_prompt = _ffn(x, norm_ffn_g, i, ffn_w_up, ffn_w_down, 0, n_p, final_norm_g)
            y_sample = _ffn(x, norm_ffn_g, i, ffn_w_up, ffn_w_down, n_p, n_s, final_norm_g)
    return (y_prompt.reshape(bsz, seq, d), y_sample.reshape(dbsz, dseq, d), jnp.stack(a_v),
            jnp.stack(b_prompt), jnp.stack(b_sample), jnp.stack(c_prompt), jnp.stack(c_sample))
```

```python
import functools

import jax
import jax.numpy as jnp
from jax import lax
from jax.experimental import pallas as pl
from jax.experimental.pallas import tpu as pltpu

_F32 = jnp.float32
_BF16 = jnp.bfloat16
_EPS = 1e-6

_LANES = 128
_SUBLANES = 8
_VMEM_CAP_BYTES = 60000 * 1024
_VMEM_SLACK_BYTES = 12 * 1024 * 1024

_TM = 1024
_TM_MIX = 512
_TN_OUT = 1024
_TF = 512
_TM_CONV = 256
_ROWS = 16
_CONV_HALO = 32
_TAP_CHAINS = 2


def _nbytes(shape, dtype):
    n = 1
    for s in shape:
        n *= s
    return n * jnp.dtype(dtype).itemsize


def _params(semantics, windows, scratch=0):
    limit = 2 * sum(_nbytes(s, d) for s, d in windows) + scratch + _VMEM_SLACK_BYTES
    return pltpu.CompilerParams(dimension_semantics=semantics, vmem_limit_bytes=min(limit, _VMEM_CAP_BYTES))


def _rms_rows(x, g):
    return x * lax.rsqrt(jnp.mean(x * x, axis=-1, keepdims=True) + _EPS) * g


def _ln_rows(x, g, b):
    xc = x - jnp.mean(x, axis=-1, keepdims=True)
    return xc * lax.rsqrt(jnp.mean(xc * xc, axis=-1, keepdims=True) + _EPS) * g + b


def _for_row_blocks(n_rows, fn):
    def body(r, carry):
        fn(pl.ds(pl.multiple_of(r * _ROWS, _ROWS), _ROWS))
        return carry
    lax.fori_loop(0, n_rows // _ROWS, body, 0, unroll=4)


def _dot(a, b):
    return jnp.dot(a, b, preferred_element_type=_F32)


def _gelu_epilogue(accs, out_ref):
    (acc,) = accs
    out_ref[...] = (0.5 * acc * (1.0 + lax.erf(acc * 0.7071067811865476))).astype(out_ref.dtype)


def _plain_epilogue(accs, out_ref):
    (acc,) = accs
    out_ref[...] = acc.astype(out_ref.dtype)


def _product_epilogue(accs, out_ref):
    out_ref[...] = (accs[0] * accs[1]).astype(out_ref.dtype)


def _glu_epilogue(accs, out_ref):
    out_ref[...] = (accs[0] * jax.nn.sigmoid(accs[1])).astype(out_ref.dtype)


def _resident_weight_block(layer, first_block, nb):
    return lambda i, j: (layer, 0, first_block + jnp.where(i == 0, j, nb - 1))


def _resident_weights(w_refs, wbf_refs):
    j = pl.program_id(1)

    @pl.when(pl.program_id(0) == 0)
    def _():
        for w_ref, wbf_ref in zip(w_refs, wbf_refs):
            wbf_ref[j] = w_ref[...].astype(wbf_ref.dtype)
    return [wbf_ref[j] for wbf_ref in wbf_refs]


def _inproj_kernel(epilogue, parts, has_bias, normed, x_ref, *refs):
    if not normed:
        g_ref, refs = refs[0], refs[1:]
    w_refs, refs = refs[:parts], refs[parts:]
    b_refs, refs = (refs[:parts], refs[parts:]) if has_bias else ((), refs)
    out_ref, refs = refs[0], refs[1:]
    if normed:
        h_ref = x_ref
    else:
        h_ref, refs = refs[0], refs[1:]

        @pl.when(pl.program_id(1) == 0)
        def _():
            def norm(rows):
                h_ref[rows, :] = _rms_rows(x_ref[rows, :], g_ref[...]).astype(h_ref.dtype)
            _for_row_blocks(h_ref.shape[0], norm)

    ws = _resident_weights(w_refs, refs)
    h = h_ref[...]
    accs = [_dot(h, w) for w in ws]
    if has_bias:
        accs = [acc + b_ref[...] for acc, b_ref in zip(accs, b_refs)]
    epilogue(accs, out_ref)


def _inproj(epilogue, x, g, g_layer, w, bias, layer, first_part, parts, n, tm, tn, out_dtype, name, emit_h=False):
    m, k = x.shape
    nb = n // tn
    normed = g is None
    in_specs = [pl.BlockSpec((tm, k), lambda i, j: (i, 0))]
    operands = [x]
    if not normed:
        in_specs.append(pl.BlockSpec((None, 1, k), lambda i, j: (g_layer, 0, 0)))
        operands.append(g.reshape(g.shape[0], 1, k))
    for p in range(parts):
        in_specs.append(pl.BlockSpec((None, k, tn), _resident_weight_block(layer, (first_part + p) * nb, nb)))
        operands.append(w)
    if bias is not None:
        for p in range(parts):
            in_specs.append(pl.BlockSpec((None, 1, tn), functools.partial(
                lambda i, j, p: (layer, 0, (first_part + p) * nb + j), p=p)))
            operands.append(bias.reshape(bias.shape[0], 1, -1))
    windows = [((tm, k), x.dtype)] + [((k, tn), _F32)] * parts + [((tm, tn), out_dtype)]
    out_specs = [pl.BlockSpec((tm, tn), lambda i, j: (i, j))]
    out_shape = [jax.ShapeDtypeStruct((m, n), out_dtype)]
    scratch = [((nb, k, tn), _BF16)] * parts
    if emit_h:
        out_specs.append(pl.BlockSpec((tm, k), lambda i, j: (i, 0)))
        out_shape.append(jax.ShapeDtypeStruct((m, k), _BF16))
        windows.append(((tm, k), _BF16))
    elif not normed:
        scratch = [((tm, k), _BF16)] + scratch
    outs = pl.pallas_call(
        functools.partial(_inproj_kernel, epilogue, parts, bias is not None, normed),
        grid=(m // tm, nb),
        in_specs=in_specs,
        out_specs=out_specs,
        out_shape=out_shape,
        scratch_shapes=[pltpu.VMEM(s, dt) for s, dt in scratch],
        compiler_params=_params(("arbitrary", "arbitrary"), windows, sum(_nbytes(s, dt) for s, dt in scratch)),
        name=name,
    )(*operands)
    return outs if emit_h else outs[0]


def _mix_gmlp_kernel(groups, chunk, u_ref, v_ref, lng_ref, lnb_ref, ws_ref, bias_ref, y_ref, vn_ref, vnb_ref):
    tm, d = y_ref.shape
    gd = d // groups

    def normalise(rows):
        vn = _ln_rows(v_ref[rows, :].astype(_F32), lng_ref[...], lnb_ref[...])
        vn_ref[rows, :] = vn
        vnb_ref[rows, :] = vn.astype(vnb_ref.dtype)
    _for_row_blocks(tm, normalise)

    def mix_chunk(c, carry):
        rows = pl.ds(pl.multiple_of(c * chunk, chunk), chunk)
        for g in range(groups):
            cols = slice(g * gd, (g + 1) * gd)
            mixed = _dot(ws_ref[0, g], vnb_ref[rows, cols]) + bias_ref[0, :, cols]
            y_ref[rows, cols] = (u_ref[rows, cols].astype(_F32) * mixed).astype(y_ref.dtype)
        return carry
    lax.fori_loop(0, tm // chunk, mix_chunk, 0)


def _mix_gmlp(z, ln_g, ln_b, ws_all, bias_all, n_prompt, n_sample):
    m, d2 = z.shape
    d = d2 // 2
    tm = _TM_MIX
    n_pt = n_prompt // tm
    groups, chunk = ws_all.shape[1], ws_all.shape[2]
    s_tiles = n_sample // tm
    windows = [((tm, d), _BF16)] * 3 + [((tm, d), _F32), ((1, groups, chunk, chunk), _BF16), ((1, chunk, d), _F32)]
    y, vn = pl.pallas_call(
        functools.partial(_mix_gmlp_kernel, groups, chunk),
        grid=(m // tm,),
        in_specs=[
            pl.BlockSpec((tm, d), lambda i: (i, 0)),
            pl.BlockSpec((tm, d), lambda i: (i, 1)),
            pl.BlockSpec((1, d), lambda i: (0, 0)),
            pl.BlockSpec((1, d), lambda i: (0, 0)),
            pl.BlockSpec((1, groups, chunk, chunk), lambda i: (jnp.where(i < n_pt, 0, 1), 0, 0, 0)),
            pl.BlockSpec((1, chunk, d), lambda i: (jnp.where(i < n_pt, 0, 1), 0, 0)),
        ],
        out_specs=[
            pl.BlockSpec((tm, d), lambda i: (i, 0)),
            pl.BlockSpec((tm, d), lambda i: (jnp.maximum(i - n_pt, 0), 0)),
        ],
        out_shape=[jax.ShapeDtypeStruct((m, d), _BF16), jax.ShapeDtypeStruct((s_tiles * tm, d), _F32)],
        scratch_shapes=[pltpu.VMEM((tm, d), _BF16)],
        compiler_params=_params(("arbitrary",), windows, _nbytes((tm, d), _BF16)),
        name="mix_gmlp",
    )(z, z, ln_g.reshape(1, d), ln_b.reshape(1, d), ws_all, bias_all)
    return y, vn


def _mix_conv3_kernel(n_pt, seq, dseq, bg_ref, cx_ref, p1_ref, p2_ref, cw_ref, y_ref, ext_ref):
    i = pl.program_id(0)
    tm, d = y_ref.shape
    halo = _SUBLANES
    is_sample = i >= n_pt

    @pl.when(i == 0)
    def _():
        ext_ref[0:halo, :] = jnp.zeros((halo, d), _F32)

    @pl.when(i > 0)
    def _():
        ext_ref[0:halo, :] = ext_ref[tm:tm + halo, :]

    ext_ref[halo:halo + tm, :] = cx_ref[...]
    seg = jnp.where(is_sample, dseq, seq)

    def conv(rows):
        r0 = rows.start
        pos = (i * tm + r0 + lax.broadcasted_iota(jnp.int32, (_ROWS, 1), 0)) & (seg - 1)
        win = ext_ref[pl.ds(r0, halo + _ROWS), :]
        cur = win[halo:]
        prev1 = jnp.where(pos >= 1, win[halo - 1:halo - 1 + _ROWS], jnp.where(is_sample, p1_ref[rows, :], 0.0))
        prev2 = jnp.where(pos >= 2, win[halo - 2:halo - 2 + _ROWS], jnp.where(is_sample, p2_ref[rows, :], 0.0))
        acc = cw_ref[0:1, :] * prev2 + cw_ref[1:2, :] * prev1 + cw_ref[2:3, :] * cur
        y_ref[rows, :] = (bg_ref[rows, :].astype(_F32) * acc).astype(y_ref.dtype)
    _for_row_blocks(tm, conv)


def _mix_conv3(bg, cx, p1, p2, conv_w, n_prompt, seq, dseq):
    m, d = cx.shape
    tm = _TM_MIX
    n_pt = n_prompt // tm
    windows = [((tm, d), _BF16)] * 2 + [((tm, d), _F32)] * 3
    return pl.pallas_call(
        functools.partial(_mix_conv3_kernel, n_pt, seq, dseq),
        grid=(m // tm,),
        in_specs=[
            pl.BlockSpec((tm, d), lambda i: (i, 0)),
            pl.BlockSpec((tm, d), lambda i: (i, 0)),
            pl.BlockSpec((tm, d), lambda i: (jnp.maximum(i - n_pt, 0), 0)),
            pl.BlockSpec((tm, d), lambda i: (jnp.maximum(i - n_pt, 0), 0)),
            pl.BlockSpec(conv_w.shape, lambda i: (0, 0)),
        ],
        out_specs=pl.BlockSpec((tm, d), lambda i: (i, 0)),
        out_shape=jax.ShapeDtypeStruct((m, d), _BF16),
        scratch_shapes=[pltpu.VMEM((tm + _SUBLANES, d), _F32)],
        compiler_params=_params(("arbitrary",), windows, _nbytes((tm + _SUBLANES, d), _F32)),
        name="mix_conv3",
    )(bg, cx, p1, p2, conv_w)


def _depthwise_taps(width, fill, slab_of, ext_ref, cw_ref, c_ref, n_blocks, in_row, out_row):
    d = c_ref.shape[1]

    def lane_block(l, carry):
        cols = pl.ds(pl.multiple_of(l * _LANES, _LANES), _LANES)
        fill(l, cols)
        slab = slab_of(l)
        w = [jnp.broadcast_to(cw_ref[k:k + 1, cols], (_SUBLANES, _LANES)) for k in range(width)]

        def row_block(b, carry2):
            base = in_row(b)
            partial = [None] * _TAP_CHAINS
            for k in range(width):
                term = w[k] * ext_ref[slab, pl.ds(base + k, _SUBLANES), :]
                c = k % _TAP_CHAINS
                partial[c] = term if partial[c] is None else partial[c] + term
            while len(partial) > 1:
                partial = [a + b_ for a, b_ in zip(partial[0::2], partial[1::2])] + (
                    [partial[-1]] if len(partial) % 2 else [])
            c_ref[pl.ds(pl.multiple_of(out_row(b), _SUBLANES), _SUBLANES), cols] = partial[0]
            return carry2
        lax.fori_loop(0, n_blocks, row_block, 0, unroll=8)
        return carry
    lax.fori_loop(0, d // _LANES, lane_block, 0)


def _conformer_tail(c_ref, cb_ref, lng_ref, lnb_ref, y_ref):
    def tail(rows):
        c = _ln_rows(c_ref[rows, :] + cb_ref[...], lng_ref[...], lnb_ref[...])
        y_ref[rows, :] = (c * jax.nn.sigmoid(c)).astype(y_ref.dtype)
    _for_row_blocks(y_ref.shape[0], tail)


def _mix_conv31_kernel(width, n_pt, seq_tiles, dseq, glu_ref, past_ref, cw_ref, cb_ref, lng_ref, lnb_ref, y_ref,
                       state_ref, ext_ref, sext_ref, c_ref):
    i = pl.program_id(0)
    tm, d = y_ref.shape
    halo = _CONV_HALO
    lead = halo - (width - 1)

    @pl.when(i < n_pt)
    def _():
        def fill(l, cols):
            @pl.when(lax.rem(i, seq_tiles) == 0)
            def _():
                ext_ref[l, 0:halo, :] = jnp.zeros((halo, _LANES), _F32)

            @pl.when(lax.rem(i, seq_tiles) != 0)
            def _():
                ext_ref[l, 0:halo, :] = ext_ref[l, tm:tm + halo, :]

            ext_ref[l, halo:halo + tm, :] = glu_ref[:, cols]

        _depthwise_taps(width, fill, lambda l: l, ext_ref, cw_ref, c_ref, tm // _SUBLANES,
                        lambda b: b * _SUBLANES + lead, lambda b: b * _SUBLANES)

    @pl.when(i >= n_pt)
    def _():
        slab = halo + dseq

        def fill(l, cols):
            def one_sequence(s, carry):
                sext_ref[0, pl.ds(s * slab + lead, width - 1), :] = past_ref[s, :, cols]
                sext_ref[0, pl.ds(pl.multiple_of(s * slab + halo, _SUBLANES), dseq), :] = (
                    glu_ref[pl.ds(pl.multiple_of(s * dseq, _SUBLANES), dseq), cols])
                state_ref[s, :, cols] = sext_ref[0, pl.ds(s * slab + lead + dseq, width - 1), :]
                return carry
            lax.fori_loop(0, tm // dseq, one_sequence, 0)

        _depthwise_taps(width, fill, lambda l: 0, sext_ref, cw_ref, c_ref, tm // dseq,
                        lambda s: s * slab + lead, lambda s: s * dseq)

    _conformer_tail(c_ref, cb_ref, lng_ref, lnb_ref, y_ref)


def _mix_conv31(glu, past, layer, conv_w, conv_b, ln_g, ln_b, n_prompt, seq, dseq):
    m, d = glu.shape
    width = conv_w.shape[0]
    assert width - 1 <= _CONV_HALO and dseq == _SUBLANES
    tm = _TM_CONV
    n_pt = n_prompt // tm
    n_seq = tm // dseq
    slab_rows = n_seq * (_CONV_HALO + dseq)
    row = pl.BlockSpec((1, d), lambda i: (0, 0))
    scratch = [((d // _LANES, tm + _CONV_HALO, _LANES), _F32), ((1, slab_rows, _LANES), _F32), ((tm, d), _F32)]
    windows = [((tm, d), _F32), ((n_seq, _CONV_HALO, d), _F32), ((tm, d), _BF16), ((n_seq, _CONV_HALO, d), _F32)]
    seq_block = lambda i: jnp.maximum(i - n_pt, 0)
    return pl.pallas_call(
        functools.partial(_mix_conv31_kernel, width, n_pt, seq // tm, dseq),
        grid=(m // tm,),
        in_specs=[pl.BlockSpec((tm, d), lambda i: (i, 0)),
                  pl.BlockSpec((None, n_seq, width - 1, d), lambda i: (layer, seq_block(i), 0, 0)),
                  pl.BlockSpec(conv_w.shape, lambda i: (0, 0)), row, row, row],
        out_specs=[pl.BlockSpec((tm, d), lambda i: (i, 0)),
                   pl.BlockSpec((n_seq, width - 1, d), lambda i: (seq_block(i), 0, 0))],
        out_shape=[jax.ShapeDtypeStruct((m, d), _BF16),
                   jax.ShapeDtypeStruct((past.shape[1], width - 1, d), _F32)],
        scratch_shapes=[pltpu.VMEM(s, dt) for s, dt in scratch],
        compiler_params=_params(("arbitrary",), windows, sum(_nbytes(s, dt) for s, dt in scratch)),
        name="mix_conv31",
    )(glu, past, conv_w, conv_b.reshape(1, d), ln_g.reshape(1, d), ln_b.reshape(1, d))


def _outproj_kernel(has_bias, y_ref, w_ref, *refs):
    if has_bias:
        b_ref, x_ref, xo_ref, wbf_ref = refs
    else:
        x_ref, xo_ref, wbf_ref = refs
    (w,) = _resident_weights([w_ref], [wbf_ref])
    acc = _dot(y_ref[...], w)
    if has_bias:
        acc = acc + b_ref[...]
    xo_ref[...] = x_ref[...] + acc


def _outproj(y, layer, w, bias, x):
    m, k = y.shape
    n = w.shape[2]
    tm, tn = _TM, _TN_OUT
    nb = n // tn
    in_specs = [pl.BlockSpec((tm, k), lambda i, j: (i, 0)),
                pl.BlockSpec((None, k, tn), _resident_weight_block(layer, 0, nb))]
    operands = [y, w]
    if bias is not None:
        in_specs.append(pl.BlockSpec((None, 1, tn), lambda i, j: (layer, 0, j)))
        operands.append(bias.reshape(bias.shape[0], 1, n))
    in_specs.append(pl.BlockSpec((tm, tn), lambda i, j: (i, j)))
    operands.append(x)
    windows = [((tm, k), _BF16), ((k, tn), _F32), ((tm, tn), _F32), ((tm, tn), _F32)]
    return pl.pallas_call(
        functools.partial(_outproj_kernel, bias is not None),
        grid=(m // tm, nb),
        in_specs=in_specs,
        out_specs=pl.BlockSpec((tm, tn), lambda i, j: (i, j)),
        out_shape=jax.ShapeDtypeStruct((m, n), _F32),
        scratch_shapes=[pltpu.VMEM((nb, k, tn), _BF16)],
        compiler_params=_params(("arbitrary", "arbitrary"), windows, _nbytes((nb, k, tn), _BF16)),
        name="outproj",
    )(*operands)


def _ffn_kernel(final, x_ref, g_ref, wu_ref, wd_ref, *refs):
    if final:
        gf_ref, o_ref, h_ref = refs
    else:
        o_ref, h_ref = refs
    f = pl.program_id(1)

    @pl.when(f == 0)
    def _():
        def start(rows):
            x = x_ref[rows, :]
            o_ref[rows, :] = x
            h_ref[rows, :] = _rms_rows(x, g_ref[...]).astype(h_ref.dtype)
        _for_row_blocks(o_ref.shape[0], start)

    a = jnp.maximum(_dot(h_ref[...], wu_ref[...].astype(_BF16)), 0.0)
    o_ref[...] += _dot((a * a).astype(_BF16), wd_ref[...].astype(_BF16))

    if final:
        @pl.when(f == pl.num_programs(1) - 1)
        def _():
            def finish(rows):
                o_ref[rows, :] = _rms_rows(o_ref[rows, :], gf_ref[...])
            _for_row_blocks(o_ref.shape[0], finish)


def _ffn(x, g, layer, w_up, w_down, row0, rows, g_final=None):
    d = x.shape[1]
    dff = w_up.shape[2]
    tm, tf = _TM, _TF
    t0 = row0 // tm
    in_specs = [pl.BlockSpec((tm, d), lambda i, f: (i + t0, 0)),
                pl.BlockSpec((None, 1, d), lambda i, f: (layer, 0, 0)),
                pl.BlockSpec((None, d, tf), lambda i, f: (layer, 0, f)),
                pl.BlockSpec((None, tf, d), lambda i, f: (layer, f, 0))]
    operands = [x, g.reshape(g.shape[0], 1, d), w_up, w_down]
    if g_final is not None:
        in_specs.append(pl.BlockSpec((1, d), lambda i, f: (0, 0)))
        operands.append(g_final.reshape(1, d))
    windows = [((tm, d), _F32), ((d, tf), _F32), ((tf, d), _F32), ((tm, d), _F32)]
    return pl.pallas_call(
        functools.partial(_ffn_kernel, g_final is not None),
        grid=(rows // tm, dff // tf),
        in_specs=in_specs,
        out_specs=pl.BlockSpec((tm, d), lambda i, f: (i, 0)),
        out_shape=jax.ShapeDtypeStruct((rows, d), _F32),
        scratch_shapes=[pltpu.VMEM((tm, d), _BF16)],
        compiler_params=_params(("arbitrary", "arbitrary"), windows, _nbytes((tm, d), _BF16)),
        name="ffn",
    )(*operands)


def _gating_operands(w_s, b_s, dseq, d):
    groups, chunk, _ = w_s.shape
    t = jnp.arange(chunk)
    causal = t[:, None] >= t[None, :]
    same_seq = (t[:, None] // dseq) == (t[None, :] // dseq)
    reps = chunk // dseq
    ws_prompt = jnp.where(causal, w_s, 0)
    ws_sample = jnp.where(causal & same_seq, jnp.tile(w_s[:, :dseq, :dseq], (1, reps, reps)), 0)
    gd = d // groups
    bias_prompt = jnp.repeat(b_s.T, gd, axis=1)
    bias_sample = jnp.repeat(jnp.tile(b_s[:, :dseq].T, (reps, 1)), gd, axis=1)
    return jnp.stack([ws_prompt, ws_sample]).astype(_BF16), jnp.stack([bias_prompt, bias_sample])


def _prompt_conv_state(t, bsz, seq, keep):
    return jnp.stack([lax.slice_in_dim(t, (b + 1) * seq - keep, (b + 1) * seq, axis=0) for b in range(bsz)])


def _sample_conv_state(past, t, n_prompt, dseq):
    dbsz, keep, d = past.shape
    cur = lax.slice_in_dim(t, n_prompt, n_prompt + dbsz * dseq, axis=0).reshape(dbsz, dseq, d)
    return jnp.concatenate([past, cur], axis=1)[:, -keep:]


def kernel(x_prompt, x_sample, state_b_conv, state_c_conv, norm_mix_g, norm_ffn_g, final_norm_g, a_w_in, a_ln_g, a_ln_b, a_w_s, a_b_s, a_w_out, b_w_in, b_conv_w, b_w_out, c_w_pw1, c_b_pw1, c_conv_w, c_conv_b, c_ln_g, c_ln_b, c_w_pw2, c_b_pw2, ffn_w_up, ffn_w_down):
    bsz, seq, d = x_prompt.shape
    dbsz, dseq, _ = x_sample.shape
    n_p, n_s = bsz * seq, dbsz * dseq
    depth = norm_mix_g.shape[0]
    chunk = a_w_s.shape[-1]
    assert seq % chunk == 0 and chunk % dseq == 0 and seq & (seq - 1) == 0 and dseq & (dseq - 1) == 0
    assert b_conv_w.shape[1] == 3 and seq % _TM_MIX == 0 and n_s % _TM_MIX == 0
    assert n_p % _TM == 0 and n_s % _TM == 0

    x = jnp.concatenate([x_prompt.reshape(n_p, d), x_sample.reshape(n_s, d)], axis=0)
    a_v, b_prompt, b_sample, c_prompt, c_sample = [], [], [], [], []
    for i in range(depth):
        j, kind = divmod(i, 3)
        if kind == 0:
            z = _inproj(_gelu_epilogue, x, norm_mix_g, i, a_w_in, None, j, 0, 1, 2 * d, 768, 1024, _BF16,
                        "inproj_gelu")
            ws_all, bias_all = _gating_operands(a_w_s[j], a_b_s[j], dseq, d)
            y, vn = _mix_gmlp(z, a_ln_g[j], a_ln_b[j], ws_all, bias_all, n_p, n_s)
            a_v.append(vn.reshape(dbsz, dseq, d))
            x = _outproj(y, j, a_w_out, None, x)
        elif kind == 1:
            bg, h = _inproj(_plain_epilogue, x, norm_mix_g, i, b_w_in, None, j, 0, 1, d, 768, 1024, _BF16,
                            "inproj_gate", emit_h=True)
            cx = _inproj(_product_epilogue, h, None, None, b_w_in, None, j, 1, 2, d, 768, 512, _F32,
                         "inproj_product")
            past = state_b_conv[j]
            p2 = jnp.pad(past, ((0, 0), (0, dseq - 2), (0, 0))).reshape(n_s, d)
            p1 = jnp.pad(past[:, 1:], ((0, 0), (0, dseq - 1), (0, 0))).reshape(n_s, d)
            y = _mix_conv3(bg, cx, p1, p2, b_conv_w[j], n_p, seq, dseq)
            b_prompt.append(_prompt_conv_state(cx, bsz, seq, past.shape[1]))
            b_sample.append(_sample_conv_state(past, cx, n_p, dseq))
            x = _outproj(y, j, b_w_out, None, x)
        else:
            glu = _inproj(_glu_epilogue, x, norm_mix_g, i, c_w_pw1, c_b_pw1, j, 0, 2, d, 768, 512, _F32,
                          "inproj_glu")
            y, sample_state = _mix_conv31(glu, state_c_conv, j, c_conv_w[j], c_conv_b[j], c_ln_g[j], c_ln_b[j],
                                          n_p, seq, dseq)
            c_prompt.append(_prompt_conv_state(glu, bsz, seq, state_c_conv.shape[2]))
            c_sample.append(sample_state)
            x = _outproj(y, j, c_w_pw2, c_b_pw2, x)
        if i + 1 < depth:
            x = _ffn(x, norm_ffn_g, i, ffn_w_up, ffn_w_down, 0, n_p + n_s)
        else:
            y_prompt = _ffn(x, norm_ffn_g, i, ffn_w_up, ffn_w_down, 0, n_p, final_norm_g)
            y_sample = _ffn(x, norm_ffn_g, i, ffn_w_up, ffn_w_down, n_p, n_s, final_norm_g)
    return (y_prompt.reshape(bsz, seq, d), y_sample.reshape(dbsz, dseq, d), jnp.stack(a_v),
            jnp.stack(b_prompt), jnp.stack(b_sample), jnp.stack(c_prompt), jnp.stack(c_sample))
```

```python
import functools

import jax
import jax.numpy as jnp
from jax import lax
from jax.experimental import pallas as pl
from jax.experimental.pallas import tpu as pltpu

_F32 = jnp.float32
_BF16 = jnp.bfloat16
_EPS = 1e-6

_LANES = 128
_SUBLANES = 8
_VMEM_CAP_BYTES = 60000 * 1024
_VMEM_SLACK_BYTES = 12 * 1024 * 1024

_TM = 1024
_TM_MIX = 512
_TN_OUT = 1024
_TF = 512
_TM_CONV = 256
_ROWS = 16
_ROWS_IN_PLACE = 64
_CONV_HALO = 32
_TAP_CHAINS = 2


def _nbytes(shape, dtype):
    n = 1
    for s in shape:
        n *= s
    return n * jnp.dtype(dtype).itemsize


def _params(semantics, windows, scratch=0):
    limit = 2 * sum(_nbytes(s, d) for s, d in windows) + scratch + _VMEM_SLACK_BYTES
    return pltpu.CompilerParams(dimension_semantics=semantics, vmem_limit_bytes=min(limit, _VMEM_CAP_BYTES))


def _rms_rows(x, g):
    return x * lax.rsqrt(jnp.mean(x * x, axis=-1, keepdims=True) + _EPS) * g


def _ln_rows(x, g, b):
    xc = x - jnp.mean(x, axis=-1, keepdims=True)
    return xc * lax.rsqrt(jnp.mean(xc * xc, axis=-1, keepdims=True) + _EPS) * g + b


def _for_row_blocks(n_rows, fn, rows=_ROWS, unroll=4):
    def body(r, carry):
        fn(pl.ds(pl.multiple_of(r * rows, rows), rows))
        return carry
    lax.fori_loop(0, n_rows // rows, body, 0, unroll=unroll)


def _dot(a, b):
    return jnp.dot(a, b, preferred_element_type=_F32)


def _gelu_epilogue(accs, out_ref):
    (acc,) = accs
    out_ref[...] = (0.5 * acc * (1.0 + lax.erf(acc * 0.7071067811865476))).astype(out_ref.dtype)


def _plain_epilogue(accs, out_ref):
    (acc,) = accs
    out_ref[...] = acc.astype(out_ref.dtype)


def _product_epilogue(accs, out_ref):
    out_ref[...] = (accs[0] * accs[1]).astype(out_ref.dtype)


def _glu_epilogue(accs, out_ref):
    out_ref[...] = (accs[0] * jax.nn.sigmoid(accs[1])).astype(out_ref.dtype)


def _resident_weight_block(layer, first_block, nb):
    return lambda i, j: (layer, 0, first_block + jnp.where(i == 0, j, nb - 1))


def _resident_weights(w_refs, wbf_refs):
    j = pl.program_id(1)

    @pl.when(pl.program_id(0) == 0)
    def _():
        for w_ref, wbf_ref in zip(w_refs, wbf_refs):
            wbf_ref[j] = w_ref[...].astype(wbf_ref.dtype)
    return [wbf_ref[j] for wbf_ref in wbf_refs]


def _inproj_kernel(epilogue, parts, has_bias, normed, x_ref, *refs):
    if not normed:
        g_ref, refs = refs[0], refs[1:]
    w_refs, refs = refs[:parts], refs[parts:]
    b_refs, refs = (refs[:parts], refs[parts:]) if has_bias else ((), refs)
    out_ref, refs = refs[0], refs[1:]
    if normed:
        h_ref = x_ref
    else:
        h_ref, refs = refs[0], refs[1:]

        @pl.when(pl.program_id(1) == 0)
        def _():
            def norm(rows):
                h_ref[rows, :] = _rms_rows(x_ref[rows, :], g_ref[...]).astype(h_ref.dtype)
            _for_row_blocks(h_ref.shape[0], norm)

    ws = _resident_weights(w_refs, refs)
    h = h_ref[...]
    accs = [_dot(h, w) for w in ws]
    if has_bias:
        accs = [acc + b_ref[...] for acc, b_ref in zip(accs, b_refs)]
    epilogue(accs, out_ref)


def _inproj(epilogue, x, g, g_layer, w, bias, layer, first_part, parts, n, tm, tn, out_dtype, name, emit_h=False):
    m, k = x.shape
    nb = n // tn
    normed = g is None
    in_specs = [pl.BlockSpec((tm, k), lambda i, j: (i, 0))]
    operands = [x]
    if not normed:
        in_specs.append(pl.BlockSpec((None, 1, k), lambda i, j: (g_layer, 0, 0)))
        operands.append(g.reshape(g.shape[0], 1, k))
    for p in range(parts):
        in_specs.append(pl.BlockSpec((None, k, tn), _resident_weight_block(layer, (first_part + p) * nb, nb)))
        operands.append(w)
    if bias is not None:
        for p in range(parts):
            in_specs.append(pl.BlockSpec((None, 1, tn), functools.partial(
                lambda i, j, p: (layer, 0, (first_part + p) * nb + j), p=p)))
            operands.append(bias.reshape(bias.shape[0], 1, -1))
    windows = [((tm, k), x.dtype)] + [((k, tn), _F32)] * parts + [((tm, tn), out_dtype)]
    out_specs = [pl.BlockSpec((tm, tn), lambda i, j: (i, j))]
    out_shape = [jax.ShapeDtypeStruct((m, n), out_dtype)]
    scratch = [((nb, k, tn), _BF16)] * parts
    if emit_h:
        out_specs.append(pl.BlockSpec((tm, k), lambda i, j: (i, 0)))
        out_shape.append(jax.ShapeDtypeStruct((m, k), _BF16))
        windows.append(((tm, k), _BF16))
    elif not normed:
        scratch = [((tm, k), _BF16)] + scratch
    outs = pl.pallas_call(
        functools.partial(_inproj_kernel, epilogue, parts, bias is not None, normed),
        grid=(m // tm, nb),
        in_specs=in_specs,
        out_specs=out_specs,
        out_shape=out_shape,
        scratch_shapes=[pltpu.VMEM(s, dt) for s, dt in scratch],
        compiler_params=_params(("arbitrary", "arbitrary"), windows, sum(_nbytes(s, dt) for s, dt in scratch)),
        name=name,
    )(*operands)
    return outs if emit_h else outs[0]


def _mix_gmlp_kernel(groups, chunk, u_ref, v_ref, lng_ref, lnb_ref, ws_ref, bias_ref, y_ref, vn_ref, vnb_ref):
    tm, d = y_ref.shape
    gd = d // groups

    def normalise(rows):
        vn = _ln_rows(v_ref[rows, :].astype(_F32), lng_ref[...], lnb_ref[...])
        vn_ref[rows, :] = vn
        vnb_ref[rows, :] = vn.astype(vnb_ref.dtype)
    _for_row_blocks(tm, normalise)

    def mix_chunk(c, carry):
        rows = pl.ds(pl.multiple_of(c * chunk, chunk), chunk)
        for g in range(groups):
            cols = slice(g * gd, (g + 1) * gd)
            mixed = _dot(ws_ref[0, g], vnb_ref[rows, cols]) + bias_ref[0, :, cols]
            y_ref[rows, cols] = (u_ref[rows, cols].astype(_F32) * mixed).astype(y_ref.dtype)
        return carry
    lax.fori_loop(0, tm // chunk, mix_chunk, 0)


def _mix_gmlp(z, ln_g, ln_b, ws_all, bias_all, n_prompt, n_sample):
    m, d2 = z.shape
    d = d2 // 2
    tm = _TM_MIX
    n_pt = n_prompt // tm
    groups, chunk = ws_all.shape[1], ws_all.shape[2]
    s_tiles = n_sample // tm
    windows = [((tm, d), _BF16)] * 3 + [((tm, d), _F32), ((1, groups, chunk, chunk), _BF16), ((1, chunk, d), _F32)]
    y, vn = pl.pallas_call(
        functools.partial(_mix_gmlp_kernel, groups, chunk),
        grid=(m // tm,),
        in_specs=[
            pl.BlockSpec((tm, d), lambda i: (i, 0)),
            pl.BlockSpec((tm, d), lambda i: (i, 1)),
            pl.BlockSpec((1, d), lambda i: (0, 0)),
            pl.BlockSpec((1, d), lambda i: (0, 0)),
            pl.BlockSpec((1, groups, chunk, chunk), lambda i: (jnp.where(i < n_pt, 0, 1), 0, 0, 0)),
            pl.BlockSpec((1, chunk, d), lambda i: (jnp.where(i < n_pt, 0, 1), 0, 0)),
        ],
        out_specs=[
            pl.BlockSpec((tm, d), lambda i: (i, 0)),
            pl.BlockSpec((tm, d), lambda i: (jnp.maximum(i - n_pt, 0), 0)),
        ],
        out_shape=[jax.ShapeDtypeStruct((m, d), _BF16), jax.ShapeDtypeStruct((s_tiles * tm, d), _F32)],
        scratch_shapes=[pltpu.VMEM((tm, d), _BF16)],
        compiler_params=_params(("arbitrary",), windows, _nbytes((tm, d), _BF16)),
        name="mix_gmlp",
    )(z, z, ln_g.reshape(1, d), ln_b.reshape(1, d), ws_all, bias_all)
    return y, vn


def _mix_conv3_kernel(n_pt, seq, dseq, bg_ref, cx_ref, p1_ref, p2_ref, cw_ref, y_ref, ext_ref):
    i = pl.program_id(0)
    tm, d = y_ref.shape
    halo = _SUBLANES
    is_sample = i >= n_pt

    @pl.when(i == 0)
    def _():
        ext_ref[0:halo, :] = jnp.zeros((halo, d), _F32)

    @pl.when(i > 0)
    def _():
        ext_ref[0:halo, :] = ext_ref[tm:tm + halo, :]

    ext_ref[halo:halo + tm, :] = cx_ref[...]
    seg = jnp.where(is_sample, dseq, seq)

    def conv(rows):
        r0 = rows.start
        pos = (i * tm + r0 + lax.broadcasted_iota(jnp.int32, (_ROWS, 1), 0)) & (seg - 1)
        win = ext_ref[pl.ds(r0, halo + _ROWS), :]
        cur = win[halo:]
        prev1 = jnp.where(pos >= 1, win[halo - 1:halo - 1 + _ROWS], jnp.where(is_sample, p1_ref[rows, :], 0.0))
        prev2 = jnp.where(pos >= 2, win[halo - 2:halo - 2 + _ROWS], jnp.where(is_sample, p2_ref[rows, :], 0.0))
        acc = cw_ref[0:1, :] * prev2 + cw_ref[1:2, :] * prev1 + cw_ref[2:3, :] * cur
        y_ref[rows, :] = (bg_ref[rows, :].astype(_F32) * acc).astype(y_ref.dtype)
    _for_row_blocks(tm, conv)


def _mix_conv3(bg, cx, p1, p2, conv_w, n_prompt, seq, dseq):
    m, d = cx.shape
    tm = _TM_MIX
    n_pt = n_prompt // tm
    windows = [((tm, d), _BF16)] * 2 + [((tm, d), _F32)] * 3
    return pl.pallas_call(
        functools.partial(_mix_conv3_kernel, n_pt, seq, dseq),
        grid=(m // tm,),
        in_specs=[
            pl.BlockSpec((tm, d), lambda i: (i, 0)),
            pl.BlockSpec((tm, d), lambda i: (i, 0)),
            pl.BlockSpec((tm, d), lambda i: (jnp.maximum(i - n_pt, 0), 0)),
            pl.BlockSpec((tm, d), lambda i: (jnp.maximum(i - n_pt, 0), 0)),
            pl.BlockSpec(conv_w.shape, lambda i: (0, 0)),
        ],
        out_specs=pl.BlockSpec((tm, d), lambda i: (i, 0)),
        out_shape=jax.ShapeDtypeStruct((m, d), _BF16),
        scratch_shapes=[pltpu.VMEM((tm + _SUBLANES, d), _F32)],
        compiler_params=_params(("arbitrary",), windows, _nbytes((tm + _SUBLANES, d), _F32)),
        name="mix_conv3",
    )(bg, cx, p1, p2, conv_w)


def _depthwise_taps(width, fill, slab_of, ext_ref, cw_ref, c_ref, n_blocks, in_row, out_row):
    d = c_ref.shape[1]

    def lane_block(l, carry):
        cols = pl.ds(pl.multiple_of(l * _LANES, _LANES), _LANES)
        fill(l, cols)
        slab = slab_of(l)
        w = [jnp.broadcast_to(cw_ref[k:k + 1, cols], (_SUBLANES, _LANES)) for k in range(width)]

        def row_block(b, carry2):
            base = in_row(b)
            partial = [None] * _TAP_CHAINS
            for k in range(width):
                term = w[k] * ext_ref[slab, pl.ds(base + k, _SUBLANES), :]
                c = k % _TAP_CHAINS
                partial[c] = term if partial[c] is None else partial[c] + term
            while len(partial) > 1:
                partial = [a + b_ for a, b_ in zip(partial[0::2], partial[1::2])] + (
                    [partial[-1]] if len(partial) % 2 else [])
            c_ref[pl.ds(pl.multiple_of(out_row(b), _SUBLANES), _SUBLANES), cols] = partial[0]
            return carry2
        lax.fori_loop(0, n_blocks, row_block, 0, unroll=8)
        return carry
    lax.fori_loop(0, d // _LANES, lane_block, 0)


def _conformer_tail(c_ref, cb_ref, lng_ref, lnb_ref, y_ref):
    def tail(rows):
        c = _ln_rows(c_ref[rows, :] + cb_ref[...], lng_ref[...], lnb_ref[...])
        y_ref[rows, :] = (c * jax.nn.sigmoid(c)).astype(y_ref.dtype)
    _for_row_blocks(y_ref.shape[0], tail)


def _mix_conv31_kernel(width, n_pt, seq_tiles, dseq, glu_ref, past_ref, cw_ref, cb_ref, lng_ref, lnb_ref, y_ref,
                       state_ref, ext_ref, sext_ref, c_ref):
    i = pl.program_id(0)
    tm, d = y_ref.shape
    halo = _CONV_HALO
    lead = halo - (width - 1)

    @pl.when(i < n_pt)
    def _():
        def fill(l, cols):
            @pl.when(lax.rem(i, seq_tiles) == 0)
            def _():
                ext_ref[l, 0:halo, :] = jnp.zeros((halo, _LANES), _F32)

            @pl.when(lax.rem(i, seq_tiles) != 0)
            def _():
                ext_ref[l, 0:halo, :] = ext_ref[l, tm:tm + halo, :]

            ext_ref[l, halo:halo + tm, :] = glu_ref[:, cols]

        _depthwise_taps(width, fill, lambda l: l, ext_ref, cw_ref, c_ref, tm // _SUBLANES,
                        lambda b: b * _SUBLANES + lead, lambda b: b * _SUBLANES)

    @pl.when(i >= n_pt)
    def _():
        slab = halo + dseq

        def fill(l, cols):
            def one_sequence(s, carry):
                sext_ref[0, pl.ds(s * slab + lead, width - 1), :] = past_ref[s, :, cols]
                sext_ref[0, pl.ds(pl.multiple_of(s * slab + halo, _SUBLANES), dseq), :] = (
                    glu_ref[pl.ds(pl.multiple_of(s * dseq, _SUBLANES), dseq), cols])
                state_ref[s, :, cols] = sext_ref[0, pl.ds(s * slab + lead + dseq, width - 1), :]
                return carry
            lax.fori_loop(0, tm // dseq, one_sequence, 0)

        _depthwise_taps(width, fill, lambda l: 0, sext_ref, cw_ref, c_ref, tm // dseq,
                        lambda s: s * slab + lead, lambda s: s * dseq)

    _conformer_tail(c_ref, cb_ref, lng_ref, lnb_ref, y_ref)


def _mix_conv31(glu, past, layer, conv_w, conv_b, ln_g, ln_b, n_prompt, seq, dseq):
    m, d = glu.shape
    width = conv_w.shape[0]
    assert width - 1 <= _CONV_HALO and dseq == _SUBLANES
    tm = _TM_CONV
    n_pt = n_prompt // tm
    n_seq = tm // dseq
    slab_rows = n_seq * (_CONV_HALO + dseq)
    row = pl.BlockSpec((1, d), lambda i: (0, 0))
    scratch = [((d // _LANES, tm + _CONV_HALO, _LANES), _F32), ((1, slab_rows, _LANES), _F32), ((tm, d), _F32)]
    windows = [((tm, d), _F32), ((n_seq, _CONV_HALO, d), _F32), ((tm, d), _BF16), ((n_seq, _CONV_HALO, d), _F32)]
    seq_block = lambda i: jnp.maximum(i - n_pt, 0)
    return pl.pallas_call(
        functools.partial(_mix_conv31_kernel, width, n_pt, seq // tm, dseq),
        grid=(m // tm,),
        in_specs=[pl.BlockSpec((tm, d), lambda i: (i, 0)),
                  pl.BlockSpec((None, n_seq, width - 1, d), lambda i: (layer, seq_block(i), 0, 0)),
                  pl.BlockSpec(conv_w.shape, lambda i: (0, 0)), row, row, row],
        out_specs=[pl.BlockSpec((tm, d), lambda i: (i, 0)),
                   pl.BlockSpec((n_seq, width - 1, d), lambda i: (seq_block(i), 0, 0))],
        out_shape=[jax.ShapeDtypeStruct((m, d), _BF16),
                   jax.ShapeDtypeStruct((past.shape[1], width - 1, d), _F32)],
        scratch_shapes=[pltpu.VMEM(s, dt) for s, dt in scratch],
        compiler_params=_params(("arbitrary",), windows, sum(_nbytes(s, dt) for s, dt in scratch)),
        name="mix_conv31",
    )(glu, past, conv_w, conv_b.reshape(1, d), ln_g.reshape(1, d), ln_b.reshape(1, d))


def _outproj_kernel(has_bias, y_ref, w_ref, *refs):
    if has_bias:
        b_ref, x_ref, xo_ref, wbf_ref = refs
    else:
        x_ref, xo_ref, wbf_ref = refs
    (w,) = _resident_weights([w_ref], [wbf_ref])
    acc = _dot(y_ref[...], w)
    if has_bias:
        acc = acc + b_ref[...]
    xo_ref[...] = x_ref[...] + acc


def _outproj(y, layer, w, bias, x):
    m, k = y.shape
    n = w.shape[2]
    tm, tn = _TM, _TN_OUT
    nb = n // tn
    in_specs = [pl.BlockSpec((tm, k), lambda i, j: (i, 0)),
                pl.BlockSpec((None, k, tn), _resident_weight_block(layer, 0, nb))]
    operands = [y, w]
    if bias is not None:
        in_specs.append(pl.BlockSpec((None, 1, tn), lambda i, j: (layer, 0, j)))
        operands.append(bias.reshape(bias.shape[0], 1, n))
    in_specs.append(pl.BlockSpec((tm, tn), lambda i, j: (i, j)))
    operands.append(x)
    windows = [((tm, k), _BF16), ((k, tn), _F32), ((tm, tn), _F32), ((tm, tn), _F32)]
    return pl.pallas_call(
        functools.partial(_outproj_kernel, bias is not None),
        grid=(m // tm, nb),
        in_specs=in_specs,
        out_specs=pl.BlockSpec((tm, tn), lambda i, j: (i, j)),
        out_shape=jax.ShapeDtypeStruct((m, n), _F32),
        scratch_shapes=[pltpu.VMEM((nb, k, tn), _BF16)],
        compiler_params=_params(("arbitrary", "arbitrary"), windows, _nbytes((nb, k, tn), _BF16)),
        name="outproj",
    )(*operands)


def _ffn_kernel(final, x_ref, g_ref, wu_ref, wd_ref, *refs):
    if final:
        gf_ref, o_ref, h_ref = refs
    else:
        o_ref, h_ref = refs
    f = pl.program_id(1)

    @pl.when(f == 0)
    def _():
        def start(rows):
            x = x_ref[rows, :]
            o_ref[rows, :] = x
            h_ref[rows, :] = _rms_rows(x, g_ref[...]).astype(h_ref.dtype)
        _for_row_blocks(o_ref.shape[0], start)

    a = jnp.maximum(_dot(h_ref[...], wu_ref[...].astype(_BF16)), 0.0)
    o_ref[...] += _dot((a * a).astype(_BF16), wd_ref[...].astype(_BF16))

    if final:
        @pl.when(f == pl.num_programs(1) - 1)
        def _():
            def finish(rows):
                o_ref[rows, :] = _rms_rows(o_ref[rows, :], gf_ref[...])
            _for_row_blocks(o_ref.shape[0], finish, rows=_ROWS_IN_PLACE, unroll=1)


def _ffn(x, g, layer, w_up, w_down, row0, rows, g_final=None):
    d = x.shape[1]
    dff = w_up.shape[2]
    tm, tf = _TM, _TF
    t0 = row0 // tm
    in_specs = [pl.BlockSpec((tm, d), lambda i, f: (i + t0, 0)),
                pl.BlockSpec((None, 1, d), lambda i, f: (layer, 0, 0)),
                pl.BlockSpec((None, d, tf), lambda i, f: (layer, 0, f)),
                pl.BlockSpec((None, tf, d), lambda i, f: (layer, f, 0))]
    operands = [x, g.reshape(g.shape[0], 1, d), w_up, w_down]
    if g_final is not None:
        in_specs.append(pl.BlockSpec((1, d), lambda i, f: (0, 0)))
        operands.append(g_final.reshape(1, d))
    windows = [((tm, d), _F32), ((d, tf), _F32), ((tf, d), _F32), ((tm, d), _F32)]
    return pl.pallas_call(
        functools.partial(_ffn_kernel, g_final is not None),
        grid=(rows // tm, dff // tf),
        in_specs=in_specs,
        out_specs=pl.BlockSpec((tm, d), lambda i, f: (i, 0)),
        out_shape=jax.ShapeDtypeStruct((rows, d), _F32),
        scratch_shapes=[pltpu.VMEM((tm, d), _BF16)],
        compiler_params=_params(("arbitrary", "arbitrary"), windows, _nbytes((tm, d), _BF16)),
        name="ffn",
    )(*operands)


def _gating_operands(w_s, b_s, dseq, d):
    groups, chunk, _ = w_s.shape
    t = jnp.arange(chunk)
    causal = t[:, None] >= t[None, :]
    same_seq = (t[:, None] // dseq) == (t[None, :] // dseq)
    reps = chunk // dseq
    ws_prompt = jnp.where(causal, w_s, 0)
    ws_sample = jnp.where(causal & same_seq, jnp.tile(w_s[:, :dseq, :dseq], (1, reps, reps)), 0)
    gd = d // groups
    bias_prompt = jnp.repeat(b_s.T, gd, axis=1)
    bias_sample = jnp.repeat(jnp.tile(b_s[:, :dseq].T, (reps, 1)), gd, axis=1)
    return jnp.stack([ws_prompt, ws_sample]).astype(_BF16), jnp.stack([bias_prompt, bias_sample])


def _prompt_conv_state(t, bsz, seq, keep):
    return jnp.stack([lax.slice_in_dim(t, (b + 1) * seq - keep, (b + 1) * seq, axis=0) for b in range(bsz)])


def _sample_conv_state(past, t, n_prompt, dseq):
    dbsz, keep, d = past.shape
    cur = lax.slice_in_dim(t, n_prompt, n_prompt + dbsz * dseq, axis=0).reshape(dbsz, dseq, d)
    return jnp.concatenate([past, cur], axis=1)[:, -keep:]


def kernel(x_prompt, x_sample, state_b_conv, state_c_conv, norm_mix_g, norm_ffn_g, final_norm_g, a_w_in, a_ln_g, a_ln_b, a_w_s, a_b_s, a_w_out, b_w_in, b_conv_w, b_w_out, c_w_pw1, c_b_pw1, c_conv_w, c_conv_b, c_ln_g, c_ln_b, c_w_pw2, c_b_pw2, ffn_w_up, ffn_w_down):
    bsz, seq, d = x_prompt.shape
    dbsz, dseq, _ = x_sample.shape
    n_p, n_s = bsz * seq, dbsz * dseq
    depth = norm_mix_g.shape[0]
    chunk = a_w_s.shape[-1]
    assert seq % chunk == 0 and chunk % dseq == 0 and seq & (seq - 1) == 0 and dseq & (dseq - 1) == 0
    assert b_conv_w.shape[1] == 3 and seq % _TM_MIX == 0 and n_s % _TM_MIX == 0
    assert n_p % _TM == 0 and n_s % _TM == 0

    x = jnp.concatenate([x_prompt.reshape(n_p, d), x_sample.reshape(n_s, d)], axis=0)
    a_v, b_prompt, b_sample, c_prompt, c_sample = [], [], [], [], []
    for i in range(depth):
        j, kind = divmod(i, 3)
        if kind == 0:
            z = _inproj(_gelu_epilogue, x, norm_mix_g, i, a_w_in, None, j, 0, 1, 2 * d, 768, 1024, _BF16,
                        "inproj_gelu")
            ws_all, bias_all = _gating_operands(a_w_s[j], a_b_s[j], dseq, d)
            y, vn = _mix_gmlp(z, a_ln_g[j], a_ln_b[j], ws_all, bias_all, n_p, n_s)
            a_v.append(vn.reshape(dbsz, dseq, d))
            x = _outproj(y, j, a_w_out, None, x)
        elif kind == 1:
            bg, h = _inproj(_plain_epilogue, x, norm_mix_g, i, b_w_in, None, j, 0, 1, d, 768, 1024, _BF16,
                            "inproj_gate", emit_h=True)
            cx = _inproj(_product_epilogue, h, None, None, b_w_in, None, j, 1, 2, d, 768, 512, _F32,
                         "inproj_product")
            past = state_b_conv[j]
            p2 = jnp.pad(past, ((0, 0), (0, dseq - 2), (0, 0))).reshape(n_s, d)
            p1 = jnp.pad(past[:, 1:], ((0, 0), (0, dseq - 1), (0, 0))).reshape(n_s, d)
            y = _mix_conv3(bg, cx, p1, p2, b_conv_w[j], n_p, seq, dseq)
            b_prompt.append(_prompt_conv_state(cx, bsz, seq, past.shape[1]))
            b_sample.append(_sample_conv_state(past, cx, n_p, dseq))
            x = _outproj(y, j, b_w_out, None, x)
        else:
            glu = _inproj(_glu_epilogue, x, norm_mix_g, i, c_w_pw1, c_b_pw1, j, 0, 2, d, 768, 512, _F32,
                          "inproj_glu")
            y, sample_state = _mix_conv31(glu, state_c_conv, j, c_conv_w[j], c_conv_b[j], c_ln_g[j], c_ln_b[j],
                                          n_p, seq, dseq)
            c_prompt.append(_prompt_conv_state(glu, bsz, seq, state_c_conv.shape[2]))
            c_sample.append(sample_state)
            x = _outproj(y, j, c_w_pw2, c_b_pw2, x)
        if i + 1 < depth:
            x = _ffn(x, norm_ffn_g, i, ffn_w_up, ffn_w_down, 0, n_p + n_s)
        else:
            y_prompt = _ffn(x, norm_ffn_g, i, ffn_w_up, ffn_w_down, 0, n_p, final_norm_g)
            y_sample = _ffn(x, norm_ffn_g, i, ffn_w_up, ffn_w_down, n_p, n_s, final_norm_g)
    return (y_prompt.reshape(bsz, seq, d), y_sample.reshape(dbsz, dseq, d), jnp.stack(a_v),
            jnp.stack(b_prompt), jnp.stack(b_sample), jnp.stack(c_prompt), jnp.stack(c_sample))
```

```python
import functools

import jax
import jax.numpy as jnp
from jax import lax
from jax.experimental import pallas as pl
from jax.experimental.pallas import tpu as pltpu

_F32 = jnp.float32
_BF16 = jnp.bfloat16
_EPS = 1e-6

_LANES = 128
_SUBLANES = 8
_VMEM_CAP_BYTES = 60000 * 1024
_VMEM_SLACK_BYTES = 12 * 1024 * 1024

_TM = 1024
_TM_MIX = 512
_TN_OUT = 1024
_TF = 512
_TM_CONV = 256
_ROWS = 16
_ROWS_IN_PLACE = 64
_ROW_BLOCKS_IN_FLIGHT = 8
_CONV_HALO = 32
_TAP_CHAINS = 2


def _nbytes(shape, dtype):
    n = 1
    for s in shape:
        n *= s
    return n * jnp.dtype(dtype).itemsize


def _params(semantics, windows, scratch=0):
    limit = 2 * sum(_nbytes(s, d) for s, d in windows) + scratch + _VMEM_SLACK_BYTES
    return pltpu.CompilerParams(dimension_semantics=semantics, vmem_limit_bytes=min(limit, _VMEM_CAP_BYTES))


def _rms_rows(x, g):
    return x * lax.rsqrt(jnp.mean(x * x, axis=-1, keepdims=True) + _EPS) * g


def _ln_rows(x, g, b):
    xc = x - jnp.mean(x, axis=-1, keepdims=True)
    return xc * lax.rsqrt(jnp.mean(xc * xc, axis=-1, keepdims=True) + _EPS) * g + b


def _for_row_blocks(n_rows, fn, rows=_ROWS, unroll=_ROW_BLOCKS_IN_FLIGHT):
    def body(r, carry):
        fn(pl.ds(pl.multiple_of(r * rows, rows), rows))
        return carry
    lax.fori_loop(0, n_rows // rows, body, 0, unroll=unroll)


def _dot(a, b):
    return jnp.dot(a, b, preferred_element_type=_F32)


def _gelu_epilogue(accs, out_ref):
    (acc,) = accs
    out_ref[...] = (0.5 * acc * (1.0 + lax.erf(acc * 0.7071067811865476))).astype(out_ref.dtype)


def _plain_epilogue(accs, out_ref):
    (acc,) = accs
    out_ref[...] = acc.astype(out_ref.dtype)


def _product_epilogue(accs, out_ref):
    out_ref[...] = (accs[0] * accs[1]).astype(out_ref.dtype)


def _glu_epilogue(accs, out_ref):
    out_ref[...] = (accs[0] * jax.nn.sigmoid(accs[1])).astype(out_ref.dtype)


def _resident_weight_block(layer, first_block, nb):
    return lambda i, j: (layer, 0, first_block + jnp.where(i == 0, j, nb - 1))


def _resident_weights(w_refs, wbf_refs):
    j = pl.program_id(1)

    @pl.when(pl.program_id(0) == 0)
    def _():
        for w_ref, wbf_ref in zip(w_refs, wbf_refs):
            wbf_ref[j] = w_ref[...].astype(wbf_ref.dtype)
    return [wbf_ref[j] for wbf_ref in wbf_refs]


def _inproj_kernel(epilogue, parts, has_bias, normed, x_ref, *refs):
    if not normed:
        g_ref, refs = refs[0], refs[1:]
    w_refs, refs = refs[:parts], refs[parts:]
    b_refs, refs = (refs[:parts], refs[parts:]) if has_bias else ((), refs)
    out_ref, refs = refs[0], refs[1:]
    if normed:
        h_ref = x_ref
    else:
        h_ref, refs = refs[0], refs[1:]

        @pl.when(pl.program_id(1) == 0)
        def _():
            def norm(rows):
                h_ref[rows, :] = _rms_rows(x_ref[rows, :], g_ref[...]).astype(h_ref.dtype)
            _for_row_blocks(h_ref.shape[0], norm)

    ws = _resident_weights(w_refs, refs)
    h = h_ref[...]
    accs = [_dot(h, w) for w in ws]
    if has_bias:
        accs = [acc + b_ref[...] for acc, b_ref in zip(accs, b_refs)]
    epilogue(accs, out_ref)


def _inproj(epilogue, x, g, g_layer, w, bias, layer, first_part, parts, n, tm, tn, out_dtype, name, emit_h=False):
    m, k = x.shape
    nb = n // tn
    normed = g is None
    in_specs = [pl.BlockSpec((tm, k), lambda i, j: (i, 0))]
    operands = [x]
    if not normed:
        in_specs.append(pl.BlockSpec((None, 1, k), lambda i, j: (g_layer, 0, 0)))
        operands.append(g.reshape(g.shape[0], 1, k))
    for p in range(parts):
        in_specs.append(pl.BlockSpec((None, k, tn), _resident_weight_block(layer, (first_part + p) * nb, nb)))
        operands.append(w)
    if bias is not None:
        for p in range(parts):
            in_specs.append(pl.BlockSpec((None, 1, tn), functools.partial(
                lambda i, j, p: (layer, 0, (first_part + p) * nb + j), p=p)))
            operands.append(bias.reshape(bias.shape[0], 1, -1))
    windows = [((tm, k), x.dtype)] + [((k, tn), _F32)] * parts + [((tm, tn), out_dtype)]
    out_specs = [pl.BlockSpec((tm, tn), lambda i, j: (i, j))]
    out_shape = [jax.ShapeDtypeStruct((m, n), out_dtype)]
    scratch = [((nb, k, tn), _BF16)] * parts
    if emit_h:
        out_specs.append(pl.BlockSpec((tm, k), lambda i, j: (i, 0)))
        out_shape.append(jax.ShapeDtypeStruct((m, k), _BF16))
        windows.append(((tm, k), _BF16))
    elif not normed:
        scratch = [((tm, k), _BF16)] + scratch
    outs = pl.pallas_call(
        functools.partial(_inproj_kernel, epilogue, parts, bias is not None, normed),
        grid=(m // tm, nb),
        in_specs=in_specs,
        out_specs=out_specs,
        out_shape=out_shape,
        scratch_shapes=[pltpu.VMEM(s, dt) for s, dt in scratch],
        compiler_params=_params(("arbitrary", "arbitrary"), windows, sum(_nbytes(s, dt) for s, dt in scratch)),
        name=name,
    )(*operands)
    return outs if emit_h else outs[0]


def _mix_gmlp_kernel(groups, chunk, u_ref, v_ref, lng_ref, lnb_ref, ws_ref, bias_ref, y_ref, vn_ref, vnb_ref):
    tm, d = y_ref.shape
    gd = d // groups

    def normalise(rows):
        vn = _ln_rows(v_ref[rows, :].astype(_F32), lng_ref[...], lnb_ref[...])
        vn_ref[rows, :] = vn
        vnb_ref[rows, :] = vn.astype(vnb_ref.dtype)
    _for_row_blocks(tm, normalise)

    def mix_chunk(c, carry):
        rows = pl.ds(pl.multiple_of(c * chunk, chunk), chunk)
        for g in range(groups):
            cols = slice(g * gd, (g + 1) * gd)
            mixed = _dot(ws_ref[0, g], vnb_ref[rows, cols]) + bias_ref[0, :, cols]
            y_ref[rows, cols] = (u_ref[rows, cols].astype(_F32) * mixed).astype(y_ref.dtype)
        return carry
    lax.fori_loop(0, tm // chunk, mix_chunk, 0)


def _mix_gmlp(z, ln_g, ln_b, ws_all, bias_all, n_prompt, n_sample):
    m, d2 = z.shape
    d = d2 // 2
    tm = _TM_MIX
    n_pt = n_prompt // tm
    groups, chunk = ws_all.shape[1], ws_all.shape[2]
    s_tiles = n_sample // tm
    windows = [((tm, d), _BF16)] * 3 + [((tm, d), _F32), ((1, groups, chunk, chunk), _BF16), ((1, chunk, d), _F32)]
    y, vn = pl.pallas_call(
        functools.partial(_mix_gmlp_kernel, groups, chunk),
        grid=(m // tm,),
        in_specs=[
            pl.BlockSpec((tm, d), lambda i: (i, 0)),
            pl.BlockSpec((tm, d), lambda i: (i, 1)),
            pl.BlockSpec((1, d), lambda i: (0, 0)),
            pl.BlockSpec((1, d), lambda i: (0, 0)),
            pl.BlockSpec((1, groups, chunk, chunk), lambda i: (jnp.where(i < n_pt, 0, 1), 0, 0, 0)),
            pl.BlockSpec((1, chunk, d), lambda i: (jnp.where(i < n_pt, 0, 1), 0, 0)),
        ],
        out_specs=[
            pl.BlockSpec((tm, d), lambda i: (i, 0)),
            pl.BlockSpec((tm, d), lambda i: (jnp.maximum(i - n_pt, 0), 0)),
        ],
        out_shape=[jax.ShapeDtypeStruct((m, d), _BF16), jax.ShapeDtypeStruct((s_tiles * tm, d), _F32)],
        scratch_shapes=[pltpu.VMEM((tm, d), _BF16)],
        compiler_params=_params(("arbitrary",), windows, _nbytes((tm, d), _BF16)),
        name="mix_gmlp",
    )(z, z, ln_g.reshape(1, d), ln_b.reshape(1, d), ws_all, bias_all)
    return y, vn


def _mix_conv3_kernel(n_pt, seq_tiles, dseq, bg_ref, cx_ref, p1_ref, p2_ref, cw_ref, y_ref, ext_ref):
    i = pl.program_id(0)
    tm, d = y_ref.shape
    halo = _SUBLANES
    seq_start = lax.rem(i, seq_tiles) == 0

    def conv(l, cols, w, sample, b, carry):
        r0 = pl.multiple_of(b * _ROWS, _ROWS)
        rows = pl.ds(r0, _ROWS)
        cur = ext_ref[l, pl.ds(r0 + halo, _ROWS), :]
        prev1 = ext_ref[l, pl.ds(r0 + halo - 1, _ROWS), :]
        prev2 = ext_ref[l, pl.ds(r0 + halo - 2, _ROWS), :]
        if sample:
            pos = lax.broadcasted_iota(jnp.int32, (_ROWS, 1), 0) & (dseq - 1)
            prev1 = jnp.where(pos >= 1, prev1, p1_ref[rows, cols])
            prev2 = jnp.where(pos >= 2, prev2, p2_ref[rows, cols])
        acc = w[0] * prev2 + w[1] * prev1 + w[2] * cur
        y_ref[rows, cols] = (bg_ref[rows, cols].astype(_F32) * acc).astype(y_ref.dtype)
        return carry

    def lane_block(l, carry):
        cols = pl.ds(pl.multiple_of(l * _LANES, _LANES), _LANES)

        @pl.when(seq_start)
        def _():
            ext_ref[l, 0:halo, :] = jnp.zeros((halo, _LANES), _F32)

        @pl.when(jnp.logical_not(seq_start))
        def _():
            ext_ref[l, 0:halo, :] = ext_ref[l, tm:tm + halo, :]

        ext_ref[l, halo:halo + tm, :] = cx_ref[:, cols]
        w = [jnp.broadcast_to(cw_ref[k:k + 1, cols], (_ROWS, _LANES)) for k in range(3)]

        @pl.when(i < n_pt)
        def _():
            lax.fori_loop(0, tm // _ROWS, functools.partial(conv, l, cols, w, False), 0,
                          unroll=_ROW_BLOCKS_IN_FLIGHT)

        @pl.when(i >= n_pt)
        def _():
            lax.fori_loop(0, tm // _ROWS, functools.partial(conv, l, cols, w, True), 0,
                          unroll=_ROW_BLOCKS_IN_FLIGHT)
        return carry
    lax.fori_loop(0, d // _LANES, lane_block, 0)


def _mix_conv3(bg, cx, p1, p2, conv_w, n_prompt, seq, dseq):
    m, d = cx.shape
    tm = _TM_MIX
    n_pt = n_prompt // tm
    windows = [((tm, d), _BF16)] * 2 + [((tm, d), _F32)] * 3
    return pl.pallas_call(
        functools.partial(_mix_conv3_kernel, n_pt, seq // tm, dseq),
        grid=(m // tm,),
        in_specs=[
            pl.BlockSpec((tm, d), lambda i: (i, 0)),
            pl.BlockSpec((tm, d), lambda i: (i, 0)),
            pl.BlockSpec((tm, d), lambda i: (jnp.maximum(i - n_pt, 0), 0)),
            pl.BlockSpec((tm, d), lambda i: (jnp.maximum(i - n_pt, 0), 0)),
            pl.BlockSpec(conv_w.shape, lambda i: (0, 0)),
        ],
        out_specs=pl.BlockSpec((tm, d), lambda i: (i, 0)),
        out_shape=jax.ShapeDtypeStruct((m, d), _BF16),
        scratch_shapes=[pltpu.VMEM((d // _LANES, tm + _SUBLANES, _LANES), _F32)],
        compiler_params=_params(("arbitrary",), windows, _nbytes((tm + _SUBLANES, d), _F32)),
        name="mix_conv3",
    )(bg, cx, p1, p2, conv_w)


def _depthwise_taps(width, fill, slab_of, ext_ref, cw_ref, c_ref, n_blocks, in_row, out_row):
    d = c_ref.shape[1]

    def lane_block(l, carry):
        cols = pl.ds(pl.multiple_of(l * _LANES, _LANES), _LANES)
        fill(l, cols)
        slab = slab_of(l)
        w = [jnp.broadcast_to(cw_ref[k:k + 1, cols], (_SUBLANES, _LANES)) for k in range(width)]

        def row_block(b, carry2):
            base = in_row(b)
            partial = [None] * _TAP_CHAINS
            for k in range(width):
                term = w[k] * ext_ref[slab, pl.ds(base + k, _SUBLANES), :]
                c = k % _TAP_CHAINS
                partial[c] = term if partial[c] is None else partial[c] + term
            while len(partial) > 1:
                partial = [a + b_ for a, b_ in zip(partial[0::2], partial[1::2])] + (
                    [partial[-1]] if len(partial) % 2 else [])
            c_ref[pl.ds(pl.multiple_of(out_row(b), _SUBLANES), _SUBLANES), cols] = partial[0]
            return carry2
        lax.fori_loop(0, n_blocks, row_block, 0, unroll=8)
        return carry
    lax.fori_loop(0, d // _LANES, lane_block, 0)


def _conformer_tail(c_ref, cb_ref, lng_ref, lnb_ref, y_ref):
    def tail(rows):
        c = _ln_rows(c_ref[rows, :] + cb_ref[...], lng_ref[...], lnb_ref[...])
        y_ref[rows, :] = (c * jax.nn.sigmoid(c)).astype(y_ref.dtype)
    _for_row_blocks(y_ref.shape[0], tail)


def _mix_conv31_kernel(width, n_pt, seq_tiles, dseq, glu_ref, past_ref, cw_ref, cb_ref, lng_ref, lnb_ref, y_ref,
                       state_ref, ext_ref, sext_ref, c_ref):
    i = pl.program_id(0)
    tm, d = y_ref.shape
    halo = _CONV_HALO
    lead = halo - (width - 1)

    @pl.when(i < n_pt)
    def _():
        def fill(l, cols):
            @pl.when(lax.rem(i, seq_tiles) == 0)
            def _():
                ext_ref[l, 0:halo, :] = jnp.zeros((halo, _LANES), _F32)

            @pl.when(lax.rem(i, seq_tiles) != 0)
            def _():
                ext_ref[l, 0:halo, :] = ext_ref[l, tm:tm + halo, :]

            ext_ref[l, halo:halo + tm, :] = glu_ref[:, cols]

        _depthwise_taps(width, fill, lambda l: l, ext_ref, cw_ref, c_ref, tm // _SUBLANES,
                        lambda b: b * _SUBLANES + lead, lambda b: b * _SUBLANES)

    @pl.when(i >= n_pt)
    def _():
        slab = halo + dseq

        def fill(l, cols):
            def one_sequence(s, carry):
                sext_ref[0, pl.ds(s * slab + lead, width - 1), :] = past_ref[s, :, cols]
                sext_ref[0, pl.ds(pl.multiple_of(s * slab + halo, _SUBLANES), dseq), :] = (
                    glu_ref[pl.ds(pl.multiple_of(s * dseq, _SUBLANES), dseq), cols])
                state_ref[s, :, cols] = sext_ref[0, pl.ds(s * slab + lead + dseq, width - 1), :]
                return carry
            lax.fori_loop(0, tm // dseq, one_sequence, 0)

        _depthwise_taps(width, fill, lambda l: 0, sext_ref, cw_ref, c_ref, tm // dseq,
                        lambda s: s * slab + lead, lambda s: s * dseq)

    _conformer_tail(c_ref, cb_ref, lng_ref, lnb_ref, y_ref)


def _mix_conv31(glu, past, layer, conv_w, conv_b, ln_g, ln_b, n_prompt, seq, dseq):
    m, d = glu.shape
    width = conv_w.shape[0]
    assert width - 1 <= _CONV_HALO and dseq == _SUBLANES
    tm = _TM_CONV
    n_pt = n_prompt // tm
    n_seq = tm // dseq
    slab_rows = n_seq * (_CONV_HALO + dseq)
    row = pl.BlockSpec((1, d), lambda i: (0, 0))
    scratch = [((d // _LANES, tm + _CONV_HALO, _LANES), _F32), ((1, slab_rows, _LANES), _F32), ((tm, d), _F32)]
    windows = [((tm, d), _F32), ((n_seq, _CONV_HALO, d), _F32), ((tm, d), _BF16), ((n_seq, _CONV_HALO, d), _F32)]
    seq_block = lambda i: jnp.maximum(i - n_pt, 0)
    return pl.pallas_call(
        functools.partial(_mix_conv31_kernel, width, n_pt, seq // tm, dseq),
        grid=(m // tm,),
        in_specs=[pl.BlockSpec((tm, d), lambda i: (i, 0)),
                  pl.BlockSpec((None, n_seq, width - 1, d), lambda i: (layer, seq_block(i), 0, 0)),
                  pl.BlockSpec(conv_w.shape, lambda i: (0, 0)), row, row, row],
        out_specs=[pl.BlockSpec((tm, d), lambda i: (i, 0)),
                   pl.BlockSpec((n_seq, width - 1, d), lambda i: (seq_block(i), 0, 0))],
        out_shape=[jax.ShapeDtypeStruct((m, d), _BF16),
                   jax.ShapeDtypeStruct((past.shape[1], width - 1, d), _F32)],
        scratch_shapes=[pltpu.VMEM(s, dt) for s, dt in scratch],
        compiler_params=_params(("arbitrary",), windows, sum(_nbytes(s, dt) for s, dt in scratch)),
        name="mix_conv31",
    )(glu, past, conv_w, conv_b.reshape(1, d), ln_g.reshape(1, d), ln_b.reshape(1, d))


def _outproj_kernel(has_bias, y_ref, w_ref, *refs):
    if has_bias:
        b_ref, x_ref, xo_ref, wbf_ref = refs
    else:
        x_ref, xo_ref, wbf_ref = refs
    (w,) = _resident_weights([w_ref], [wbf_ref])
    acc = _dot(y_ref[...], w)
    if has_bias:
        acc = acc + b_ref[...]
    xo_ref[...] = x_ref[...] + acc


def _outproj(y, layer, w, bias, x):
    m, k = y.shape
    n = w.shape[2]
    tm, tn = _TM, _TN_OUT
    nb = n // tn
    in_specs = [pl.BlockSpec((tm, k), lambda i, j: (i, 0)),
                pl.BlockSpec((None, k, tn), _resident_weight_block(layer, 0, nb))]
    operands = [y, w]
    if bias is not None:
        in_specs.append(pl.BlockSpec((None, 1, tn), lambda i, j: (layer, 0, j)))
        operands.append(bias.reshape(bias.shape[0], 1, n))
    in_specs.append(pl.BlockSpec((tm, tn), lambda i, j: (i, j)))
    operands.append(x)
    windows = [((tm, k), _BF16), ((k, tn), _F32), ((tm, tn), _F32), ((tm, tn), _F32)]
    return pl.pallas_call(
        functools.partial(_outproj_kernel, bias is not None),
        grid=(m // tm, nb),
        in_specs=in_specs,
        out_specs=pl.BlockSpec((tm, tn), lambda i, j: (i, j)),
        out_shape=jax.ShapeDtypeStruct((m, n), _F32),
        scratch_shapes=[pltpu.VMEM((nb, k, tn), _BF16)],
        compiler_params=_params(("arbitrary", "arbitrary"), windows, _nbytes((nb, k, tn), _BF16)),
        name="outproj",
    )(*operands)


def _ffn_kernel(final, x_ref, g_ref, wu_ref, wd_ref, *refs):
    if final:
        gf_ref, o_ref, h_ref = refs
    else:
        o_ref, h_ref = refs
    f = pl.program_id(1)

    @pl.when(f == 0)
    def _():
        def start(rows):
            x = x_ref[rows, :]
            o_ref[rows, :] = x
            h_ref[rows, :] = _rms_rows(x, g_ref[...]).astype(h_ref.dtype)
        _for_row_blocks(o_ref.shape[0], start)

    a = jnp.maximum(_dot(h_ref[...], wu_ref[...].astype(_BF16)), 0.0)
    o_ref[...] += _dot((a * a).astype(_BF16), wd_ref[...].astype(_BF16))

    if final:
        @pl.when(f == pl.num_programs(1) - 1)
        def _():
            def finish(rows):
                o_ref[rows, :] = _rms_rows(o_ref[rows, :], gf_ref[...])
            _for_row_blocks(o_ref.shape[0], finish, rows=_ROWS_IN_PLACE, unroll=1)


def _ffn(x, g, layer, w_up, w_down, row0, rows, g_final=None):
    d = x.shape[1]
    dff = w_up.shape[2]
    tm, tf = _TM, _TF
    t0 = row0 // tm
    in_specs = [pl.BlockSpec((tm, d), lambda i, f: (i + t0, 0)),
                pl.BlockSpec((None, 1, d), lambda i, f: (layer, 0, 0)),
                pl.BlockSpec((None, d, tf), lambda i, f: (layer, 0, f)),
                pl.BlockSpec((None, tf, d), lambda i, f: (layer, f, 0))]
    operands = [x, g.reshape(g.shape[0], 1, d), w_up, w_down]
    if g_final is not None:
        in_specs.append(pl.BlockSpec((1, d), lambda i, f: (0, 0)))
        operands.append(g_final.reshape(1, d))
    windows = [((tm, d), _F32), ((d, tf), _F32), ((tf, d), _F32), ((tm, d), _F32)]
    return pl.pallas_call(
        functools.partial(_ffn_kernel, g_final is not None),
        grid=(rows // tm, dff // tf),
        in_specs=in_specs,
        out_specs=pl.BlockSpec((tm, d), lambda i, f: (i, 0)),
        out_shape=jax.ShapeDtypeStruct((rows, d), _F32),
        scratch_shapes=[pltpu.VMEM((tm, d), _BF16)],
        compiler_params=_params(("arbitrary", "arbitrary"), windows, _nbytes((tm, d), _BF16)),
        name="ffn",
    )(*operands)


def _gating_operands(w_s, b_s, dseq, d):
    groups, chunk, _ = w_s.shape
    t = jnp.arange(chunk)
    causal = t[:, None] >= t[None, :]
    same_seq = (t[:, None] // dseq) == (t[None, :] // dseq)
    reps = chunk // dseq
    ws_prompt = jnp.where(causal, w_s, 0)
    ws_sample = jnp.where(causal & same_seq, jnp.tile(w_s[:, :dseq, :dseq], (1, reps, reps)), 0)
    gd = d // groups
    bias_prompt = jnp.repeat(b_s.T, gd, axis=1)
    bias_sample = jnp.repeat(jnp.tile(b_s[:, :dseq].T, (reps, 1)), gd, axis=1)
    return jnp.stack([ws_prompt, ws_sample]).astype(_BF16), jnp.stack([bias_prompt, bias_sample])


def _prompt_conv_state(t, bsz, seq, keep):
    return jnp.stack([lax.slice_in_dim(t, (b + 1) * seq - keep, (b + 1) * seq, axis=0) for b in range(bsz)])


def _sample_conv_state(past, t, n_prompt, dseq):
    dbsz, keep, d = past.shape
    cur = lax.slice_in_dim(t, n_prompt, n_prompt + dbsz * dseq, axis=0).reshape(dbsz, dseq, d)
    return jnp.concatenate([past, cur], axis=1)[:, -keep:]


def kernel(x_prompt, x_sample, state_b_conv, state_c_conv, norm_mix_g, norm_ffn_g, final_norm_g, a_w_in, a_ln_g, a_ln_b, a_w_s, a_b_s, a_w_out, b_w_in, b_conv_w, b_w_out, c_w_pw1, c_b_pw1, c_conv_w, c_conv_b, c_ln_g, c_ln_b, c_w_pw2, c_b_pw2, ffn_w_up, ffn_w_down):
    bsz, seq, d = x_prompt.shape
    dbsz, dseq, _ = x_sample.shape
    n_p, n_s = bsz * seq, dbsz * dseq
    depth = norm_mix_g.shape[0]
    chunk = a_w_s.shape[-1]
    assert seq % chunk == 0 and chunk % dseq == 0 and seq & (seq - 1) == 0 and dseq & (dseq - 1) == 0
    assert b_conv_w.shape[1] == 3 and seq % _TM_MIX == 0 and n_s % _TM_MIX == 0
    assert n_p % _TM == 0 and n_s % _TM == 0

    x = jnp.concatenate([x_prompt.reshape(n_p, d), x_sample.reshape(n_s, d)], axis=0)
    a_v, b_prompt, b_sample, c_prompt, c_sample = [], [], [], [], []
    for i in range(depth):
        j, kind = divmod(i, 3)
        if kind == 0:
            z = _inproj(_gelu_epilogue, x, norm_mix_g, i, a_w_in, None, j, 0, 1, 2 * d, 768, 1024, _BF16,
                        "inproj_gelu")
            ws_all, bias_all = _gating_operands(a_w_s[j], a_b_s[j], dseq, d)
            y, vn = _mix_gmlp(z, a_ln_g[j], a_ln_b[j], ws_all, bias_all, n_p, n_s)
            a_v.append(vn.reshape(dbsz, dseq, d))
            x = _outproj(y, j, a_w_out, None, x)
        elif kind == 1:
            bg, h = _inproj(_plain_epilogue, x, norm_mix_g, i, b_w_in, None, j, 0, 1, d, 768, 1024, _BF16,
                            "inproj_gate", emit_h=True)
            cx = _inproj(_product_epilogue, h, None, None, b_w_in, None, j, 1, 2, d, 768, 512, _F32,
                         "inproj_product")
            past = state_b_conv[j]
            p2 = jnp.pad(past, ((0, 0), (0, dseq - 2), (0, 0))).reshape(n_s, d)
            p1 = jnp.pad(past[:, 1:], ((0, 0), (0, dseq - 1), (0, 0))).reshape(n_s, d)
            y = _mix_conv3(bg, cx, p1, p2, b_conv_w[j], n_p, seq, dseq)
            b_prompt.append(_prompt_conv_state(cx, bsz, seq, past.shape[1]))
            b_sample.append(_sample_conv_state(past, cx, n_p, dseq))
            x = _outproj(y, j, b_w_out, None, x)
        else:
            glu = _inproj(_glu_epilogue, x, norm_mix_g, i, c_w_pw1, c_b_pw1, j, 0, 2, d, 768, 512, _F32,
                          "inproj_glu")
            y, sample_state = _mix_conv31(glu, state_c_conv, j, c_conv_w[j], c_conv_b[j], c_ln_g[j], c_ln_b[j],
                                          n_p, seq, dseq)
            c_prompt.append(_prompt_conv_state(glu, bsz, seq, state_c_conv.shape[2]))
            c_sample.append(sample_state)
            x = _outproj(y, j, c_w_pw2, c_b_pw2, x)
        if i + 1 < depth:
            x = _ffn(x, norm_ffn_g, i, ffn_w_up, ffn_w_down, 0, n_p + n_s)
        else:
            y_prompt = _ffn(x, norm_ffn_g, i, ffn_w_up, ffn_w_down, 0, n_p, final_norm_g)
            y_sample = _ffn(x, norm_ffn_g, i, ffn_w_up, ffn_w_down, n_p, n_s, final_norm_g)
    return (y_prompt.reshape(bsz, seq, d), y_sample.reshape(dbsz, dseq, d), jnp.stack(a_v),
            jnp.stack(b_prompt), jnp.stack(b_sample), jnp.stack(c_prompt), jnp.stack(c_sample))
```

```python
import functools

import jax
import jax.numpy as jnp
from jax import lax
from jax.experimental import pallas as pl
from jax.experimental.pallas import tpu as pltpu

_F32 = jnp.float32
_BF16 = jnp.bfloat16
_EPS = 1e-6

_LANES = 128
_SUBLANES = 8
_VMEM_CAP_BYTES = 60000 * 1024
_VMEM_SLACK_BYTES = 12 * 1024 * 1024

_TM = 1024
_TM_MIX = 512
_TN_OUT = 1024
_TF = 512
_TM_CONV = 256
_ROWS = 16
_ROWS_IN_PLACE = 64
_ROW_BLOCKS_IN_FLIGHT = 8
_CONV_HALO = 32
_TAP_CHAINS = 2


def _nbytes(shape, dtype):
    n = 1
    for s in shape:
        n *= s
    return n * jnp.dtype(dtype).itemsize


def _params(semantics, windows, scratch=0):
    limit = 2 * sum(_nbytes(s, d) for s, d in windows) + scratch + _VMEM_SLACK_BYTES
    return pltpu.CompilerParams(dimension_semantics=semantics, vmem_limit_bytes=min(limit, _VMEM_CAP_BYTES))


def _rms_rows(x, g):
    return x * lax.rsqrt(jnp.mean(x * x, axis=-1, keepdims=True) + _EPS) * g


def _ln_rows(x, g, b):
    xc = x - jnp.mean(x, axis=-1, keepdims=True)
    return xc * lax.rsqrt(jnp.mean(xc * xc, axis=-1, keepdims=True) + _EPS) * g + b


def _for_row_blocks(n_rows, fn, rows=_ROWS, unroll=_ROW_BLOCKS_IN_FLIGHT):
    def body(r, carry):
        fn(pl.ds(pl.multiple_of(r * rows, rows), rows))
        return carry
    lax.fori_loop(0, n_rows // rows, body, 0, unroll=unroll)


def _dot(a, b):
    return jnp.dot(a, b, preferred_element_type=_F32)


def _gelu_epilogue(accs, out_ref):
    (acc,) = accs
    out_ref[...] = (0.5 * acc * (1.0 + lax.erf(acc * 0.7071067811865476))).astype(out_ref.dtype)


def _plain_epilogue(accs, out_ref):
    (acc,) = accs
    out_ref[...] = acc.astype(out_ref.dtype)


def _product_epilogue(accs, out_ref):
    out_ref[...] = (accs[0] * accs[1]).astype(out_ref.dtype)


def _glu_epilogue(accs, out_ref):
    out_ref[...] = (accs[0] * jax.nn.sigmoid(accs[1])).astype(out_ref.dtype)


def _resident_weight_block(layer, first_block, nb):
    return lambda i, j: (layer, 0, first_block + jnp.where(i == 0, j, nb - 1))


def _resident_weights(w_refs, wbf_refs):
    j = pl.program_id(1)

    @pl.when(pl.program_id(0) == 0)
    def _():
        for w_ref, wbf_ref in zip(w_refs, wbf_refs):
            wbf_ref[j] = w_ref[...].astype(wbf_ref.dtype)
    return [wbf_ref[j] for wbf_ref in wbf_refs]


def _inproj_kernel(epilogue, parts, has_bias, normed, x_ref, *refs):
    if not normed:
        g_ref, refs = refs[0], refs[1:]
    w_refs, refs = refs[:parts], refs[parts:]
    b_refs, refs = (refs[:parts], refs[parts:]) if has_bias else ((), refs)
    out_ref, refs = refs[0], refs[1:]
    if normed:
        h_ref = x_ref
    else:
        h_ref, refs = refs[0], refs[1:]

        @pl.when(pl.program_id(1) == 0)
        def _():
            def norm(rows):
                h_ref[rows, :] = _rms_rows(x_ref[rows, :], g_ref[...]).astype(h_ref.dtype)
            _for_row_blocks(h_ref.shape[0], norm)

    ws = _resident_weights(w_refs, refs)
    h = h_ref[...]
    accs = [_dot(h, w) for w in ws]
    if has_bias:
        accs = [acc + b_ref[...] for acc, b_ref in zip(accs, b_refs)]
    epilogue(accs, out_ref)


def _inproj(epilogue, x, g, g_layer, w, bias, layer, first_part, parts, n, tm, tn, out_dtype, name, emit_h=False):
    m, k = x.shape
    nb = n // tn
    normed = g is None
    in_specs = [pl.BlockSpec((tm, k), lambda i, j: (i, 0))]
    operands = [x]
    if not normed:
        in_specs.append(pl.BlockSpec((None, 1, k), lambda i, j: (g_layer, 0, 0)))
        operands.append(g.reshape(g.shape[0], 1, k))
    for p in range(parts):
        in_specs.append(pl.BlockSpec((None, k, tn), _resident_weight_block(layer, (first_part + p) * nb, nb)))
        operands.append(w)
    if bias is not None:
        for p in range(parts):
            in_specs.append(pl.BlockSpec((None, 1, tn), functools.partial(
                lambda i, j, p: (layer, 0, (first_part + p) * nb + j), p=p)))
            operands.append(bias.reshape(bias.shape[0], 1, -1))
    windows = [((tm, k), x.dtype)] + [((k, tn), _F32)] * parts + [((tm, tn), out_dtype)]
    out_specs = [pl.BlockSpec((tm, tn), lambda i, j: (i, j))]
    out_shape = [jax.ShapeDtypeStruct((m, n), out_dtype)]
    scratch = [((nb, k, tn), _BF16)] * parts
    if emit_h:
        out_specs.append(pl.BlockSpec((tm, k), lambda i, j: (i, 0)))
        out_shape.append(jax.ShapeDtypeStruct((m, k), _BF16))
        windows.append(((tm, k), _BF16))
    elif not normed:
        scratch = [((tm, k), _BF16)] + scratch
    outs = pl.pallas_call(
        functools.partial(_inproj_kernel, epilogue, parts, bias is not None, normed),
        grid=(m // tm, nb),
        in_specs=in_specs,
        out_specs=out_specs,
        out_shape=out_shape,
        scratch_shapes=[pltpu.VMEM(s, dt) for s, dt in scratch],
        compiler_params=_params(("arbitrary", "arbitrary"), windows, sum(_nbytes(s, dt) for s, dt in scratch)),
        name=name,
    )(*operands)
    return outs if emit_h else outs[0]


def _mix_gmlp_kernel(groups, chunk, u_ref, v_ref, lng_ref, lnb_ref, ws_ref, bias_ref, y_ref, vn_ref, vnb_ref):
    tm, d = y_ref.shape
    gd = d // groups

    def normalise(rows):
        vn = _ln_rows(v_ref[rows, :].astype(_F32), lng_ref[...], lnb_ref[...])
        vn_ref[rows, :] = vn
        vnb_ref[rows, :] = vn.astype(vnb_ref.dtype)
    _for_row_blocks(tm, normalise)

    def mix_chunk(c, carry):
        rows = pl.ds(pl.multiple_of(c * chunk, chunk), chunk)
        for g in range(groups):
            cols = slice(g * gd, (g + 1) * gd)
            mixed = _dot(ws_ref[0, g], vnb_ref[rows, cols]) + bias_ref[0, :, cols]
            y_ref[rows, cols] = (u_ref[rows, cols].astype(_F32) * mixed).astype(y_ref.dtype)
        return carry
    lax.fori_loop(0, tm // chunk, mix_chunk, 0)


def _mix_gmlp(z, ln_g, ln_b, ws_all, bias_all, n_prompt, n_sample):
    m, d2 = z.shape
    d = d2 // 2
    tm = _TM_MIX
    n_pt = n_prompt // tm
    groups, chunk = ws_all.shape[1], ws_all.shape[2]
    s_tiles = n_sample // tm
    windows = [((tm, d), _BF16)] * 3 + [((tm, d), _F32), ((1, groups, chunk, chunk), _BF16), ((1, chunk, d), _F32)]
    y, vn = pl.pallas_call(
        functools.partial(_mix_gmlp_kernel, groups, chunk),
        grid=(m // tm,),
        in_specs=[
            pl.BlockSpec((tm, d), lambda i: (i, 0)),
            pl.BlockSpec((tm, d), lambda i: (i, 1)),
            pl.BlockSpec((1, d), lambda i: (0, 0)),
            pl.BlockSpec((1, d), lambda i: (0, 0)),
            pl.BlockSpec((1, groups, chunk, chunk), lambda i: (jnp.where(i < n_pt, 0, 1), 0, 0, 0)),
            pl.BlockSpec((1, chunk, d), lambda i: (jnp.where(i < n_pt, 0, 1), 0, 0)),
        ],
        out_specs=[
            pl.BlockSpec((tm, d), lambda i: (i, 0)),
            pl.BlockSpec((tm, d), lambda i: (jnp.maximum(i - n_pt, 0), 0)),
        ],
        out_shape=[jax.ShapeDtypeStruct((m, d), _BF16), jax.ShapeDtypeStruct((s_tiles * tm, d), _F32)],
        scratch_shapes=[pltpu.VMEM((tm, d), _BF16)],
        compiler_params=_params(("arbitrary",), windows, _nbytes((tm, d), _BF16)),
        name="mix_gmlp",
    )(z, z, ln_g.reshape(1, d), ln_b.reshape(1, d), ws_all, bias_all)
    return y, vn


def _mix_conv3_kernel(n_pt, seq_tiles, dseq, bg_ref, cx_ref, past_ref, cw_ref, y_ref, state_ref, ext_ref):
    i = pl.program_id(0)
    tm, d = y_ref.shape
    halo = _SUBLANES
    seq_start = lax.rem(i, seq_tiles) == 0
    keep = past_ref.shape[1]

    def conv(l, cols, w, sample, b, carry):
        r0 = pl.multiple_of(b * _ROWS, _ROWS)
        rows = pl.ds(r0, _ROWS)
        cur = ext_ref[l, pl.ds(r0 + halo, _ROWS), :]
        prev1 = ext_ref[l, pl.ds(r0 + halo - 1, _ROWS), :]
        prev2 = ext_ref[l, pl.ds(r0 + halo - 2, _ROWS), :]
        if sample:
            pos = lax.broadcasted_iota(jnp.int32, (_ROWS, 1), 0) & (dseq - 1)
            seqs = [b * (_ROWS // dseq) + q for q in range(_ROWS // dseq)]

            def past_row(k):
                return jnp.concatenate(
                    [jnp.broadcast_to(past_ref[s, pl.ds(k, 1), cols], (dseq, _LANES)) for s in seqs], axis=0)
            prev1 = jnp.where(pos >= 1, prev1, past_row(1))
            prev2 = jnp.where(pos >= 2, prev2, jnp.where(pos == 0, past_row(0), past_row(1)))
            for q, s in enumerate(seqs):
                state_ref[s, :, cols] = ext_ref[l, pl.ds(r0 + halo + (q + 1) * dseq - keep, keep), :]
        acc = w[0] * prev2 + w[1] * prev1 + w[2] * cur
        y_ref[rows, cols] = (bg_ref[rows, cols].astype(_F32) * acc).astype(y_ref.dtype)
        return carry

    def lane_block(l, carry):
        cols = pl.ds(pl.multiple_of(l * _LANES, _LANES), _LANES)

        @pl.when(seq_start)
        def _():
            ext_ref[l, 0:halo, :] = jnp.zeros((halo, _LANES), _F32)

        @pl.when(jnp.logical_not(seq_start))
        def _():
            ext_ref[l, 0:halo, :] = ext_ref[l, tm:tm + halo, :]

        ext_ref[l, halo:halo + tm, :] = cx_ref[:, cols]
        w = [jnp.broadcast_to(cw_ref[k:k + 1, cols], (_ROWS, _LANES)) for k in range(3)]

        @pl.when(i < n_pt)
        def _():
            lax.fori_loop(0, tm // _ROWS, functools.partial(conv, l, cols, w, False), 0,
                          unroll=_ROW_BLOCKS_IN_FLIGHT)

        @pl.when(i >= n_pt)
        def _():
            lax.fori_loop(0, tm // _ROWS, functools.partial(conv, l, cols, w, True), 0,
                          unroll=_ROW_BLOCKS_IN_FLIGHT)
        return carry
    lax.fori_loop(0, d // _LANES, lane_block, 0)


def _mix_conv3(bg, cx, past, layer, conv_w, n_prompt, seq, dseq):
    m, d = cx.shape
    keep = past.shape[2]
    assert conv_w.shape[0] == keep + 1 == 3 and keep <= dseq and _ROWS % dseq == 0
    tm = _TM_MIX
    n_pt = n_prompt // tm
    n_seq = tm // dseq
    windows = [((tm, d), _BF16)] * 2 + [((tm, d), _F32)] + [((n_seq, _SUBLANES, d), _F32)] * 2
    seq_block = lambda i: jnp.maximum(i - n_pt, 0)
    return pl.pallas_call(
        functools.partial(_mix_conv3_kernel, n_pt, seq // tm, dseq),
        grid=(m // tm,),
        in_specs=[
            pl.BlockSpec((tm, d), lambda i: (i, 0)),
            pl.BlockSpec((tm, d), lambda i: (i, 0)),
            pl.BlockSpec((None, n_seq, keep, d), lambda i: (layer, seq_block(i), 0, 0)),
            pl.BlockSpec(conv_w.shape, lambda i: (0, 0)),
        ],
        out_specs=[pl.BlockSpec((tm, d), lambda i: (i, 0)),
                   pl.BlockSpec((n_seq, keep, d), lambda i: (seq_block(i), 0, 0))],
        out_shape=[jax.ShapeDtypeStruct((m, d), _BF16), jax.ShapeDtypeStruct((past.shape[1], keep, d), _F32)],
        scratch_shapes=[pltpu.VMEM((d // _LANES, tm + _SUBLANES, _LANES), _F32)],
        compiler_params=_params(("arbitrary",), windows, _nbytes((tm + _SUBLANES, d), _F32)),
        name="mix_conv3",
    )(bg, cx, past, conv_w)


def _depthwise_taps(width, fill, slab_of, ext_ref, cw_ref, c_ref, n_blocks, in_row, out_row):
    d = c_ref.shape[1]

    def lane_block(l, carry):
        cols = pl.ds(pl.multiple_of(l * _LANES, _LANES), _LANES)
        fill(l, cols)
        slab = slab_of(l)
        w = [jnp.broadcast_to(cw_ref[k:k + 1, cols], (_SUBLANES, _LANES)) for k in range(width)]

        def row_block(b, carry2):
            base = in_row(b)
            partial = [None] * _TAP_CHAINS
            for k in range(width):
                term = w[k] * ext_ref[slab, pl.ds(base + k, _SUBLANES), :]
                c = k % _TAP_CHAINS
                partial[c] = term if partial[c] is None else partial[c] + term
            while len(partial) > 1:
                partial = [a + b_ for a, b_ in zip(partial[0::2], partial[1::2])] + (
                    [partial[-1]] if len(partial) % 2 else [])
            c_ref[pl.ds(pl.multiple_of(out_row(b), _SUBLANES), _SUBLANES), cols] = partial[0]
            return carry2
        lax.fori_loop(0, n_blocks, row_block, 0, unroll=8)
        return carry
    lax.fori_loop(0, d // _LANES, lane_block, 0)


def _conformer_tail(c_ref, cb_ref, lng_ref, lnb_ref, y_ref):
    def tail(rows):
        c = _ln_rows(c_ref[rows, :] + cb_ref[...], lng_ref[...], lnb_ref[...])
        y_ref[rows, :] = (c * jax.nn.sigmoid(c)).astype(y_ref.dtype)
    _for_row_blocks(y_ref.shape[0], tail)


def _mix_conv31_kernel(width, n_pt, seq_tiles, dseq, glu_ref, past_ref, cw_ref, cb_ref, lng_ref, lnb_ref, y_ref,
                       state_ref, ext_ref, sext_ref, c_ref):
    i = pl.program_id(0)
    tm, d = y_ref.shape
    halo = _CONV_HALO
    lead = halo - (width - 1)

    @pl.when(i < n_pt)
    def _():
        def fill(l, cols):
            @pl.when(lax.rem(i, seq_tiles) == 0)
            def _():
                ext_ref[l, 0:halo, :] = jnp.zeros((halo, _LANES), _F32)

            @pl.when(lax.rem(i, seq_tiles) != 0)
            def _():
                ext_ref[l, 0:halo, :] = ext_ref[l, tm:tm + halo, :]

            ext_ref[l, halo:halo + tm, :] = glu_ref[:, cols]

        _depthwise_taps(width, fill, lambda l: l, ext_ref, cw_ref, c_ref, tm // _SUBLANES,
                        lambda b: b * _SUBLANES + lead, lambda b: b * _SUBLANES)

    @pl.when(i >= n_pt)
    def _():
        slab = halo + dseq

        def fill(l, cols):
            def one_sequence(s, carry):
                sext_ref[0, pl.ds(s * slab + lead, width - 1), :] = past_ref[s, :, cols]
                sext_ref[0, pl.ds(pl.multiple_of(s * slab + halo, _SUBLANES), dseq), :] = (
                    glu_ref[pl.ds(pl.multiple_of(s * dseq, _SUBLANES), dseq), cols])
                state_ref[s, :, cols] = sext_ref[0, pl.ds(s * slab + lead + dseq, width - 1), :]
                return carry
            lax.fori_loop(0, tm // dseq, one_sequence, 0)

        _depthwise_taps(width, fill, lambda l: 0, sext_ref, cw_ref, c_ref, tm // dseq,
                        lambda s: s * slab + lead, lambda s: s * dseq)

    _conformer_tail(c_ref, cb_ref, lng_ref, lnb_ref, y_ref)


def _mix_conv31(glu, past, layer, conv_w, conv_b, ln_g, ln_b, n_prompt, seq, dseq):
    m, d = glu.shape
    width = conv_w.shape[0]
    assert width - 1 <= _CONV_HALO and dseq == _SUBLANES
    tm = _TM_CONV
    n_pt = n_prompt // tm
    n_seq = tm // dseq
    slab_rows = n_seq * (_CONV_HALO + dseq)
    row = pl.BlockSpec((1, d), lambda i: (0, 0))
    scratch = [((d // _LANES, tm + _CONV_HALO, _LANES), _F32), ((1, slab_rows, _LANES), _F32), ((tm, d), _F32)]
    windows = [((tm, d), _F32), ((n_seq, _CONV_HALO, d), _F32), ((tm, d), _BF16), ((n_seq, _CONV_HALO, d), _F32)]
    seq_block = lambda i: jnp.maximum(i - n_pt, 0)
    return pl.pallas_call(
        functools.partial(_mix_conv31_kernel, width, n_pt, seq // tm, dseq),
        grid=(m // tm,),
        in_specs=[pl.BlockSpec((tm, d), lambda i: (i, 0)),
                  pl.BlockSpec((None, n_seq, width - 1, d), lambda i: (layer, seq_block(i), 0, 0)),
                  pl.BlockSpec(conv_w.shape, lambda i: (0, 0)), row, row, row],
        out_specs=[pl.BlockSpec((tm, d), lambda i: (i, 0)),
                   pl.BlockSpec((n_seq, width - 1, d), lambda i: (seq_block(i), 0, 0))],
        out_shape=[jax.ShapeDtypeStruct((m, d), _BF16),
                   jax.ShapeDtypeStruct((past.shape[1], width - 1, d), _F32)],
        scratch_shapes=[pltpu.VMEM(s, dt) for s, dt in scratch],
        compiler_params=_params(("arbitrary",), windows, sum(_nbytes(s, dt) for s, dt in scratch)),
        name="mix_conv31",
    )(glu, past, conv_w, conv_b.reshape(1, d), ln_g.reshape(1, d), ln_b.reshape(1, d))


def _outproj_kernel(has_bias, y_ref, w_ref, *refs):
    if has_bias:
        b_ref, x_ref, xo_ref, wbf_ref = refs
    else:
        x_ref, xo_ref, wbf_ref = refs
    (w,) = _resident_weights([w_ref], [wbf_ref])
    acc = _dot(y_ref[...], w)
    if has_bias:
        acc = acc + b_ref[...]
    xo_ref[...] = x_ref[...] + acc


def _outproj(y, layer, w, bias, x):
    m, k = y.shape
    n = w.shape[2]
    tm, tn = _TM, _TN_OUT
    nb = n // tn
    in_specs = [pl.BlockSpec((tm, k), lambda i, j: (i, 0)),
                pl.BlockSpec((None, k, tn), _resident_weight_block(layer, 0, nb))]
    operands = [y, w]
    if bias is not None:
        in_specs.append(pl.BlockSpec((None, 1, tn), lambda i, j: (layer, 0, j)))
        operands.append(bias.reshape(bias.shape[0], 1, n))
    in_specs.append(pl.BlockSpec((tm, tn), lambda i, j: (i, j)))
    operands.append(x)
    windows = [((tm, k), _BF16), ((k, tn), _F32), ((tm, tn), _F32), ((tm, tn), _F32)]
    return pl.pallas_call(
        functools.partial(_outproj_kernel, bias is not None),
        grid=(m // tm, nb),
        in_specs=in_specs,
        out_specs=pl.BlockSpec((tm, tn), lambda i, j: (i, j)),
        out_shape=jax.ShapeDtypeStruct((m, n), _F32),
        scratch_shapes=[pltpu.VMEM((nb, k, tn), _BF16)],
        compiler_params=_params(("arbitrary", "arbitrary"), windows, _nbytes((nb, k, tn), _BF16)),
        name="outproj",
    )(*operands)


def _ffn_kernel(final, x_ref, g_ref, wu_ref, wd_ref, *refs):
    if final:
        gf_ref, o_ref, h_ref = refs
    else:
        o_ref, h_ref = refs
    f = pl.program_id(1)

    @pl.when(f == 0)
    def _():
        def start(rows):
            x = x_ref[rows, :]
            o_ref[rows, :] = x
            h_ref[rows, :] = _rms_rows(x, g_ref[...]).astype(h_ref.dtype)
        _for_row_blocks(o_ref.shape[0], start)

    a = jnp.maximum(_dot(h_ref[...], wu_ref[...].astype(_BF16)), 0.0)
    o_ref[...] += _dot((a * a).astype(_BF16), wd_ref[...].astype(_BF16))

    if final:
        @pl.when(f == pl.num_programs(1) - 1)
        def _():
            def finish(rows):
                o_ref[rows, :] = _rms_rows(o_ref[rows, :], gf_ref[...])
            _for_row_blocks(o_ref.shape[0], finish, rows=_ROWS_IN_PLACE, unroll=1)


def _ffn(x, g, layer, w_up, w_down, row0, rows, g_final=None):
    d = x.shape[1]
    dff = w_up.shape[2]
    tm, tf = _TM, _TF
    t0 = row0 // tm
    in_specs = [pl.BlockSpec((tm, d), lambda i, f: (i + t0, 0)),
                pl.BlockSpec((None, 1, d), lambda i, f: (layer, 0, 0)),
                pl.BlockSpec((None, d, tf), lambda i, f: (layer, 0, f)),
                pl.BlockSpec((None, tf, d), lambda i, f: (layer, f, 0))]
    operands = [x, g.reshape(g.shape[0], 1, d), w_up, w_down]
    if g_final is not None:
        in_specs.append(pl.BlockSpec((1, d), lambda i, f: (0, 0)))
        operands.append(g_final.reshape(1, d))
    windows = [((tm, d), _F32), ((d, tf), _F32), ((tf, d), _F32), ((tm, d), _F32)]
    return pl.pallas_call(
        functools.partial(_ffn_kernel, g_final is not None),
        grid=(rows // tm, dff // tf),
        in_specs=in_specs,
        out_specs=pl.BlockSpec((tm, d), lambda i, f: (i, 0)),
        out_shape=jax.ShapeDtypeStruct((rows, d), _F32),
        scratch_shapes=[pltpu.VMEM((tm, d), _BF16)],
        compiler_params=_params(("arbitrary", "arbitrary"), windows, _nbytes((tm, d), _BF16)),
        name="ffn",
    )(*operands)


def _gating_operands(w_s, b_s, dseq, d):
    groups, chunk, _ = w_s.shape
    t = jnp.arange(chunk)
    causal = t[:, None] >= t[None, :]
    same_seq = (t[:, None] // dseq) == (t[None, :] // dseq)
    reps = chunk // dseq
    ws_prompt = jnp.where(causal, w_s, 0)
    ws_sample = jnp.where(causal & same_seq, jnp.tile(w_s[:, :dseq, :dseq], (1, reps, reps)), 0)
    gd = d // groups
    bias_prompt = jnp.repeat(b_s.T, gd, axis=1)
    bias_sample = jnp.repeat(jnp.tile(b_s[:, :dseq].T, (reps, 1)), gd, axis=1)
    return jnp.stack([ws_prompt, ws_sample]).astype(_BF16), jnp.stack([bias_prompt, bias_sample])


def _prompt_conv_state(t, bsz, seq, keep):
    return jnp.stack([lax.slice_in_dim(t, (b + 1) * seq - keep, (b + 1) * seq, axis=0) for b in range(bsz)])


def kernel(x_prompt, x_sample, state_b_conv, state_c_conv, norm_mix_g, norm_ffn_g, final_norm_g, a_w_in, a_ln_g, a_ln_b, a_w_s, a_b_s, a_w_out, b_w_in, b_conv_w, b_w_out, c_w_pw1, c_b_pw1, c_conv_w, c_conv_b, c_ln_g, c_ln_b, c_w_pw2, c_b_pw2, ffn_w_up, ffn_w_down):
    bsz, seq, d = x_prompt.shape
    dbsz, dseq, _ = x_sample.shape
    n_p, n_s = bsz * seq, dbsz * dseq
    depth = norm_mix_g.shape[0]
    chunk = a_w_s.shape[-1]
    assert seq % chunk == 0 and chunk % dseq == 0 and seq & (seq - 1) == 0 and dseq & (dseq - 1) == 0
    assert b_conv_w.shape[1] == 3 and seq % _TM_MIX == 0 and n_s % _TM_MIX == 0
    assert n_p % _TM == 0 and n_s % _TM == 0

    x = jnp.concatenate([x_prompt.reshape(n_p, d), x_sample.reshape(n_s, d)], axis=0)
    a_v, b_prompt, b_sample, c_prompt, c_sample = [], [], [], [], []
    for i in range(depth):
        j, kind = divmod(i, 3)
        if kind == 0:
            z = _inproj(_gelu_epilogue, x, norm_mix_g, i, a_w_in, None, j, 0, 1, 2 * d, 768, 1024, _BF16,
                        "inproj_gelu")
            ws_all, bias_all = _gating_operands(a_w_s[j], a_b_s[j], dseq, d)
            y, vn = _mix_gmlp(z, a_ln_g[j], a_ln_b[j], ws_all, bias_all, n_p, n_s)
            a_v.append(vn.reshape(dbsz, dseq, d))
            x = _outproj(y, j, a_w_out, None, x)
        elif kind == 1:
            bg, h = _inproj(_plain_epilogue, x, norm_mix_g, i, b_w_in, None, j, 0, 1, d, 768, 1024, _BF16,
                            "inproj_gate", emit_h=True)
            cx = _inproj(_product_epilogue, h, None, None, b_w_in, None, j, 1, 2, d, 768, 512, _F32,
                         "inproj_product")
            y, sample_state = _mix_conv3(bg, cx, state_b_conv, j, b_conv_w[j], n_p, seq, dseq)
            b_prompt.append(_prompt_conv_state(cx, bsz, seq, state_b_conv.shape[2]))
            b_sample.append(sample_state)
            x = _outproj(y, j, b_w_out, None, x)
        else:
            glu = _inproj(_glu_epilogue, x, norm_mix_g, i, c_w_pw1, c_b_pw1, j, 0, 2, d, 768, 512, _F32,
                          "inproj_glu")
            y, sample_state = _mix_conv31(glu, state_c_conv, j, c_conv_w[j], c_conv_b[j], c_ln_g[j], c_ln_b[j],
                                          n_p, seq, dseq)
            c_prompt.append(_prompt_conv_state(glu, bsz, seq, state_c_conv.shape[2]))
            c_sample.append(sample_state)
            x = _outproj(y, j, c_w_pw2, c_b_pw2, x)
        if i + 1 < depth:
            x = _ffn(x, norm_ffn_g, i, ffn_w_up, ffn_w_down, 0, n_p + n_s)
        else:
            y_prompt = _ffn(x, norm_ffn_g, i, ffn_w_up, ffn_w_down, 0, n_p, final_norm_g)
            y_sample = _ffn(x, norm_ffn_g, i, ffn_w_up, ffn_w_down, n_p, n_s, final_norm_g)
    return (y_prompt.reshape(bsz, seq, d), y_sample.reshape(dbsz, dseq, d), jnp.stack(a_v),
            jnp.stack(b_prompt), jnp.stack(b_sample), jnp.stack(c_prompt), jnp.stack(c_sample))
```

```python
import functools

import jax
import jax.numpy as jnp
from jax import lax
from jax.experimental import pallas as pl
from jax.experimental.pallas import tpu as pltpu

_F32 = jnp.float32
_BF16 = jnp.bfloat16
_EPS = 1e-6

_LANES = 128
_SUBLANES = 8
_VMEM_CAP_BYTES = 60000 * 1024
_VMEM_SLACK_BYTES = 12 * 1024 * 1024

_TM = 1024
_TM_MIX = 512
_TN_OUT = 1024
_TF = 1024
_FFN_PREFETCH_STEP = 2
_TM_CONV = 256
_ROWS = 16
_ROWS_IN_PLACE = 64
_ROW_BLOCKS_IN_FLIGHT = 8
_CONV_HALO = 32
_TAP_CHAINS = 2


def _nbytes(shape, dtype):
    n = 1
    for s in shape:
        n *= s
    return n * jnp.dtype(dtype).itemsize


def _params(semantics, windows, scratch=0):
    limit = 2 * sum(_nbytes(s, d) for s, d in windows) + scratch + _VMEM_SLACK_BYTES
    return pltpu.CompilerParams(dimension_semantics=semantics, vmem_limit_bytes=min(limit, _VMEM_CAP_BYTES))


def _rms_rows(x, g):
    return x * lax.rsqrt(jnp.mean(x * x, axis=-1, keepdims=True) + _EPS) * g


def _ln_rows(x, g, b):
    xc = x - jnp.mean(x, axis=-1, keepdims=True)
    return xc * lax.rsqrt(jnp.mean(xc * xc, axis=-1, keepdims=True) + _EPS) * g + b


def _for_row_blocks(n_rows, fn, rows=_ROWS, unroll=_ROW_BLOCKS_IN_FLIGHT):
    def body(r, carry):
        fn(pl.ds(pl.multiple_of(r * rows, rows), rows))
        return carry
    lax.fori_loop(0, n_rows // rows, body, 0, unroll=unroll)


def _dot(a, b):
    return jnp.dot(a, b, preferred_element_type=_F32)


def _gelu_epilogue(accs, out_ref):
    (acc,) = accs
    out_ref[...] = (0.5 * acc * (1.0 + lax.erf(acc * 0.7071067811865476))).astype(out_ref.dtype)


def _plain_epilogue(accs, out_ref):
    (acc,) = accs
    out_ref[...] = acc.astype(out_ref.dtype)


def _product_epilogue(accs, out_ref):
    out_ref[...] = (accs[0] * accs[1]).astype(out_ref.dtype)


def _glu_epilogue(accs, out_ref):
    out_ref[...] = (accs[0] * jax.nn.sigmoid(accs[1])).astype(out_ref.dtype)


def _resident_weight_block(layer, first_block, nb):
    return lambda i, j: (layer, 0, first_block + jnp.where(i == 0, j, nb - 1))


def _resident_weights(w_refs, wbf_refs):
    j = pl.program_id(1)

    @pl.when(pl.program_id(0) == 0)
    def _():
        for w_ref, wbf_ref in zip(w_refs, wbf_refs):
            wbf_ref[j] = w_ref[...].astype(wbf_ref.dtype)
    return [wbf_ref[j] for wbf_ref in wbf_refs]


def _inproj_kernel(epilogue, parts, has_bias, normed, x_ref, *refs):
    if not normed:
        g_ref, refs = refs[0], refs[1:]
    w_refs, refs = refs[:parts], refs[parts:]
    b_refs, refs = (refs[:parts], refs[parts:]) if has_bias else ((), refs)
    out_ref, refs = refs[0], refs[1:]
    if normed:
        h_ref = x_ref
    else:
        h_ref, refs = refs[0], refs[1:]

        @pl.when(pl.program_id(1) == 0)
        def _():
            def norm(rows):
                h_ref[rows, :] = _rms_rows(x_ref[rows, :], g_ref[...]).astype(h_ref.dtype)
            _for_row_blocks(h_ref.shape[0], norm)

    ws = _resident_weights(w_refs, refs)
    h = h_ref[...]
    accs = [_dot(h, w) for w in ws]
    if has_bias:
        accs = [acc + b_ref[...] for acc, b_ref in zip(accs, b_refs)]
    epilogue(accs, out_ref)


def _inproj(epilogue, x, g, g_layer, w, bias, layer, first_part, parts, n, tm, tn, out_dtype, name, emit_h=False):
    m, k = x.shape
    nb = n // tn
    normed = g is None
    in_specs = [pl.BlockSpec((tm, k), lambda i, j: (i, 0))]
    operands = [x]
    if not normed:
        in_specs.append(pl.BlockSpec((None, 1, k), lambda i, j: (g_layer, 0, 0)))
        operands.append(g.reshape(g.shape[0], 1, k))
    for p in range(parts):
        in_specs.append(pl.BlockSpec((None, k, tn), _resident_weight_block(layer, (first_part + p) * nb, nb)))
        operands.append(w)
    if bias is not None:
        for p in range(parts):
            in_specs.append(pl.BlockSpec((None, 1, tn), functools.partial(
                lambda i, j, p: (layer, 0, (first_part + p) * nb + j), p=p)))
            operands.append(bias.reshape(bias.shape[0], 1, -1))
    windows = [((tm, k), x.dtype)] + [((k, tn), _F32)] * parts + [((tm, tn), out_dtype)]
    out_specs = [pl.BlockSpec((tm, tn), lambda i, j: (i, j))]
    out_shape = [jax.ShapeDtypeStruct((m, n), out_dtype)]
    scratch = [((nb, k, tn), _BF16)] * parts
    if emit_h:
        out_specs.append(pl.BlockSpec((tm, k), lambda i, j: (i, 0)))
        out_shape.append(jax.ShapeDtypeStruct((m, k), _BF16))
        windows.append(((tm, k), _BF16))
    elif not normed:
        scratch = [((tm, k), _BF16)] + scratch
    outs = pl.pallas_call(
        functools.partial(_inproj_kernel, epilogue, parts, bias is not None, normed),
        grid=(m // tm, nb),
        in_specs=in_specs,
        out_specs=out_specs,
        out_shape=out_shape,
        scratch_shapes=[pltpu.VMEM(s, dt) for s, dt in scratch],
        compiler_params=_params(("arbitrary", "arbitrary"), windows, sum(_nbytes(s, dt) for s, dt in scratch)),
        name=name,
    )(*operands)
    return outs if emit_h else outs[0]


def _mix_gmlp_kernel(groups, chunk, u_ref, v_ref, lng_ref, lnb_ref, ws_ref, bias_ref, y_ref, vn_ref, vnb_ref):
    tm, d = y_ref.shape
    gd = d // groups

    def normalise(rows):
        vn = _ln_rows(v_ref[rows, :].astype(_F32), lng_ref[...], lnb_ref[...])
        vn_ref[rows, :] = vn
        vnb_ref[rows, :] = vn.astype(vnb_ref.dtype)
    _for_row_blocks(tm, normalise)

    def mix_chunk(c, carry):
        rows = pl.ds(pl.multiple_of(c * chunk, chunk), chunk)
        for g in range(groups):
            cols = slice(g * gd, (g + 1) * gd)
            mixed = _dot(ws_ref[0, g], vnb_ref[rows, cols]) + bias_ref[0, :, cols]
            y_ref[rows, cols] = (u_ref[rows, cols].astype(_F32) * mixed).astype(y_ref.dtype)
        return carry
    lax.fori_loop(0, tm // chunk, mix_chunk, 0)


def _mix_gmlp(z, ln_g, ln_b, ws_all, bias_all, n_prompt, n_sample):
    m, d2 = z.shape
    d = d2 // 2
    tm = _TM_MIX
    n_pt = n_prompt // tm
    groups, chunk = ws_all.shape[1], ws_all.shape[2]
    s_tiles = n_sample // tm
    windows = [((tm, d), _BF16)] * 3 + [((tm, d), _F32), ((1, groups, chunk, chunk), _BF16), ((1, chunk, d), _F32)]
    y, vn = pl.pallas_call(
        functools.partial(_mix_gmlp_kernel, groups, chunk),
        grid=(m // tm,),
        in_specs=[
            pl.BlockSpec((tm, d), lambda i: (i, 0)),
            pl.BlockSpec((tm, d), lambda i: (i, 1)),
            pl.BlockSpec((1, d), lambda i: (0, 0)),
            pl.BlockSpec((1, d), lambda i: (0, 0)),
            pl.BlockSpec((1, groups, chunk, chunk), lambda i: (jnp.where(i < n_pt, 0, 1), 0, 0, 0)),
            pl.BlockSpec((1, chunk, d), lambda i: (jnp.where(i < n_pt, 0, 1), 0, 0)),
        ],
        out_specs=[
            pl.BlockSpec((tm, d), lambda i: (i, 0)),
            pl.BlockSpec((tm, d), lambda i: (jnp.maximum(i - n_pt, 0), 0)),
        ],
        out_shape=[jax.ShapeDtypeStruct((m, d), _BF16), jax.ShapeDtypeStruct((s_tiles * tm, d), _F32)],
        scratch_shapes=[pltpu.VMEM((tm, d), _BF16)],
        compiler_params=_params(("arbitrary",), windows, _nbytes((tm, d), _BF16)),
        name="mix_gmlp",
    )(z, z, ln_g.reshape(1, d), ln_b.reshape(1, d), ws_all, bias_all)
    return y, vn


def _mix_conv3_kernel(n_pt, seq_tiles, dseq, bg_ref, cx_ref, past_ref, cw_ref, y_ref, state_ref, ext_ref):
    i = pl.program_id(0)
    tm, d = y_ref.shape
    halo = _SUBLANES
    seq_start = lax.rem(i, seq_tiles) == 0
    keep = past_ref.shape[1]

    def conv(l, cols, w, sample, b, carry):
        r0 = pl.multiple_of(b * _ROWS, _ROWS)
        rows = pl.ds(r0, _ROWS)
        cur = ext_ref[l, pl.ds(r0 + halo, _ROWS), :]
        prev1 = ext_ref[l, pl.ds(r0 + halo - 1, _ROWS), :]
        prev2 = ext_ref[l, pl.ds(r0 + halo - 2, _ROWS), :]
        if sample:
            pos = lax.broadcasted_iota(jnp.int32, (_ROWS, 1), 0) & (dseq - 1)
            seqs = [b * (_ROWS // dseq) + q for q in range(_ROWS // dseq)]

            def past_row(k):
                return jnp.concatenate(
                    [jnp.broadcast_to(past_ref[s, pl.ds(k, 1), cols], (dseq, _LANES)) for s in seqs], axis=0)
            prev1 = jnp.where(pos >= 1, prev1, past_row(1))
            prev2 = jnp.where(pos >= 2, prev2, jnp.where(pos == 0, past_row(0), past_row(1)))
            for q, s in enumerate(seqs):
                state_ref[s, :, cols] = ext_ref[l, pl.ds(r0 + halo + (q + 1) * dseq - keep, keep), :]
        acc = w[0] * prev2 + w[1] * prev1 + w[2] * cur
        y_ref[rows, cols] = (bg_ref[rows, cols].astype(_F32) * acc).astype(y_ref.dtype)
        return carry

    def lane_block(l, carry):
        cols = pl.ds(pl.multiple_of(l * _LANES, _LANES), _LANES)

        @pl.when(seq_start)
        def _():
            ext_ref[l, 0:halo, :] = jnp.zeros((halo, _LANES), _F32)

        @pl.when(jnp.logical_not(seq_start))
        def _():
            ext_ref[l, 0:halo, :] = ext_ref[l, tm:tm + halo, :]

        ext_ref[l, halo:halo + tm, :] = cx_ref[:, cols]
        w = [jnp.broadcast_to(cw_ref[k:k + 1, cols], (_ROWS, _LANES)) for k in range(3)]

        @pl.when(i < n_pt)
        def _():
            lax.fori_loop(0, tm // _ROWS, functools.partial(conv, l, cols, w, False), 0,
                          unroll=_ROW_BLOCKS_IN_FLIGHT)

        @pl.when(i >= n_pt)
        def _():
            lax.fori_loop(0, tm // _ROWS, functools.partial(conv, l, cols, w, True), 0,
                          unroll=_ROW_BLOCKS_IN_FLIGHT)
        return carry
    lax.fori_loop(0, d // _LANES, lane_block, 0)


def _mix_conv3(bg, cx, past, layer, conv_w, n_prompt, seq, dseq):
    m, d = cx.shape
    keep = past.shape[2]
    assert conv_w.shape[0] == keep + 1 == 3 and keep <= dseq and _ROWS % dseq == 0
    tm = _TM_MIX
    n_pt = n_prompt // tm
    n_seq = tm // dseq
    windows = [((tm, d), _BF16)] * 2 + [((tm, d), _F32)] + [((n_seq, _SUBLANES, d), _F32)] * 2
    seq_block = lambda i: jnp.maximum(i - n_pt, 0)
    return pl.pallas_call(
        functools.partial(_mix_conv3_kernel, n_pt, seq // tm, dseq),
        grid=(m // tm,),
        in_specs=[
            pl.BlockSpec((tm, d), lambda i: (i, 0)),
            pl.BlockSpec((tm, d), lambda i: (i, 0)),
            pl.BlockSpec((None, n_seq, keep, d), lambda i: (layer, seq_block(i), 0, 0)),
            pl.BlockSpec(conv_w.shape, lambda i: (0, 0)),
        ],
        out_specs=[pl.BlockSpec((tm, d), lambda i: (i, 0)),
                   pl.BlockSpec((n_seq, keep, d), lambda i: (seq_block(i), 0, 0))],
        out_shape=[jax.ShapeDtypeStruct((m, d), _BF16), jax.ShapeDtypeStruct((past.shape[1], keep, d), _F32)],
        scratch_shapes=[pltpu.VMEM((d // _LANES, tm + _SUBLANES, _LANES), _F32)],
        compiler_params=_params(("arbitrary",), windows, _nbytes((tm + _SUBLANES, d), _F32)),
        name="mix_conv3",
    )(bg, cx, past, conv_w)


def _depthwise_taps(width, fill, slab_of, ext_ref, cw_ref, c_ref, n_blocks, in_row, out_row):
    d = c_ref.shape[1]

    def lane_block(l, carry):
        cols = pl.ds(pl.multiple_of(l * _LANES, _LANES), _LANES)
        fill(l, cols)
        slab = slab_of(l)
        w = [jnp.broadcast_to(cw_ref[k:k + 1, cols], (_SUBLANES, _LANES)) for k in range(width)]

        def row_block(b, carry2):
            base = in_row(b)
            partial = [None] * _TAP_CHAINS
            for k in range(width):
                term = w[k] * ext_ref[slab, pl.ds(base + k, _SUBLANES), :]
                c = k % _TAP_CHAINS
                partial[c] = term if partial[c] is None else partial[c] + term
            while len(partial) > 1:
                partial = [a + b_ for a, b_ in zip(partial[0::2], partial[1::2])] + (
                    [partial[-1]] if len(partial) % 2 else [])
            c_ref[pl.ds(pl.multiple_of(out_row(b), _SUBLANES), _SUBLANES), cols] = partial[0]
            return carry2
        lax.fori_loop(0, n_blocks, row_block, 0, unroll=8)
        return carry
    lax.fori_loop(0, d // _LANES, lane_block, 0)


def _conformer_tail(c_ref, cb_ref, lng_ref, lnb_ref, y_ref):
    def tail(rows):
        c = _ln_rows(c_ref[rows, :] + cb_ref[...], lng_ref[...], lnb_ref[...])
        y_ref[rows, :] = (c * jax.nn.sigmoid(c)).astype(y_ref.dtype)
    _for_row_blocks(y_ref.shape[0], tail)


def _mix_conv31_kernel(width, n_pt, seq_tiles, dseq, glu_ref, past_ref, cw_ref, cb_ref, lng_ref, lnb_ref, y_ref,
                       state_ref, ext_ref, sext_ref, c_ref):
    i = pl.program_id(0)
    tm, d = y_ref.shape
    halo = _CONV_HALO
    lead = halo - (width - 1)

    @pl.when(i < n_pt)
    def _():
        def fill(l, cols):
            @pl.when(lax.rem(i, seq_tiles) == 0)
            def _():
                ext_ref[l, 0:halo, :] = jnp.zeros((halo, _LANES), _F32)

            @pl.when(lax.rem(i, seq_tiles) != 0)
            def _():
                ext_ref[l, 0:halo, :] = ext_ref[l, tm:tm + halo, :]

            ext_ref[l, halo:halo + tm, :] = glu_ref[:, cols]

        _depthwise_taps(width, fill, lambda l: l, ext_ref, cw_ref, c_ref, tm // _SUBLANES,
                        lambda b: b * _SUBLANES + lead, lambda b: b * _SUBLANES)

    @pl.when(i >= n_pt)
    def _():
        slab = halo + dseq

        def fill(l, cols):
            def one_sequence(s, carry):
                sext_ref[0, pl.ds(s * slab + lead, width - 1), :] = past_ref[s, :, cols]
                sext_ref[0, pl.ds(pl.multiple_of(s * slab + halo, _SUBLANES), dseq), :] = (
                    glu_ref[pl.ds(pl.multiple_of(s * dseq, _SUBLANES), dseq), cols])
                state_ref[s, :, cols] = sext_ref[0, pl.ds(s * slab + lead + dseq, width - 1), :]
                return carry
            lax.fori_loop(0, tm // dseq, one_sequence, 0)

        _depthwise_taps(width, fill, lambda l: 0, sext_ref, cw_ref, c_ref, tm // dseq,
                        lambda s: s * slab + lead, lambda s: s * dseq)

    _conformer_tail(c_ref, cb_ref, lng_ref, lnb_ref, y_ref)


def _mix_conv31(glu, past, layer, conv_w, conv_b, ln_g, ln_b, n_prompt, seq, dseq):
    m, d = glu.shape
    width = conv_w.shape[0]
    assert width - 1 <= _CONV_HALO and dseq == _SUBLANES
    tm = _TM_CONV
    n_pt = n_prompt // tm
    n_seq = tm // dseq
    slab_rows = n_seq * (_CONV_HALO + dseq)
    row = pl.BlockSpec((1, d), lambda i: (0, 0))
    scratch = [((d // _LANES, tm + _CONV_HALO, _LANES), _F32), ((1, slab_rows, _LANES), _F32), ((tm, d), _F32)]
    windows = [((tm, d), _F32), ((n_seq, _CONV_HALO, d), _F32), ((tm, d), _BF16), ((n_seq, _CONV_HALO, d), _F32)]
    seq_block = lambda i: jnp.maximum(i - n_pt, 0)
    return pl.pallas_call(
        functools.partial(_mix_conv31_kernel, width, n_pt, seq // tm, dseq),
        grid=(m // tm,),
        in_specs=[pl.BlockSpec((tm, d), lambda i: (i, 0)),
                  pl.BlockSpec((None, n_seq, width - 1, d), lambda i: (layer, seq_block(i), 0, 0)),
                  pl.BlockSpec(conv_w.shape, lambda i: (0, 0)), row, row, row],
        out_specs=[pl.BlockSpec((tm, d), lambda i: (i, 0)),
                   pl.BlockSpec((n_seq, width - 1, d), lambda i: (seq_block(i), 0, 0))],
        out_shape=[jax.ShapeDtypeStruct((m, d), _BF16),
                   jax.ShapeDtypeStruct((past.shape[1], width - 1, d), _F32)],
        scratch_shapes=[pltpu.VMEM(s, dt) for s, dt in scratch],
        compiler_params=_params(("arbitrary",), windows, sum(_nbytes(s, dt) for s, dt in scratch)),
        name="mix_conv31",
    )(glu, past, conv_w, conv_b.reshape(1, d), ln_g.reshape(1, d), ln_b.reshape(1, d))


def _outproj_kernel(has_bias, y_ref, w_ref, *refs):
    if has_bias:
        b_ref, x_ref, xo_ref, wbf_ref = refs
    else:
        x_ref, xo_ref, wbf_ref = refs
    (w,) = _resident_weights([w_ref], [wbf_ref])
    acc = _dot(y_ref[...], w)
    if has_bias:
        acc = acc + b_ref[...]
    xo_ref[...] = x_ref[...] + acc


def _outproj(y, layer, w, bias, x):
    m, k = y.shape
    n = w.shape[2]
    tm, tn = _TM, _TN_OUT
    nb = n // tn
    in_specs = [pl.BlockSpec((tm, k), lambda i, j: (i, 0)),
                pl.BlockSpec((None, k, tn), _resident_weight_block(layer, 0, nb))]
    operands = [y, w]
    if bias is not None:
        in_specs.append(pl.BlockSpec((None, 1, tn), lambda i, j: (layer, 0, j)))
        operands.append(bias.reshape(bias.shape[0], 1, n))
    in_specs.append(pl.BlockSpec((tm, tn), lambda i, j: (i, j)))
    operands.append(x)
    windows = [((tm, k), _BF16), ((k, tn), _F32), ((tm, tn), _F32), ((tm, tn), _F32)]
    return pl.pallas_call(
        functools.partial(_outproj_kernel, bias is not None),
        grid=(m // tm, nb),
        in_specs=in_specs,
        out_specs=pl.BlockSpec((tm, tn), lambda i, j: (i, j)),
        out_shape=jax.ShapeDtypeStruct((m, n), _F32),
        scratch_shapes=[pltpu.VMEM((nb, k, tn), _BF16)],
        compiler_params=_params(("arbitrary", "arbitrary"), windows, _nbytes((nb, k, tn), _BF16)),
        name="outproj",
    )(*operands)


def _ffn_kernel(final, first_tile, x_hbm, g_ref, wu_ref, wd_ref, *refs):
    if final:
        gf_ref, o_hbm, acc_ref, h_ref, in_sem, out_sem = refs
    else:
        o_hbm, acc_ref, h_ref, in_sem, out_sem = refs
    i, f = pl.program_id(0), pl.program_id(1)
    n_tiles, n_steps = pl.num_programs(0), pl.num_programs(1)
    tm = acc_ref.shape[1]
    slot = lax.rem(i, 2)

    def x_copy(tile, s):
        return pltpu.make_async_copy(x_hbm.at[pl.ds((tile + first_tile) * tm, tm)], acc_ref.at[s], in_sem.at[s])

    def o_copy(tile, s):
        return pltpu.make_async_copy(acc_ref.at[s], o_hbm.at[pl.ds(tile * tm, tm)], out_sem.at[s])

    @pl.when(f == 0)
    def _():
        @pl.when(i == 0)
        def _():
            x_copy(0, 0).start()
        x_copy(i, slot).wait()

        def norm(rows):
            h_ref[rows, :] = _rms_rows(acc_ref[slot, rows, :], g_ref[...]).astype(h_ref.dtype)
        _for_row_blocks(tm, norm)

    @pl.when(jnp.logical_and(f == _FFN_PREFETCH_STEP, i + 1 < n_tiles))
    def _():
        @pl.when(i >= 1)
        def _():
            o_copy(i - 1, 1 - slot).wait()
        x_copy(i + 1, 1 - slot).start()

    a = jnp.maximum(_dot(h_ref[...], wu_ref[...].astype(_BF16)), 0.0)
    acc_ref[slot] += _dot((a * a).astype(_BF16), wd_ref[...].astype(_BF16))

    @pl.when(f == n_steps - 1)
    def _():
        if final:
            def finish(rows):
                acc_ref[slot, rows, :] = _rms_rows(acc_ref[slot, rows, :], gf_ref[...])
            _for_row_blocks(tm, finish, rows=_ROWS_IN_PLACE, unroll=1)
        o_copy(i, slot).start()

        @pl.when(i == n_tiles - 1)
        def _():
            o_copy(i, slot).wait()

            @pl.when(i >= 1)
            def _():
                o_copy(i - 1, 1 - slot).wait()


def _ffn(x, g, layer, w_up, w_down, row0, rows, g_final=None):
    d = x.shape[1]
    dff = w_up.shape[2]
    tm, tf = _TM, _TF
    assert dff // tf > _FFN_PREFETCH_STEP >= 1 and row0 % tm == 0 and rows % tm == 0
    in_specs = [pl.BlockSpec(memory_space=pl.ANY),
                pl.BlockSpec((None, 1, d), lambda i, f: (layer, 0, 0)),
                pl.BlockSpec((None, d, tf), lambda i, f: (layer, 0, f)),
                pl.BlockSpec((None, tf, d), lambda i, f: (layer, f, 0))]
    operands = [x, g.reshape(g.shape[0], 1, d), w_up, w_down]
    if g_final is not None:
        in_specs.append(pl.BlockSpec((1, d), lambda i, f: (0, 0)))
        operands.append(g_final.reshape(1, d))
    windows = [((d, tf), _F32), ((tf, d), _F32)]
    scratch = [((2, tm, d), _F32), ((tm, d), _BF16)]
    return pl.pallas_call(
        functools.partial(_ffn_kernel, g_final is not None, row0 // tm),
        grid=(rows // tm, dff // tf),
        in_specs=in_specs,
        out_specs=pl.BlockSpec(memory_space=pl.ANY),
        out_shape=jax.ShapeDtypeStruct((rows, d), _F32),
        scratch_shapes=[pltpu.VMEM(s, dt) for s, dt in scratch]
                       + [pltpu.SemaphoreType.DMA((2,)), pltpu.SemaphoreType.DMA((2,))],
        compiler_params=_params(("arbitrary", "arbitrary"), windows, sum(_nbytes(s, dt) for s, dt in scratch)),
        name="ffn",
    )(*operands)


def _gating_operands(w_s, b_s, dseq, d):
    groups, chunk, _ = w_s.shape
    t = jnp.arange(chunk)
    causal = t[:, None] >= t[None, :]
    same_seq = (t[:, None] // dseq) == (t[None, :] // dseq)
    reps = chunk // dseq
    ws_prompt = jnp.where(causal, w_s, 0)
    ws_sample = jnp.where(causal & same_seq, jnp.tile(w_s[:, :dseq, :dseq], (1, reps, reps)), 0)
    gd = d // groups
    bias_prompt = jnp.repeat(b_s.T, gd, axis=1)
    bias_sample = jnp.repeat(jnp.tile(b_s[:, :dseq].T, (reps, 1)), gd, axis=1)
    return jnp.stack([ws_prompt, ws_sample]).astype(_BF16), jnp.stack([bias_prompt, bias_sample])


def _prompt_conv_state(t, bsz, seq, keep):
    return jnp.stack([lax.slice_in_dim(t, (b + 1) * seq - keep, (b + 1) * seq, axis=0) for b in range(bsz)])


def kernel(x_prompt, x_sample, state_b_conv, state_c_conv, norm_mix_g, norm_ffn_g, final_norm_g, a_w_in, a_ln_g, a_ln_b, a_w_s, a_b_s, a_w_out, b_w_in, b_conv_w, b_w_out, c_w_pw1, c_b_pw1, c_conv_w, c_conv_b, c_ln_g, c_ln_b, c_w_pw2, c_b_pw2, ffn_w_up, ffn_w_down):
    bsz, seq, d = x_prompt.shape
    dbsz, dseq, _ = x_sample.shape
    n_p, n_s = bsz * seq, dbsz * dseq
    depth = norm_mix_g.shape[0]
    chunk = a_w_s.shape[-1]
    assert seq % chunk == 0 and chunk % dseq == 0 and seq & (seq - 1) == 0 and dseq & (dseq - 1) == 0
    assert b_conv_w.shape[1] == 3 and seq % _TM_MIX == 0 and n_s % _TM_MIX == 0
    assert n_p % _TM == 0 and n_s % _TM == 0

    x = jnp.concatenate([x_prompt.reshape(n_p, d), x_sample.reshape(n_s, d)], axis=0)
    a_v, b_prompt, b_sample, c_prompt, c_sample = [], [], [], [], []
    for i in range(depth):
        j, kind = divmod(i, 3)
        if kind == 0:
            z = _inproj(_gelu_epilogue, x, norm_mix_g, i, a_w_in, None, j, 0, 1, 2 * d, 768, 1024, _BF16,
                        "inproj_gelu")
            ws_all, bias_all = _gating_operands(a_w_s[j], a_b_s[j], dseq, d)
            y, vn = _mix_gmlp(z, a_ln_g[j], a_ln_b[j], ws_all, bias_all, n_p, n_s)
            a_v.append(vn.reshape(dbsz, dseq, d))
            x = _outproj(y, j, a_w_out, None, x)
        elif kind == 1:
            bg, h = _inproj(_plain_epilogue, x, norm_mix_g, i, b_w_in, None, j, 0, 1, d, 768, 1024, _BF16,
                            "inproj_gate", emit_h=True)
            cx = _inproj(_product_epilogue, h, None, None, b_w_in, None, j, 1, 2, d, 768, 512, _F32,
                         "inproj_product")
            y, sample_state = _mix_conv3(bg, cx, state_b_conv, j, b_conv_w[j], n_p, seq, dseq)
            b_prompt.append(_prompt_conv_state(cx, bsz, seq, state_b_conv.shape[2]))
            b_sample.append(sample_state)
            x = _outproj(y, j, b_w_out, None, x)
        else:
            glu = _inproj(_glu_epilogue, x, norm_mix_g, i, c_w_pw1, c_b_pw1, j, 0, 2, d, 768, 512, _F32,
                          "inproj_glu")
            y, sample_state = _mix_conv31(glu, state_c_conv, j, c_conv_w[j], c_conv_b[j], c_ln_g[j], c_ln_b[j],
                                          n_p, seq, dseq)
            c_prompt.append(_prompt_conv_state(glu, bsz, seq, state_c_conv.shape[2]))
            c_sample.append(sample_state)
            x = _outproj(y, j, c_w_pw2, c_b_pw2, x)
        if i + 1 < depth:
            x = _ffn(x, norm_ffn_g, i, ffn_w_up, ffn_w_down, 0, n_p + n_s)
        else:
            y_prompt = _ffn(x, norm_ffn_g, i, ffn_w_up, ffn_w_down, 0, n_p, final_norm_g)
            y_sample = _ffn(x, norm_ffn_g, i, ffn_w_up, ffn_w_down, n_p, n_s, final_norm_g)
    return (y_prompt.reshape(bsz, seq, d), y_sample.reshape(dbsz, dseq, d), jnp.stack(a_v),
            jnp.stack(b_prompt), jnp.stack(b_sample), jnp.stack(c_prompt), jnp.stack(c_sample))
```

```python
import functools

import jax
import jax.numpy as jnp
from jax import lax
from jax.experimental import pallas as pl
from jax.experimental.pallas import tpu as pltpu

_F32 = jnp.float32
_BF16 = jnp.bfloat16
_EPS = 1e-6

_LANES = 128
_SUBLANES = 8
_VMEM_CAP_BYTES = 60000 * 1024
_VMEM_SLACK_BYTES = 12 * 1024 * 1024

_TM = 1024
_TM_MIX = 512
_TN_OUT = 1024
_TF = 1024
_FFN_PREFETCH_STEP = 2
_TM_CONV = 256
_ROWS = 16
_ROWS_IN_PLACE = 64
_ROW_BLOCKS_IN_FLIGHT = 8
_CONV_HALO = 32
_TAP_CHAINS = 2
_TAP_BLOCKS_IN_FLIGHT = 16


def _nbytes(shape, dtype):
    n = 1
    for s in shape:
        n *= s
    return n * jnp.dtype(dtype).itemsize


def _params(semantics, windows, scratch=0):
    limit = 2 * sum(_nbytes(s, d) for s, d in windows) + scratch + _VMEM_SLACK_BYTES
    return pltpu.CompilerParams(dimension_semantics=semantics, vmem_limit_bytes=min(limit, _VMEM_CAP_BYTES))


def _rms_rows(x, g):
    return x * lax.rsqrt(jnp.mean(x * x, axis=-1, keepdims=True) + _EPS) * g


def _ln_rows(x, g, b):
    xc = x - jnp.mean(x, axis=-1, keepdims=True)
    return xc * lax.rsqrt(jnp.mean(xc * xc, axis=-1, keepdims=True) + _EPS) * g + b


def _for_row_blocks(n_rows, fn, rows=_ROWS, unroll=_ROW_BLOCKS_IN_FLIGHT):
    def body(r, carry):
        fn(pl.ds(pl.multiple_of(r * rows, rows), rows))
        return carry
    lax.fori_loop(0, n_rows // rows, body, 0, unroll=unroll)


def _dot(a, b):
    return jnp.dot(a, b, preferred_element_type=_F32)


def _gelu_epilogue(accs, out_ref):
    (acc,) = accs
    out_ref[...] = (0.5 * acc * (1.0 + lax.erf(acc * 0.7071067811865476))).astype(out_ref.dtype)


def _plain_epilogue(accs, out_ref):
    (acc,) = accs
    out_ref[...] = acc.astype(out_ref.dtype)


def _product_epilogue(accs, out_ref):
    out_ref[...] = (accs[0] * accs[1]).astype(out_ref.dtype)


def _glu_epilogue(accs, out_ref):
    out_ref[...] = (accs[0] * jax.nn.sigmoid(accs[1])).astype(out_ref.dtype)


def _resident_weight_block(layer, first_block, nb):
    return lambda i, j: (layer, 0, first_block + jnp.where(i == 0, j, nb - 1))


def _resident_weights(w_refs, wbf_refs):
    j = pl.program_id(1)

    @pl.when(pl.program_id(0) == 0)
    def _():
        for w_ref, wbf_ref in zip(w_refs, wbf_refs):
            wbf_ref[j] = w_ref[...].astype(wbf_ref.dtype)
    return [wbf_ref[j] for wbf_ref in wbf_refs]


def _inproj_kernel(epilogue, parts, has_bias, normed, x_ref, *refs):
    if not normed:
        g_ref, refs = refs[0], refs[1:]
    w_refs, refs = refs[:parts], refs[parts:]
    b_refs, refs = (refs[:parts], refs[parts:]) if has_bias else ((), refs)
    out_ref, refs = refs[0], refs[1:]
    if normed:
        h_ref = x_ref
    else:
        h_ref, refs = refs[0], refs[1:]

        @pl.when(pl.program_id(1) == 0)
        def _():
            def norm(rows):
                h_ref[rows, :] = _rms_rows(x_ref[rows, :], g_ref[...]).astype(h_ref.dtype)
            _for_row_blocks(h_ref.shape[0], norm)

    ws = _resident_weights(w_refs, refs)
    h = h_ref[...]
    accs = [_dot(h, w) for w in ws]
    if has_bias:
        accs = [acc + b_ref[...] for acc, b_ref in zip(accs, b_refs)]
    epilogue(accs, out_ref)


def _inproj(epilogue, x, g, g_layer, w, bias, layer, first_part, parts, n, tm, tn, out_dtype, name, emit_h=False):
    m, k = x.shape
    nb = n // tn
    normed = g is None
    in_specs = [pl.BlockSpec((tm, k), lambda i, j: (i, 0))]
    operands = [x]
    if not normed:
        in_specs.append(pl.BlockSpec((None, 1, k), lambda i, j: (g_layer, 0, 0)))
        operands.append(g.reshape(g.shape[0], 1, k))
    for p in range(parts):
        in_specs.append(pl.BlockSpec((None, k, tn), _resident_weight_block(layer, (first_part + p) * nb, nb)))
        operands.append(w)
    if bias is not None:
        for p in range(parts):
            in_specs.append(pl.BlockSpec((None, 1, tn), functools.partial(
                lambda i, j, p: (layer, 0, (first_part + p) * nb + j), p=p)))
            operands.append(bias.reshape(bias.shape[0], 1, -1))
    windows = [((tm, k), x.dtype)] + [((k, tn), _F32)] * parts + [((tm, tn), out_dtype)]
    out_specs = [pl.BlockSpec((tm, tn), lambda i, j: (i, j))]
    out_shape = [jax.ShapeDtypeStruct((m, n), out_dtype)]
    scratch = [((nb, k, tn), _BF16)] * parts
    if emit_h:
        out_specs.append(pl.BlockSpec((tm, k), lambda i, j: (i, 0)))
        out_shape.append(jax.ShapeDtypeStruct((m, k), _BF16))
        windows.append(((tm, k), _BF16))
    elif not normed:
        scratch = [((tm, k), _BF16)] + scratch
    outs = pl.pallas_call(
        functools.partial(_inproj_kernel, epilogue, parts, bias is not None, normed),
        grid=(m // tm, nb),
        in_specs=in_specs,
        out_specs=out_specs,
        out_shape=out_shape,
        scratch_shapes=[pltpu.VMEM(s, dt) for s, dt in scratch],
        compiler_params=_params(("arbitrary", "arbitrary"), windows, sum(_nbytes(s, dt) for s, dt in scratch)),
        name=name,
    )(*operands)
    return outs if emit_h else outs[0]


def _mix_gmlp_kernel(groups, chunk, u_ref, v_ref, lng_ref, lnb_ref, ws_ref, bias_ref, y_ref, vn_ref, vnb_ref):
    tm, d = y_ref.shape
    gd = d // groups

    def normalise(rows):
        vn = _ln_rows(v_ref[rows, :].astype(_F32), lng_ref[...], lnb_ref[...])
        vn_ref[rows, :] = vn
        vnb_ref[rows, :] = vn.astype(vnb_ref.dtype)
    _for_row_blocks(tm, normalise)

    def mix_chunk(c, carry):
        rows = pl.ds(pl.multiple_of(c * chunk, chunk), chunk)
        for g in range(groups):
            cols = slice(g * gd, (g + 1) * gd)
            mixed = _dot(ws_ref[0, g], vnb_ref[rows, cols]) + bias_ref[0, :, cols]
            y_ref[rows, cols] = (u_ref[rows, cols].astype(_F32) * mixed).astype(y_ref.dtype)
        return carry
    lax.fori_loop(0, tm // chunk, mix_chunk, 0)


def _mix_gmlp(z, ln_g, ln_b, ws_all, bias_all, n_prompt, n_sample):
    m, d2 = z.shape
    d = d2 // 2
    tm = _TM_MIX
    n_pt = n_prompt // tm
    groups, chunk = ws_all.shape[1], ws_all.shape[2]
    s_tiles = n_sample // tm
    windows = [((tm, d), _BF16)] * 3 + [((tm, d), _F32), ((1, groups, chunk, chunk), _BF16), ((1, chunk, d), _F32)]
    y, vn = pl.pallas_call(
        functools.partial(_mix_gmlp_kernel, groups, chunk),
        grid=(m // tm,),
        in_specs=[
            pl.BlockSpec((tm, d), lambda i: (i, 0)),
            pl.BlockSpec((tm, d), lambda i: (i, 1)),
            pl.BlockSpec((1, d), lambda i: (0, 0)),
            pl.BlockSpec((1, d), lambda i: (0, 0)),
            pl.BlockSpec((1, groups, chunk, chunk), lambda i: (jnp.where(i < n_pt, 0, 1), 0, 0, 0)),
            pl.BlockSpec((1, chunk, d), lambda i: (jnp.where(i < n_pt, 0, 1), 0, 0)),
        ],
        out_specs=[
            pl.BlockSpec((tm, d), lambda i: (i, 0)),
            pl.BlockSpec((tm, d), lambda i: (jnp.maximum(i - n_pt, 0), 0)),
        ],
        out_shape=[jax.ShapeDtypeStruct((m, d), _BF16), jax.ShapeDtypeStruct((s_tiles * tm, d), _F32)],
        scratch_shapes=[pltpu.VMEM((tm, d), _BF16)],
        compiler_params=_params(("arbitrary",), windows, _nbytes((tm, d), _BF16)),
        name="mix_gmlp",
    )(z, z, ln_g.reshape(1, d), ln_b.reshape(1, d), ws_all, bias_all)
    return y, vn


def _mix_conv3_kernel(n_pt, seq_tiles, dseq, bg_ref, cx_ref, past_ref, cw_ref, y_ref, state_ref, ext_ref):
    i = pl.program_id(0)
    tm, d = y_ref.shape
    halo = _SUBLANES
    seq_start = lax.rem(i, seq_tiles) == 0
    keep = past_ref.shape[1]

    def conv(l, cols, w, sample, b, carry):
        r0 = pl.multiple_of(b * _ROWS, _ROWS)
        rows = pl.ds(r0, _ROWS)
        cur = ext_ref[l, pl.ds(r0 + halo, _ROWS), :]
        prev1 = ext_ref[l, pl.ds(r0 + halo - 1, _ROWS), :]
        prev2 = ext_ref[l, pl.ds(r0 + halo - 2, _ROWS), :]
        if sample:
            pos = lax.broadcasted_iota(jnp.int32, (_ROWS, 1), 0) & (dseq - 1)
            seqs = [b * (_ROWS // dseq) + q for q in range(_ROWS // dseq)]

            def past_row(k):
                return jnp.concatenate(
                    [jnp.broadcast_to(past_ref[s, pl.ds(k, 1), cols], (dseq, _LANES)) for s in seqs], axis=0)
            prev1 = jnp.where(pos >= 1, prev1, past_row(1))
            prev2 = jnp.where(pos >= 2, prev2, jnp.where(pos == 0, past_row(0), past_row(1)))
            for q, s in enumerate(seqs):
                state_ref[s, :, cols] = ext_ref[l, pl.ds(r0 + halo + (q + 1) * dseq - keep, keep), :]
        acc = w[0] * prev2 + w[1] * prev1 + w[2] * cur
        y_ref[rows, cols] = (bg_ref[rows, cols].astype(_F32) * acc).astype(y_ref.dtype)
        return carry

    def lane_block(l, carry):
        cols = pl.ds(pl.multiple_of(l * _LANES, _LANES), _LANES)

        @pl.when(seq_start)
        def _():
            ext_ref[l, 0:halo, :] = jnp.zeros((halo, _LANES), _F32)

        @pl.when(jnp.logical_not(seq_start))
        def _():
            ext_ref[l, 0:halo, :] = ext_ref[l, tm:tm + halo, :]

        ext_ref[l, halo:halo + tm, :] = cx_ref[:, cols]
        w = [jnp.broadcast_to(cw_ref[k:k + 1, cols], (_ROWS, _LANES)) for k in range(3)]

        @pl.when(i < n_pt)
        def _():
            lax.fori_loop(0, tm // _ROWS, functools.partial(conv, l, cols, w, False), 0,
                          unroll=_ROW_BLOCKS_IN_FLIGHT)

        @pl.when(i >= n_pt)
        def _():
            lax.fori_loop(0, tm // _ROWS, functools.partial(conv, l, cols, w, True), 0,
                          unroll=_ROW_BLOCKS_IN_FLIGHT)
        return carry
    lax.fori_loop(0, d // _LANES, lane_block, 0)


def _mix_conv3(bg, cx, past, layer, conv_w, n_prompt, seq, dseq):
    m, d = cx.shape
    keep = past.shape[2]
    assert conv_w.shape[0] == keep + 1 == 3 and keep <= dseq and _ROWS % dseq == 0
    tm = _TM_MIX
    n_pt = n_prompt // tm
    n_seq = tm // dseq
    windows = [((tm, d), _BF16)] * 2 + [((tm, d), _F32)] + [((n_seq, _SUBLANES, d), _F32)] * 2
    seq_block = lambda i: jnp.maximum(i - n_pt, 0)
    return pl.pallas_call(
        functools.partial(_mix_conv3_kernel, n_pt, seq // tm, dseq),
        grid=(m // tm,),
        in_specs=[
            pl.BlockSpec((tm, d), lambda i: (i, 0)),
            pl.BlockSpec((tm, d), lambda i: (i, 0)),
            pl.BlockSpec((None, n_seq, keep, d), lambda i: (layer, seq_block(i), 0, 0)),
            pl.BlockSpec(conv_w.shape, lambda i: (0, 0)),
        ],
        out_specs=[pl.BlockSpec((tm, d), lambda i: (i, 0)),
                   pl.BlockSpec((n_seq, keep, d), lambda i: (seq_block(i), 0, 0))],
        out_shape=[jax.ShapeDtypeStruct((m, d), _BF16), jax.ShapeDtypeStruct((past.shape[1], keep, d), _F32)],
        scratch_shapes=[pltpu.VMEM((d // _LANES, tm + _SUBLANES, _LANES), _F32)],
        compiler_params=_params(("arbitrary",), windows, _nbytes((tm + _SUBLANES, d), _F32)),
        name="mix_conv3",
    )(bg, cx, past, conv_w)


def _depthwise_taps(width, fill, slab_of, ext_ref, cw_ref, c_ref, n_blocks, in_row, out_row):
    d = c_ref.shape[1]

    def lane_block(l, carry):
        cols = pl.ds(pl.multiple_of(l * _LANES, _LANES), _LANES)
        fill(l, cols)
        slab = slab_of(l)
        w = [jnp.broadcast_to(cw_ref[k:k + 1, cols], (_SUBLANES, _LANES)) for k in range(width)]

        def row_block(b, carry2):
            base = in_row(b)
            partial = [None] * _TAP_CHAINS
            for k in range(width):
                term = w[k] * ext_ref[slab, pl.ds(base + k, _SUBLANES), :]
                c = k % _TAP_CHAINS
                partial[c] = term if partial[c] is None else partial[c] + term
            while len(partial) > 1:
                partial = [a + b_ for a, b_ in zip(partial[0::2], partial[1::2])] + (
                    [partial[-1]] if len(partial) % 2 else [])
            c_ref[pl.ds(pl.multiple_of(out_row(b), _SUBLANES), _SUBLANES), cols] = partial[0]
            return carry2
        lax.fori_loop(0, n_blocks, row_block, 0, unroll=_TAP_BLOCKS_IN_FLIGHT)
        return carry
    lax.fori_loop(0, d // _LANES, lane_block, 0)


def _conformer_tail(c_ref, cb_ref, lng_ref, lnb_ref, y_ref):
    def tail(rows):
        c = _ln_rows(c_ref[rows, :] + cb_ref[...], lng_ref[...], lnb_ref[...])
        y_ref[rows, :] = (c * jax.nn.sigmoid(c)).astype(y_ref.dtype)
    _for_row_blocks(y_ref.shape[0], tail)


def _mix_conv31_kernel(width, n_pt, seq_tiles, dseq, glu_ref, past_ref, cw_ref, cb_ref, lng_ref, lnb_ref, y_ref,
                       state_ref, ext_ref, sext_ref, c_ref):
    i = pl.program_id(0)
    tm, d = y_ref.shape
    halo = _CONV_HALO
    lead = halo - (width - 1)

    @pl.when(i < n_pt)
    def _():
        def fill(l, cols):
            @pl.when(lax.rem(i, seq_tiles) == 0)
            def _():
                ext_ref[l, 0:halo, :] = jnp.zeros((halo, _LANES), _F32)

            @pl.when(lax.rem(i, seq_tiles) != 0)
            def _():
                ext_ref[l, 0:halo, :] = ext_ref[l, tm:tm + halo, :]

            ext_ref[l, halo:halo + tm, :] = glu_ref[:, cols]

        _depthwise_taps(width, fill, lambda l: l, ext_ref, cw_ref, c_ref, tm // _SUBLANES,
                        lambda b: b * _SUBLANES + lead, lambda b: b * _SUBLANES)

    @pl.when(i >= n_pt)
    def _():
        slab = halo + dseq

        def fill(l, cols):
            def one_sequence(s, carry):
                sext_ref[0, pl.ds(s * slab + lead, width - 1), :] = past_ref[s, :, cols]
                sext_ref[0, pl.ds(pl.multiple_of(s * slab + halo, _SUBLANES), dseq), :] = (
                    glu_ref[pl.ds(pl.multiple_of(s * dseq, _SUBLANES), dseq), cols])
                state_ref[s, :, cols] = sext_ref[0, pl.ds(s * slab + lead + dseq, width - 1), :]
                return carry
            lax.fori_loop(0, tm // dseq, one_sequence, 0)

        _depthwise_taps(width, fill, lambda l: 0, sext_ref, cw_ref, c_ref, tm // dseq,
                        lambda s: s * slab + lead, lambda s: s * dseq)

    _conformer_tail(c_ref, cb_ref, lng_ref, lnb_ref, y_ref)


def _mix_conv31(glu, past, layer, conv_w, conv_b, ln_g, ln_b, n_prompt, seq, dseq):
    m, d = glu.shape
    width = conv_w.shape[0]
    assert width - 1 <= _CONV_HALO and dseq == _SUBLANES
    tm = _TM_CONV
    n_pt = n_prompt // tm
    n_seq = tm // dseq
    slab_rows = n_seq * (_CONV_HALO + dseq)
    row = pl.BlockSpec((1, d), lambda i: (0, 0))
    scratch = [((d // _LANES, tm + _CONV_HALO, _LANES), _F32), ((1, slab_rows, _LANES), _F32), ((tm, d), _F32)]
    windows = [((tm, d), _F32), ((n_seq, _CONV_HALO, d), _F32), ((tm, d), _BF16), ((n_seq, _CONV_HALO, d), _F32)]
    seq_block = lambda i: jnp.maximum(i - n_pt, 0)
    return pl.pallas_call(
        functools.partial(_mix_conv31_kernel, width, n_pt, seq // tm, dseq),
        grid=(m // tm,),
        in_specs=[pl.BlockSpec((tm, d), lambda i: (i, 0)),
                  pl.BlockSpec((None, n_seq, width - 1, d), lambda i: (layer, seq_block(i), 0, 0)),
                  pl.BlockSpec(conv_w.shape, lambda i: (0, 0)), row, row, row],
        out_specs=[pl.BlockSpec((tm, d), lambda i: (i, 0)),
                   pl.BlockSpec((n_seq, width - 1, d), lambda i: (seq_block(i), 0, 0))],
        out_shape=[jax.ShapeDtypeStruct((m, d), _BF16),
                   jax.ShapeDtypeStruct((past.shape[1], width - 1, d), _F32)],
        scratch_shapes=[pltpu.VMEM(s, dt) for s, dt in scratch],
        compiler_params=_params(("arbitrary",), windows, sum(_nbytes(s, dt) for s, dt in scratch)),
        name="mix_conv31",
    )(glu, past, conv_w, conv_b.reshape(1, d), ln_g.reshape(1, d), ln_b.reshape(1, d))


def _outproj_kernel(has_bias, y_ref, w_ref, *refs):
    if has_bias:
        b_ref, x_ref, xo_ref, wbf_ref = refs
    else:
        x_ref, xo_ref, wbf_ref = refs
    (w,) = _resident_weights([w_ref], [wbf_ref])
    acc = _dot(y_ref[...], w)
    if has_bias:
        acc = acc + b_ref[...]
    xo_ref[...] = x_ref[...] + acc


def _outproj(y, layer, w, bias, x):
    m, k = y.shape
    n = w.shape[2]
    tm, tn = _TM, _TN_OUT
    nb = n // tn
    in_specs = [pl.BlockSpec((tm, k), lambda i, j: (i, 0)),
                pl.BlockSpec((None, k, tn), _resident_weight_block(layer, 0, nb))]
    operands = [y, w]
    if bias is not None:
        in_specs.append(pl.BlockSpec((None, 1, tn), lambda i, j: (layer, 0, j)))
        operands.append(bias.reshape(bias.shape[0], 1, n))
    in_specs.append(pl.BlockSpec((tm, tn), lambda i, j: (i, j)))
    operands.append(x)
    windows = [((tm, k), _BF16), ((k, tn), _F32), ((tm, tn), _F32), ((tm, tn), _F32)]
    return pl.pallas_call(
        functools.partial(_outproj_kernel, bias is not None),
        grid=(m // tm, nb),
        in_specs=in_specs,
        out_specs=pl.BlockSpec((tm, tn), lambda i, j: (i, j)),
        out_shape=jax.ShapeDtypeStruct((m, n), _F32),
        scratch_shapes=[pltpu.VMEM((nb, k, tn), _BF16)],
        compiler_params=_params(("arbitrary", "arbitrary"), windows, _nbytes((nb, k, tn), _BF16)),
        name="outproj",
    )(*operands)


def _ffn_kernel(final, first_tile, x_hbm, g_ref, wu_ref, wd_ref, *refs):
    if final:
        gf_ref, o_hbm, acc_ref, h_ref, in_sem, out_sem = refs
    else:
        o_hbm, acc_ref, h_ref, in_sem, out_sem = refs
    i, f = pl.program_id(0), pl.program_id(1)
    n_tiles, n_steps = pl.num_programs(0), pl.num_programs(1)
    tm = acc_ref.shape[1]
    slot = lax.rem(i, 2)

    def x_copy(tile, s):
        return pltpu.make_async_copy(x_hbm.at[pl.ds((tile + first_tile) * tm, tm)], acc_ref.at[s], in_sem.at[s])

    def o_copy(tile, s):
        return pltpu.make_async_copy(acc_ref.at[s], o_hbm.at[pl.ds(tile * tm, tm)], out_sem.at[s])

    @pl.when(f == 0)
    def _():
        @pl.when(i == 0)
        def _():
            x_copy(0, 0).start()
        x_copy(i, slot).wait()

        def norm(rows):
            h_ref[rows, :] = _rms_rows(acc_ref[slot, rows, :], g_ref[...]).astype(h_ref.dtype)
        _for_row_blocks(tm, norm)

    @pl.when(jnp.logical_and(f == _FFN_PREFETCH_STEP, i + 1 < n_tiles))
    def _():
        @pl.when(i >= 1)
        def _():
            o_copy(i - 1, 1 - slot).wait()
        x_copy(i + 1, 1 - slot).start()

    a = jnp.maximum(_dot(h_ref[...], wu_ref[...].astype(_BF16)), 0.0)
    acc_ref[slot] += _dot((a * a).astype(_BF16), wd_ref[...].astype(_BF16))

    @pl.when(f == n_steps - 1)
    def _():
        if final:
            def finish(rows):
                acc_ref[slot, rows, :] = _rms_rows(acc_ref[slot, rows, :], gf_ref[...])
            _for_row_blocks(tm, finish, rows=_ROWS_IN_PLACE, unroll=1)
        o_copy(i, slot).start()

        @pl.when(i == n_tiles - 1)
        def _():
            o_copy(i, slot).wait()

            @pl.when(i >= 1)
            def _():
                o_copy(i - 1, 1 - slot).wait()


def _ffn(x, g, layer, w_up, w_down, row0, rows, g_final=None):
    d = x.shape[1]
    dff = w_up.shape[2]
    tm, tf = _TM, _TF
    assert dff // tf > _FFN_PREFETCH_STEP >= 1 and row0 % tm == 0 and rows % tm == 0
    in_specs = [pl.BlockSpec(memory_space=pl.ANY),
                pl.BlockSpec((None, 1, d), lambda i, f: (layer, 0, 0)),
                pl.BlockSpec((None, d, tf), lambda i, f: (layer, 0, f)),
                pl.BlockSpec((None, tf, d), lambda i, f: (layer, f, 0))]
    operands = [x, g.reshape(g.shape[0], 1, d), w_up, w_down]
    if g_final is not None:
        in_specs.append(pl.BlockSpec((1, d), lambda i, f: (0, 0)))
        operands.append(g_final.reshape(1, d))
    windows = [((d, tf), _F32), ((tf, d), _F32)]
    scratch = [((2, tm, d), _F32), ((tm, d), _BF16)]
    return pl.pallas_call(
        functools.partial(_ffn_kernel, g_final is not None, row0 // tm),
        grid=(rows // tm, dff // tf),
        in_specs=in_specs,
        out_specs=pl.BlockSpec(memory_space=pl.ANY),
        out_shape=jax.ShapeDtypeStruct((rows, d), _F32),
        scratch_shapes=[pltpu.VMEM(s, dt) for s, dt in scratch]
                       + [pltpu.SemaphoreType.DMA((2,)), pltpu.SemaphoreType.DMA((2,))],
        compiler_params=_params(("arbitrary", "arbitrary"), windows, sum(_nbytes(s, dt) for s, dt in scratch)),
        name="ffn",
    )(*operands)


def _gating_operands(w_s, b_s, dseq, d):
    groups, chunk, _ = w_s.shape
    t = jnp.arange(chunk)
    causal = t[:, None] >= t[None, :]
    same_seq = (t[:, None] // dseq) == (t[None, :] // dseq)
    reps = chunk // dseq
    ws_prompt = jnp.where(causal, w_s, 0)
    ws_sample = jnp.where(causal & same_seq, jnp.tile(w_s[:, :dseq, :dseq], (1, reps, reps)), 0)
    gd = d // groups
    bias_prompt = jnp.repeat(b_s.T, gd, axis=1)
    bias_sample = jnp.repeat(jnp.tile(b_s[:, :dseq].T, (reps, 1)), gd, axis=1)
    return jnp.stack([ws_prompt, ws_sample]).astype(_BF16), jnp.stack([bias_prompt, bias_sample])


def _prompt_conv_state(t, bsz, seq, keep):
    return jnp.stack([lax.slice_in_dim(t, (b + 1) * seq - keep, (b + 1) * seq, axis=0) for b in range(bsz)])


def kernel(x_prompt, x_sample, state_b_conv, state_c_conv, norm_mix_g, norm_ffn_g, final_norm_g, a_w_in, a_ln_g, a_ln_b, a_w_s, a_b_s, a_w_out, b_w_in, b_conv_w, b_w_out, c_w_pw1, c_b_pw1, c_conv_w, c_conv_b, c_ln_g, c_ln_b, c_w_pw2, c_b_pw2, ffn_w_up, ffn_w_down):
    bsz, seq, d = x_prompt.shape
    dbsz, dseq, _ = x_sample.shape
    n_p, n_s = bsz * seq, dbsz * dseq
    depth = norm_mix_g.shape[0]
    chunk = a_w_s.shape[-1]
    assert seq % chunk == 0 and chunk % dseq == 0 and seq & (seq - 1) == 0 and dseq & (dseq - 1) == 0
    assert b_conv_w.shape[1] == 3 and seq % _TM_MIX == 0 and n_s % _TM_MIX == 0
    assert n_p % _TM == 0 and n_s % _TM == 0

    x = jnp.concatenate([x_prompt.reshape(n_p, d), x_sample.reshape(n_s, d)], axis=0)
    a_v, b_prompt, b_sample, c_prompt, c_sample = [], [], [], [], []
    for i in range(depth):
        j, kind = divmod(i, 3)
        if kind == 0:
            z = _inproj(_gelu_epilogue, x, norm_mix_g, i, a_w_in, None, j, 0, 1, 2 * d, 768, 1024, _BF16,
                        "inproj_gelu")
            ws_all, bias_all = _gating_operands(a_w_s[j], a_b_s[j], dseq, d)
            y, vn = _mix_gmlp(z, a_ln_g[j], a_ln_b[j], ws_all, bias_all, n_p, n_s)
            a_v.append(vn.reshape(dbsz, dseq, d))
            x = _outproj(y, j, a_w_out, None, x)
        elif kind == 1:
            bg, h = _inproj(_plain_epilogue, x, norm_mix_g, i, b_w_in, None, j, 0, 1, d, 768, 1024, _BF16,
                            "inproj_gate", emit_h=True)
            cx = _inproj(_product_epilogue, h, None, None, b_w_in, None, j, 1, 2, d, 768, 512, _F32,
                         "inproj_product")
            y, sample_state = _mix_conv3(bg, cx, state_b_conv, j, b_conv_w[j], n_p, seq, dseq)
            b_prompt.append(_prompt_conv_state(cx, bsz, seq, state_b_conv.shape[2]))
            b_sample.append(sample_state)
            x = _outproj(y, j, b_w_out, None, x)
        else:
            glu = _inproj(_glu_epilogue, x, norm_mix_g, i, c_w_pw1, c_b_pw1, j, 0, 2, d, 768, 512, _F32,
                          "inproj_glu")
            y, sample_state = _mix_conv31(glu, state_c_conv, j, c_conv_w[j], c_conv_b[j], c_ln_g[j], c_ln_b[j],
                                          n_p, seq, dseq)
            c_prompt.append(_prompt_conv_state(glu, bsz, seq, state_c_conv.shape[2]))
            c_sample.append(sample_state)
            x = _outproj(y, j, c_w_pw2, c_b_pw2, x)
        if i + 1 < depth:
            x = _ffn(x, norm_ffn_g, i, ffn_w_up, ffn_w_down, 0, n_p + n_s)
        else:
            y_prompt = _ffn(x, norm_ffn_g, i, ffn_w_up, ffn_w_down, 0, n_p, final_norm_g)
            y_sample = _ffn(x, norm_ffn_g, i, ffn_w_up, ffn_w_down, n_p, n_s, final_norm_g)
    return (y_prompt.reshape(bsz, seq, d), y_sample.reshape(dbsz, dseq, d), jnp.stack(a_v),
            jnp.stack(b_prompt), jnp.stack(b_sample), jnp.stack(c_prompt), jnp.stack(c_sample))
```

```python
import functools

import jax
import jax.numpy as jnp
from jax import lax
from jax.experimental import pallas as pl
from jax.experimental.pallas import tpu as pltpu

_F32 = jnp.float32
_BF16 = jnp.bfloat16
_EPS = 1e-6

_LANES = 128
_SUBLANES = 8
_VMEM_CAP_BYTES = 60000 * 1024
_VMEM_SLACK_BYTES = 12 * 1024 * 1024

_TM = 1024
_TM_IN = 768
_TN_IN = 1024
_TN_IN_PAIR = 512
_TM_MIX = 512
_TN_OUT = 1024
_TF = 1024
_FFN_PREFETCH_STEP = 2
_TM_CONV = 256
_ROWS = 16
_ROWS_IN_PLACE = 64
_ROW_BLOCKS_IN_FLIGHT = 8
_CONV_HALO = 32
_TAP_CHAINS = 2
_TAP_BLOCKS_IN_FLIGHT = 16


def _nbytes(shape, dtype):
    n = 1
    for s in shape:
        n *= s
    return n * jnp.dtype(dtype).itemsize


def _params(semantics, windows, scratch=0):
    limit = 2 * sum(_nbytes(s, d) for s, d in windows) + scratch + _VMEM_SLACK_BYTES
    return pltpu.CompilerParams(dimension_semantics=semantics, vmem_limit_bytes=min(limit, _VMEM_CAP_BYTES))


def _rms_rows(x, g):
    return x * lax.rsqrt(jnp.mean(x * x, axis=-1, keepdims=True) + _EPS) * g


def _ln_rows(x, g, b):
    xc = x - jnp.mean(x, axis=-1, keepdims=True)
    return xc * lax.rsqrt(jnp.mean(xc * xc, axis=-1, keepdims=True) + _EPS) * g + b


def _for_row_blocks(n_rows, fn, rows=_ROWS, unroll=_ROW_BLOCKS_IN_FLIGHT):
    def body(r, carry):
        fn(pl.ds(pl.multiple_of(r * rows, rows), rows))
        return carry
    lax.fori_loop(0, n_rows // rows, body, 0, unroll=unroll)


def _dot(a, b):
    return jnp.dot(a, b, preferred_element_type=_F32)


def _gelu_epilogue(accs, out_ref):
    (acc,) = accs
    out_ref[...] = (0.5 * acc * (1.0 + lax.erf(acc * 0.7071067811865476))).astype(out_ref.dtype)


def _plain_epilogue(accs, out_ref):
    (acc,) = accs
    out_ref[...] = acc.astype(out_ref.dtype)


def _product_epilogue(accs, out_ref):
    out_ref[...] = (accs[0] * accs[1]).astype(out_ref.dtype)


def _glu_epilogue(accs, out_ref):
    out_ref[...] = (accs[0] * jax.nn.sigmoid(accs[1])).astype(out_ref.dtype)


def _resident_weight_block(layer, first_block, nb):
    return lambda i, j: (layer, 0, first_block + jnp.where(i == 0, j, nb - 1))


def _resident_weights(w_refs, wbf_refs):
    j = pl.program_id(1)

    @pl.when(pl.program_id(0) == 0)
    def _():
        for w_ref, wbf_ref in zip(w_refs, wbf_refs):
            wbf_ref[j] = w_ref[...].astype(wbf_ref.dtype)
    return [wbf_ref[j] for wbf_ref in wbf_refs]


def _inproj_kernel(epilogue, parts, has_bias, normed, x_ref, *refs):
    if not normed:
        g_ref, refs = refs[0], refs[1:]
    w_refs, refs = refs[:parts], refs[parts:]
    b_refs, refs = (refs[:parts], refs[parts:]) if has_bias else ((), refs)
    out_ref, refs = refs[0], refs[1:]
    if normed:
        h_ref = x_ref
    else:
        h_ref, refs = refs[0], refs[1:]

        @pl.when(pl.program_id(1) == 0)
        def _():
            def norm(rows):
                h_ref[rows, :] = _rms_rows(x_ref[rows, :], g_ref[...]).astype(h_ref.dtype)
            _for_row_blocks(h_ref.shape[0], norm)

    ws = _resident_weights(w_refs, refs)
    h = h_ref[...]
    accs = [_dot(h, w) for w in ws]
    if has_bias:
        accs = [acc + b_ref[...] for acc, b_ref in zip(accs, b_refs)]
    epilogue(accs, out_ref)


def _inproj(epilogue, x, g, g_layer, w, bias, layer, first_part, parts, n, tm, tn, out_dtype, name, emit_h=False):
    m, k = x.shape
    nb = n // tn
    normed = g is None
    in_specs = [pl.BlockSpec((tm, k), lambda i, j: (i, 0))]
    operands = [x]
    if not normed:
        in_specs.append(pl.BlockSpec((None, 1, k), lambda i, j: (g_layer, 0, 0)))
        operands.append(g.reshape(g.shape[0], 1, k))
    for p in range(parts):
        in_specs.append(pl.BlockSpec((None, k, tn), _resident_weight_block(layer, (first_part + p) * nb, nb)))
        operands.append(w)
    if bias is not None:
        for p in range(parts):
            in_specs.append(pl.BlockSpec((None, 1, tn), functools.partial(
                lambda i, j, p: (layer, 0, (first_part + p) * nb + j), p=p)))
            operands.append(bias.reshape(bias.shape[0], 1, -1))
    windows = [((tm, k), x.dtype)] + [((k, tn), _F32)] * parts + [((tm, tn), out_dtype)]
    out_specs = [pl.BlockSpec((tm, tn), lambda i, j: (i, j))]
    out_shape = [jax.ShapeDtypeStruct((m, n), out_dtype)]
    scratch = [((nb, k, tn), _BF16)] * parts
    if emit_h:
        out_specs.append(pl.BlockSpec((tm, k), lambda i, j: (i, 0)))
        out_shape.append(jax.ShapeDtypeStruct((m, k), _BF16))
        windows.append(((tm, k), _BF16))
    elif not normed:
        scratch = [((tm, k), _BF16)] + scratch
    outs = pl.pallas_call(
        functools.partial(_inproj_kernel, epilogue, parts, bias is not None, normed),
        grid=(m // tm, nb),
        in_specs=in_specs,
        out_specs=out_specs,
        out_shape=out_shape,
        scratch_shapes=[pltpu.VMEM(s, dt) for s, dt in scratch],
        compiler_params=_params(("arbitrary", "arbitrary"), windows, sum(_nbytes(s, dt) for s, dt in scratch)),
        name=name,
    )(*operands)
    return outs if emit_h else outs[0]


def _mix_gmlp_kernel(groups, chunk, u_ref, v_ref, lng_ref, lnb_ref, ws_ref, bias_ref, y_ref, vn_ref, vnb_ref):
    tm, d = y_ref.shape
    gd = d // groups

    def normalise(rows):
        vn = _ln_rows(v_ref[rows, :].astype(_F32), lng_ref[...], lnb_ref[...])
        vn_ref[rows, :] = vn
        vnb_ref[rows, :] = vn.astype(vnb_ref.dtype)
    _for_row_blocks(tm, normalise)

    def mix_chunk(c, carry):
        rows = pl.ds(pl.multiple_of(c * chunk, chunk), chunk)
        for g in range(groups):
            cols = slice(g * gd, (g + 1) * gd)
            mixed = _dot(ws_ref[0, g], vnb_ref[rows, cols]) + bias_ref[0, :, cols]
            y_ref[rows, cols] = (u_ref[rows, cols].astype(_F32) * mixed).astype(y_ref.dtype)
        return carry
    lax.fori_loop(0, tm // chunk, mix_chunk, 0, unroll=True)


def _mix_gmlp(z, ln_g, ln_b, ws_all, bias_all, n_prompt, n_sample):
    m, d2 = z.shape
    d = d2 // 2
    tm = _TM_MIX
    n_pt = n_prompt // tm
    groups, chunk = ws_all.shape[1], ws_all.shape[2]
    s_tiles = n_sample // tm
    windows = [((tm, d), _BF16)] * 3 + [((tm, d), _F32), ((1, groups, chunk, chunk), _BF16), ((1, chunk, d), _F32)]
    y, vn = pl.pallas_call(
        functools.partial(_mix_gmlp_kernel, groups, chunk),
        grid=(m // tm,),
        in_specs=[
            pl.BlockSpec((tm, d), lambda i: (i, 0)),
            pl.BlockSpec((tm, d), lambda i: (i, 1)),
            pl.BlockSpec((1, d), lambda i: (0, 0)),
            pl.BlockSpec((1, d), lambda i: (0, 0)),
            pl.BlockSpec((1, groups, chunk, chunk), lambda i: (jnp.where(i < n_pt, 0, 1), 0, 0, 0)),
            pl.BlockSpec((1, chunk, d), lambda i: (jnp.where(i < n_pt, 0, 1), 0, 0)),
        ],
        out_specs=[
            pl.BlockSpec((tm, d), lambda i: (i, 0)),
            pl.BlockSpec((tm, d), lambda i: (jnp.maximum(i - n_pt, 0), 0)),
        ],
        out_shape=[jax.ShapeDtypeStruct((m, d), _BF16), jax.ShapeDtypeStruct((s_tiles * tm, d), _F32)],
        scratch_shapes=[pltpu.VMEM((tm, d), _BF16)],
        compiler_params=_params(("arbitrary",), windows, _nbytes((tm, d), _BF16)),
        name="mix_gmlp",
    )(z, z, ln_g.reshape(1, d), ln_b.reshape(1, d), ws_all, bias_all)
    return y, vn


def _mix_conv3_kernel(n_pt, seq_tiles, dseq, bg_ref, cx_ref, past_ref, cw_ref, y_ref, state_ref, ext_ref):
    i = pl.program_id(0)
    tm, d = y_ref.shape
    halo = _SUBLANES
    seq_start = lax.rem(i, seq_tiles) == 0
    keep = past_ref.shape[1]

    def conv(l, cols, w, sample, b, carry):
        r0 = pl.multiple_of(b * _ROWS, _ROWS)
        rows = pl.ds(r0, _ROWS)
        cur = ext_ref[l, pl.ds(r0 + halo, _ROWS), :]
        prev1 = ext_ref[l, pl.ds(r0 + halo - 1, _ROWS), :]
        prev2 = ext_ref[l, pl.ds(r0 + halo - 2, _ROWS), :]
        if sample:
            pos = lax.broadcasted_iota(jnp.int32, (_ROWS, 1), 0) & (dseq - 1)
            seqs = [b * (_ROWS // dseq) + q for q in range(_ROWS // dseq)]

            def past_row(k):
                return jnp.concatenate(
                    [jnp.broadcast_to(past_ref[s, pl.ds(k, 1), cols], (dseq, _LANES)) for s in seqs], axis=0)
            prev1 = jnp.where(pos >= 1, prev1, past_row(1))
            prev2 = jnp.where(pos >= 2, prev2, jnp.where(pos == 0, past_row(0), past_row(1)))
            for q, s in enumerate(seqs):
                state_ref[s, :, cols] = ext_ref[l, pl.ds(r0 + halo + (q + 1) * dseq - keep, keep), :]
        acc = w[0] * prev2 + w[1] * prev1 + w[2] * cur
        y_ref[rows, cols] = (bg_ref[rows, cols].astype(_F32) * acc).astype(y_ref.dtype)
        return carry

    def lane_block(l, carry):
        cols = pl.ds(pl.multiple_of(l * _LANES, _LANES), _LANES)

        @pl.when(seq_start)
        def _():
            ext_ref[l, 0:halo, :] = jnp.zeros((halo, _LANES), _F32)

        @pl.when(jnp.logical_not(seq_start))
        def _():
            ext_ref[l, 0:halo, :] = ext_ref[l, tm:tm + halo, :]

        ext_ref[l, halo:halo + tm, :] = cx_ref[:, cols]
        w = [jnp.broadcast_to(cw_ref[k:k + 1, cols], (_ROWS, _LANES)) for k in range(3)]

        @pl.when(i < n_pt)
        def _():
            lax.fori_loop(0, tm // _ROWS, functools.partial(conv, l, cols, w, False), 0,
                          unroll=_ROW_BLOCKS_IN_FLIGHT)

        @pl.when(i >= n_pt)
        def _():
            lax.fori_loop(0, tm // _ROWS, functools.partial(conv, l, cols, w, True), 0,
                          unroll=_ROW_BLOCKS_IN_FLIGHT)
        return carry
    lax.fori_loop(0, d // _LANES, lane_block, 0)


def _mix_conv3(bg, cx, past, layer, conv_w, n_prompt, seq, dseq):
    m, d = cx.shape
    keep = past.shape[2]
    assert conv_w.shape[0] == keep + 1 == 3 and keep <= dseq and _ROWS % dseq == 0
    tm = _TM_MIX
    n_pt = n_prompt // tm
    n_seq = tm // dseq
    windows = [((tm, d), _BF16)] * 2 + [((tm, d), _F32)] + [((n_seq, _SUBLANES, d), _F32)] * 2
    seq_block = lambda i: jnp.maximum(i - n_pt, 0)
    return pl.pallas_call(
        functools.partial(_mix_conv3_kernel, n_pt, seq // tm, dseq),
        grid=(m // tm,),
        in_specs=[
            pl.BlockSpec((tm, d), lambda i: (i, 0)),
            pl.BlockSpec((tm, d), lambda i: (i, 0)),
            pl.BlockSpec((None, n_seq, keep, d), lambda i: (layer, seq_block(i), 0, 0)),
            pl.BlockSpec(conv_w.shape, lambda i: (0, 0)),
        ],
        out_specs=[pl.BlockSpec((tm, d), lambda i: (i, 0)),
                   pl.BlockSpec((n_seq, keep, d), lambda i: (seq_block(i), 0, 0))],
        out_shape=[jax.ShapeDtypeStruct((m, d), _BF16), jax.ShapeDtypeStruct((past.shape[1], keep, d), _F32)],
        scratch_shapes=[pltpu.VMEM((d // _LANES, tm + _SUBLANES, _LANES), _F32)],
        compiler_params=_params(("arbitrary",), windows, _nbytes((tm + _SUBLANES, d), _F32)),
        name="mix_conv3",
    )(bg, cx, past, conv_w)


def _depthwise_taps(width, fill, slab_of, ext_ref, cw_ref, c_ref, n_blocks, in_row, out_row):
    d = c_ref.shape[1]

    def lane_block(l, carry):
        cols = pl.ds(pl.multiple_of(l * _LANES, _LANES), _LANES)
        fill(l, cols)
        slab = slab_of(l)
        w = [jnp.broadcast_to(cw_ref[k:k + 1, cols], (_SUBLANES, _LANES)) for k in range(width)]

        def row_block(b, carry2):
            base = in_row(b)
            partial = [None] * _TAP_CHAINS
            for k in range(width):
                term = w[k] * ext_ref[slab, pl.ds(base + k, _SUBLANES), :]
                c = k % _TAP_CHAINS
                partial[c] = term if partial[c] is None else partial[c] + term
            while len(partial) > 1:
                partial = [a + b_ for a, b_ in zip(partial[0::2], partial[1::2])] + (
                    [partial[-1]] if len(partial) % 2 else [])
            c_ref[pl.ds(pl.multiple_of(out_row(b), _SUBLANES), _SUBLANES), cols] = partial[0]
            return carry2
        lax.fori_loop(0, n_blocks, row_block, 0, unroll=_TAP_BLOCKS_IN_FLIGHT)
        return carry
    lax.fori_loop(0, d // _LANES, lane_block, 0)


def _conformer_tail(c_ref, cb_ref, lng_ref, lnb_ref, y_ref):
    def tail(rows):
        c = _ln_rows(c_ref[rows, :] + cb_ref[...], lng_ref[...], lnb_ref[...])
        y_ref[rows, :] = (c * jax.nn.sigmoid(c)).astype(y_ref.dtype)
    _for_row_blocks(y_ref.shape[0], tail)


def _mix_conv31_kernel(width, n_pt, seq_tiles, dseq, glu_ref, past_ref, cw_ref, cb_ref, lng_ref, lnb_ref, y_ref,
                       state_ref, ext_ref, sext_ref, c_ref):
    i = pl.program_id(0)
    tm, d = y_ref.shape
    halo = _CONV_HALO
    lead = halo - (width - 1)

    @pl.when(i < n_pt)
    def _():
        def fill(l, cols):
            @pl.when(lax.rem(i, seq_tiles) == 0)
            def _():
                ext_ref[l, 0:halo, :] = jnp.zeros((halo, _LANES), _F32)

            @pl.when(lax.rem(i, seq_tiles) != 0)
            def _():
                ext_ref[l, 0:halo, :] = ext_ref[l, tm:tm + halo, :]

            ext_ref[l, halo:halo + tm, :] = glu_ref[:, cols]

        _depthwise_taps(width, fill, lambda l: l, ext_ref, cw_ref, c_ref, tm // _SUBLANES,
                        lambda b: b * _SUBLANES + lead, lambda b: b * _SUBLANES)

    @pl.when(i >= n_pt)
    def _():
        slab = halo + dseq

        def fill(l, cols):
            def one_sequence(s, carry):
                sext_ref[0, pl.ds(s * slab + lead, width - 1), :] = past_ref[s, :, cols]
                sext_ref[0, pl.ds(pl.multiple_of(s * slab + halo, _SUBLANES), dseq), :] = (
                    glu_ref[pl.ds(pl.multiple_of(s * dseq, _SUBLANES), dseq), cols])
                state_ref[s, :, cols] = sext_ref[0, pl.ds(s * slab + lead + dseq, width - 1), :]
                return carry
            lax.fori_loop(0, tm // dseq, one_sequence, 0)

        _depthwise_taps(width, fill, lambda l: 0, sext_ref, cw_ref, c_ref, tm // dseq,
                        lambda s: s * slab + lead, lambda s: s * dseq)

    _conformer_tail(c_ref, cb_ref, lng_ref, lnb_ref, y_ref)


def _mix_conv31(glu, past, layer, conv_w, conv_b, ln_g, ln_b, n_prompt, seq, dseq):
    m, d = glu.shape
    width = conv_w.shape[0]
    assert width - 1 <= _CONV_HALO and dseq == _SUBLANES
    tm = _TM_CONV
    n_pt = n_prompt // tm
    n_seq = tm // dseq
    slab_rows = n_seq * (_CONV_HALO + dseq)
    row = pl.BlockSpec((1, d), lambda i: (0, 0))
    scratch = [((d // _LANES, tm + _CONV_HALO, _LANES), _F32), ((1, slab_rows, _LANES), _F32), ((tm, d), _F32)]
    windows = [((tm, d), _F32), ((n_seq, _CONV_HALO, d), _F32), ((tm, d), _BF16), ((n_seq, _CONV_HALO, d), _F32)]
    seq_block = lambda i: jnp.maximum(i - n_pt, 0)
    return pl.pallas_call(
        functools.partial(_mix_conv31_kernel, width, n_pt, seq // tm, dseq),
        grid=(m // tm,),
        in_specs=[pl.BlockSpec((tm, d), lambda i: (i, 0)),
                  pl.BlockSpec((None, n_seq, width - 1, d), lambda i: (layer, seq_block(i), 0, 0)),
                  pl.BlockSpec(conv_w.shape, lambda i: (0, 0)), row, row, row],
        out_specs=[pl.BlockSpec((tm, d), lambda i: (i, 0)),
                   pl.BlockSpec((n_seq, width - 1, d), lambda i: (seq_block(i), 0, 0))],
        out_shape=[jax.ShapeDtypeStruct((m, d), _BF16),
                   jax.ShapeDtypeStruct((past.shape[1], width - 1, d), _F32)],
        scratch_shapes=[pltpu.VMEM(s, dt) for s, dt in scratch],
        compiler_params=_params(("arbitrary",), windows, sum(_nbytes(s, dt) for s, dt in scratch)),
        name="mix_conv31",
    )(glu, past, conv_w, conv_b.reshape(1, d), ln_g.reshape(1, d), ln_b.reshape(1, d))


def _outproj_kernel(has_bias, y_ref, w_ref, *refs):
    if has_bias:
        b_ref, x_ref, xo_ref, wbf_ref = refs
    else:
        x_ref, xo_ref, wbf_ref = refs
    (w,) = _resident_weights([w_ref], [wbf_ref])
    acc = _dot(y_ref[...], w)
    if has_bias:
        acc = acc + b_ref[...]
    xo_ref[...] = x_ref[...] + acc


def _outproj(y, layer, w, bias, x):
    m, k = y.shape
    n = w.shape[2]
    tm, tn = _TM, _TN_OUT
    nb = n // tn
    in_specs = [pl.BlockSpec((tm, k), lambda i, j: (i, 0)),
                pl.BlockSpec((None, k, tn), _resident_weight_block(layer, 0, nb))]
    operands = [y, w]
    if bias is not None:
        in_specs.append(pl.BlockSpec((None, 1, tn), lambda i, j: (layer, 0, j)))
        operands.append(bias.reshape(bias.shape[0], 1, n))
    in_specs.append(pl.BlockSpec((tm, tn), lambda i, j: (i, j)))
    operands.append(x)
    windows = [((tm, k), _BF16), ((k, tn), _F32), ((tm, tn), _F32), ((tm, tn), _F32)]
    return pl.pallas_call(
        functools.partial(_outproj_kernel, bias is not None),
        grid=(m // tm, nb),
        in_specs=in_specs,
        out_specs=pl.BlockSpec((tm, tn), lambda i, j: (i, j)),
        out_shape=jax.ShapeDtypeStruct((m, n), _F32),
        scratch_shapes=[pltpu.VMEM((nb, k, tn), _BF16)],
        compiler_params=_params(("arbitrary", "arbitrary"), windows, _nbytes((nb, k, tn), _BF16)),
        name="outproj",
    )(*operands)


def _ffn_kernel(final, first_tile, x_hbm, g_ref, wu_ref, wd_ref, *refs):
    if final:
        gf_ref, o_hbm, acc_ref, h_ref, in_sem, out_sem = refs
    else:
        o_hbm, acc_ref, h_ref, in_sem, out_sem = refs
    i, f = pl.program_id(0), pl.program_id(1)
    n_tiles, n_steps = pl.num_programs(0), pl.num_programs(1)
    tm = acc_ref.shape[1]
    slot = lax.rem(i, 2)

    def x_copy(tile, s):
        return pltpu.make_async_copy(x_hbm.at[pl.ds((tile + first_tile) * tm, tm)], acc_ref.at[s], in_sem.at[s])

    def o_copy(tile, s):
        return pltpu.make_async_copy(acc_ref.at[s], o_hbm.at[pl.ds(tile * tm, tm)], out_sem.at[s])

    @pl.when(f == 0)
    def _():
        @pl.when(i == 0)
        def _():
            x_copy(0, 0).start()
        x_copy(i, slot).wait()

        def norm(rows):
            h_ref[rows, :] = _rms_rows(acc_ref[slot, rows, :], g_ref[...]).astype(h_ref.dtype)
        _for_row_blocks(tm, norm)

    @pl.when(jnp.logical_and(f == _FFN_PREFETCH_STEP, i + 1 < n_tiles))
    def _():
        @pl.when(i >= 1)
        def _():
            o_copy(i - 1, 1 - slot).wait()
        x_copy(i + 1, 1 - slot).start()

    a = jnp.maximum(_dot(h_ref[...], wu_ref[...].astype(_BF16)), 0.0)
    acc_ref[slot] += _dot((a * a).astype(_BF16), wd_ref[...].astype(_BF16))

    @pl.when(f == n_steps - 1)
    def _():
        if final:
            def finish(rows):
                acc_ref[slot, rows, :] = _rms_rows(acc_ref[slot, rows, :], gf_ref[...])
            _for_row_blocks(tm, finish, rows=_ROWS_IN_PLACE, unroll=1)
        o_copy(i, slot).start()

        @pl.when(i == n_tiles - 1)
        def _():
            o_copy(i, slot).wait()

            @pl.when(i >= 1)
            def _():
                o_copy(i - 1, 1 - slot).wait()


def _ffn(x, g, layer, w_up, w_down, row0, rows, g_final=None):
    d = x.shape[1]
    dff = w_up.shape[2]
    tm, tf = _TM, _TF
    assert dff // tf > _FFN_PREFETCH_STEP >= 1 and row0 % tm == 0 and rows % tm == 0
    in_specs = [pl.BlockSpec(memory_space=pl.ANY),
                pl.BlockSpec((None, 1, d), lambda i, f: (layer, 0, 0)),
                pl.BlockSpec((None, d, tf), lambda i, f: (layer, 0, f)),
                pl.BlockSpec((None, tf, d), lambda i, f: (layer, f, 0))]
    operands = [x, g.reshape(g.shape[0], 1, d), w_up, w_down]
    if g_final is not None:
        in_specs.append(pl.BlockSpec((1, d), lambda i, f: (0, 0)))
        operands.append(g_final.reshape(1, d))
    windows = [((d, tf), _F32), ((tf, d), _F32)]
    scratch = [((2, tm, d), _F32), ((tm, d), _BF16)]
    return pl.pallas_call(
        functools.partial(_ffn_kernel, g_final is not None, row0 // tm),
        grid=(rows // tm, dff // tf),
        in_specs=in_specs,
        out_specs=pl.BlockSpec(memory_space=pl.ANY),
        out_shape=jax.ShapeDtypeStruct((rows, d), _F32),
        scratch_shapes=[pltpu.VMEM(s, dt) for s, dt in scratch]
                       + [pltpu.SemaphoreType.DMA((2,)), pltpu.SemaphoreType.DMA((2,))],
        compiler_params=_params(("arbitrary", "arbitrary"), windows, sum(_nbytes(s, dt) for s, dt in scratch)),
        name="ffn",
    )(*operands)


def _gating_operands(w_s, b_s, dseq, d):
    groups, chunk, _ = w_s.shape
    t = jnp.arange(chunk)
    causal = t[:, None] >= t[None, :]
    same_seq = (t[:, None] // dseq) == (t[None, :] // dseq)
    reps = chunk // dseq
    ws_prompt = jnp.where(causal, w_s, 0)
    ws_sample = jnp.where(causal & same_seq, jnp.tile(w_s[:, :dseq, :dseq], (1, reps, reps)), 0)
    gd = d // groups
    bias_prompt = jnp.repeat(b_s.T, gd, axis=1)
    bias_sample = jnp.repeat(jnp.tile(b_s[:, :dseq].T, (reps, 1)), gd, axis=1)
    return jnp.stack([ws_prompt, ws_sample]).astype(_BF16), jnp.stack([bias_prompt, bias_sample])


def _prompt_conv_state(t, bsz, seq, keep):
    return jnp.stack([lax.slice_in_dim(t, (b + 1) * seq - keep, (b + 1) * seq, axis=0) for b in range(bsz)])


def kernel(x_prompt, x_sample, state_b_conv, state_c_conv, norm_mix_g, norm_ffn_g, final_norm_g, a_w_in, a_ln_g, a_ln_b, a_w_s, a_b_s, a_w_out, b_w_in, b_conv_w, b_w_out, c_w_pw1, c_b_pw1, c_conv_w, c_conv_b, c_ln_g, c_ln_b, c_w_pw2, c_b_pw2, ffn_w_up, ffn_w_down):
    bsz, seq, d = x_prompt.shape
    dbsz, dseq, _ = x_sample.shape
    n_p, n_s = bsz * seq, dbsz * dseq
    depth = norm_mix_g.shape[0]
    chunk = a_w_s.shape[-1]
    assert seq % chunk == 0 and chunk % dseq == 0 and seq & (seq - 1) == 0 and dseq & (dseq - 1) == 0
    assert seq % _TM_MIX == 0 and n_s % _TM_MIX == 0 and seq % _TM_CONV == 0 and n_s % _TM_CONV == 0
    assert n_p % _TM == 0 and n_s % _TM == 0 and (n_p + n_s) % _TM_IN == 0

    x = jnp.concatenate([x_prompt.reshape(n_p, d), x_sample.reshape(n_s, d)], axis=0)
    a_v, b_prompt, b_sample, c_prompt, c_sample = [], [], [], [], []
    for i in range(depth):
        j, kind = divmod(i, 3)
        if kind == 0:
            z = _inproj(_gelu_epilogue, x, norm_mix_g, i, a_w_in, None, j, 0, 1, 2 * d, _TM_IN, _TN_IN, _BF16,
                        "inproj_gelu")
            ws_all, bias_all = _gating_operands(a_w_s[j], a_b_s[j], dseq, d)
            y, vn = _mix_gmlp(z, a_ln_g[j], a_ln_b[j], ws_all, bias_all, n_p, n_s)
            a_v.append(vn.reshape(dbsz, dseq, d))
            x = _outproj(y, j, a_w_out, None, x)
        elif kind == 1:
            bg, h = _inproj(_plain_epilogue, x, norm_mix_g, i, b_w_in, None, j, 0, 1, d, _TM_IN, _TN_IN, _BF16,
                            "inproj_gate", emit_h=True)
            cx = _inproj(_product_epilogue, h, None, None, b_w_in, None, j, 1, 2, d, _TM_IN, _TN_IN_PAIR, _F32,
                         "inproj_product")
            y, sample_state = _mix_conv3(bg, cx, state_b_conv, j, b_conv_w[j], n_p, seq, dseq)
            b_prompt.append(_prompt_conv_state(cx, bsz, seq, state_b_conv.shape[2]))
            b_sample.append(sample_state)
            x = _outproj(y, j, b_w_out, None, x)
        else:
            glu = _inproj(_glu_epilogue, x, norm_mix_g, i, c_w_pw1, c_b_pw1, j, 0, 2, d, _TM_IN, _TN_IN_PAIR, _F32,
                          "inproj_glu")
            y, sample_state = _mix_conv31(glu, state_c_conv, j, c_conv_w[j], c_conv_b[j], c_ln_g[j], c_ln_b[j],
                                          n_p, seq, dseq)
            c_prompt.append(_prompt_conv_state(glu, bsz, seq, state_c_conv.shape[2]))
            c_sample.append(sample_state)
            x = _outproj(y, j, c_w_pw2, c_b_pw2, x)
        if i + 1 < depth:
            x = _ffn(x, norm_ffn_g, i, ffn_w_up, ffn_w_down, 0, n_p + n_s)
        else:
            y_prompt = _ffn(x, norm_ffn_g, i, ffn_w_up, ffn_w_down, 0, n_p, final_norm_g)
            y_sample = _ffn(x, norm_ffn_g, i, ffn_w_up, ffn_w_down, n_p, n_s, final_norm_g)
    return (y_prompt.reshape(bsz, seq, d), y_sample.reshape(dbsz, dseq, d), jnp.stack(a_v),
            jnp.stack(b_prompt), jnp.stack(b_sample), jnp.stack(c_prompt), jnp.stack(c_sample))
```

```python
import functools

import jax
import jax.numpy as jnp
from jax import lax
from jax.experimental import pallas as pl
from jax.experimental.pallas import tpu as pltpu

_F32 = jnp.float32
_BF16 = jnp.bfloat16
_EPS = 1e-6
_N_MIXERS = 3

_LANES = 128
_SUBLANES = 8
_VMEM_CAP_BYTES = 60000 * 1024
_VMEM_SLACK_BYTES = 12 * 1024 * 1024

_TM = 1024
_TM_IN = 768
_TN_IN = 1024
_TN_IN_PAIR = 512
_TM_MIX = 512
_TN_OUT = 1024
_TF = 1024
_FFN_PREFETCH_STEP = 2
_TM_CONV = 256
_ROWS = 16
_ROWS_IN_PLACE = 64
_ROW_BLOCKS_IN_FLIGHT = 8
_CONV_HALO = 32
_TAP_CHAINS = 2
_TAP_BLOCKS_IN_FLIGHT = 16


def _nbytes(shape, dtype):
    n = 1
    for s in shape:
        n *= s
    return n * jnp.dtype(dtype).itemsize


def _params(semantics, windows, scratch=0):
    limit = 2 * sum(_nbytes(s, d) for s, d in windows) + scratch + _VMEM_SLACK_BYTES
    return pltpu.CompilerParams(dimension_semantics=semantics, vmem_limit_bytes=min(limit, _VMEM_CAP_BYTES))


def _rms_rows(x, g):
    return x * lax.rsqrt(jnp.mean(x * x, axis=-1, keepdims=True) + _EPS) * g


def _ln_rows(x, g, b):
    xc = x - jnp.mean(x, axis=-1, keepdims=True)
    return xc * lax.rsqrt(jnp.mean(xc * xc, axis=-1, keepdims=True) + _EPS) * g + b


def _for_row_blocks(n_rows, fn, rows=_ROWS, unroll=_ROW_BLOCKS_IN_FLIGHT):
    def body(r, carry):
        fn(pl.ds(pl.multiple_of(r * rows, rows), rows))
        return carry
    lax.fori_loop(0, n_rows // rows, body, 0, unroll=unroll)


def _dot(a, b):
    return jnp.dot(a, b, preferred_element_type=_F32)


def _gelu_epilogue(accs, out_ref):
    (acc,) = accs
    out_ref[...] = (0.5 * acc * (1.0 + lax.erf(acc * 0.7071067811865476))).astype(out_ref.dtype)


def _plain_epilogue(accs, out_ref):
    (acc,) = accs
    out_ref[...] = acc.astype(out_ref.dtype)


def _product_epilogue(accs, out_ref):
    out_ref[...] = (accs[0] * accs[1]).astype(out_ref.dtype)


def _glu_epilogue(accs, out_ref):
    out_ref[...] = (accs[0] * jax.nn.sigmoid(accs[1])).astype(out_ref.dtype)


def _resident_weight_block(layer, first_block, nb):
    return lambda i, j: (layer, 0, first_block + jnp.where(i == 0, j, nb - 1))


def _resident_weights(w_refs, wbf_refs):
    j = pl.program_id(1)

    @pl.when(pl.program_id(0) == 0)
    def _():
        for w_ref, wbf_ref in zip(w_refs, wbf_refs):
            wbf_ref[j] = w_ref[...].astype(wbf_ref.dtype)
    return [wbf_ref[j] for wbf_ref in wbf_refs]


def _inproj_kernel(epilogue, parts, has_bias, normed, x_ref, *refs):
    if not normed:
        g_ref, refs = refs[0], refs[1:]
    w_refs, refs = refs[:parts], refs[parts:]
    b_refs, refs = (refs[:parts], refs[parts:]) if has_bias else ((), refs)
    out_ref, refs = refs[0], refs[1:]
    if normed:
        h_ref = x_ref
    else:
        h_ref, refs = refs[0], refs[1:]

        @pl.when(pl.program_id(1) == 0)
        def _():
            def norm(rows):
                h_ref[rows, :] = _rms_rows(x_ref[rows, :], g_ref[...]).astype(h_ref.dtype)
            _for_row_blocks(h_ref.shape[0], norm)

    ws = _resident_weights(w_refs, refs)
    h = h_ref[...]
    accs = [_dot(h, w) for w in ws]
    if has_bias:
        accs = [acc + b_ref[...] for acc, b_ref in zip(accs, b_refs)]
    epilogue(accs, out_ref)


def _inproj(epilogue, x, g, g_layer, w, bias, layer, first_part, parts, n, tm, tn, out_dtype, name, emit_h=False):
    m, k = x.shape
    nb = n // tn
    normed = g is None
    in_specs = [pl.BlockSpec((tm, k), lambda i, j: (i, 0))]
    operands = [x]
    if not normed:
        in_specs.append(pl.BlockSpec((None, 1, k), lambda i, j: (g_layer, 0, 0)))
        operands.append(g.reshape(g.shape[0], 1, k))
    for p in range(parts):
        in_specs.append(pl.BlockSpec((None, k, tn), _resident_weight_block(layer, (first_part + p) * nb, nb)))
        operands.append(w)
    if bias is not None:
        for p in range(parts):
            in_specs.append(pl.BlockSpec((None, 1, tn), functools.partial(
                lambda i, j, p: (layer, 0, (first_part + p) * nb + j), p=p)))
            operands.append(bias.reshape(bias.shape[0], 1, -1))
    windows = [((tm, k), x.dtype)] + [((k, tn), _F32)] * parts + [((tm, tn), out_dtype)]
    out_specs = [pl.BlockSpec((tm, tn), lambda i, j: (i, j))]
    out_shape = [jax.ShapeDtypeStruct((m, n), out_dtype)]
    scratch = [((nb, k, tn), _BF16)] * parts
    if emit_h:
        out_specs.append(pl.BlockSpec((tm, k), lambda i, j: (i, 0)))
        out_shape.append(jax.ShapeDtypeStruct((m, k), _BF16))
        windows.append(((tm, k), _BF16))
    elif not normed:
        scratch = [((tm, k), _BF16)] + scratch
    outs = pl.pallas_call(
        functools.partial(_inproj_kernel, epilogue, parts, bias is not None, normed),
        grid=(m // tm, nb),
        in_specs=in_specs,
        out_specs=out_specs,
        out_shape=out_shape,
        scratch_shapes=[pltpu.VMEM(s, dt) for s, dt in scratch],
        compiler_params=_params(("arbitrary", "arbitrary"), windows, sum(_nbytes(s, dt) for s, dt in scratch)),
        name=name,
    )(*operands)
    return outs if emit_h else outs[0]


def _mix_gmlp_kernel(groups, chunk, u_ref, v_ref, lng_ref, lnb_ref, ws_ref, bias_ref, y_ref, vn_ref, vnb_ref):
    tm, d = y_ref.shape
    gd = d // groups

    def normalise(rows):
        vn = _ln_rows(v_ref[rows, :].astype(_F32), lng_ref[...], lnb_ref[...])
        vn_ref[rows, :] = vn
        vnb_ref[rows, :] = vn.astype(vnb_ref.dtype)
    _for_row_blocks(tm, normalise)

    def mix_chunk(c, carry):
        rows = pl.ds(pl.multiple_of(c * chunk, chunk), chunk)
        for g in range(groups):
            cols = slice(g * gd, (g + 1) * gd)
            mixed = _dot(ws_ref[0, g], vnb_ref[rows, cols]) + bias_ref[0, :, cols]
            y_ref[rows, cols] = (u_ref[rows, cols].astype(_F32) * mixed).astype(y_ref.dtype)
        return carry
    lax.fori_loop(0, tm // chunk, mix_chunk, 0, unroll=True)


def _mix_gmlp(z, ln_g, ln_b, ws_all, bias_all, n_prompt, n_sample):
    m, d2 = z.shape
    d = d2 // 2
    tm = _TM_MIX
    n_pt = n_prompt // tm
    groups, chunk = ws_all.shape[1], ws_all.shape[2]
    s_tiles = n_sample // tm
    windows = [((tm, d), _BF16)] * 3 + [((tm, d), _F32), ((1, groups, chunk, chunk), _BF16), ((1, chunk, d), _F32)]
    y, vn = pl.pallas_call(
        functools.partial(_mix_gmlp_kernel, groups, chunk),
        grid=(m // tm,),
        in_specs=[
            pl.BlockSpec((tm, d), lambda i: (i, 0)),
            pl.BlockSpec((tm, d), lambda i: (i, 1)),
            pl.BlockSpec((1, d), lambda i: (0, 0)),
            pl.BlockSpec((1, d), lambda i: (0, 0)),
            pl.BlockSpec((1, groups, chunk, chunk), lambda i: (jnp.where(i < n_pt, 0, 1), 0, 0, 0)),
            pl.BlockSpec((1, chunk, d), lambda i: (jnp.where(i < n_pt, 0, 1), 0, 0)),
        ],
        out_specs=[
            pl.BlockSpec((tm, d), lambda i: (i, 0)),
            pl.BlockSpec((tm, d), lambda i: (jnp.maximum(i - n_pt, 0), 0)),
        ],
        out_shape=[jax.ShapeDtypeStruct((m, d), _BF16), jax.ShapeDtypeStruct((s_tiles * tm, d), _F32)],
        scratch_shapes=[pltpu.VMEM((tm, d), _BF16)],
        compiler_params=_params(("arbitrary",), windows, _nbytes((tm, d), _BF16)),
        name="mix_gmlp",
    )(z, z, ln_g.reshape(1, d), ln_b.reshape(1, d), ws_all, bias_all)
    return y, vn


def _mix_conv3_kernel(n_pt, seq_tiles, dseq, bg_ref, cx_ref, past_ref, cw_ref, y_ref, state_ref, ext_ref):
    i = pl.program_id(0)
    tm, d = y_ref.shape
    halo = _SUBLANES
    seq_start = lax.rem(i, seq_tiles) == 0
    keep = past_ref.shape[1]

    def conv(l, cols, w, sample, b, carry):
        r0 = pl.multiple_of(b * _ROWS, _ROWS)
        rows = pl.ds(r0, _ROWS)
        cur = ext_ref[l, pl.ds(r0 + halo, _ROWS), :]
        prev1 = ext_ref[l, pl.ds(r0 + halo - 1, _ROWS), :]
        prev2 = ext_ref[l, pl.ds(r0 + halo - 2, _ROWS), :]
        if sample:
            pos = lax.broadcasted_iota(jnp.int32, (_ROWS, 1), 0) & (dseq - 1)
            seqs = [b * (_ROWS // dseq) + q for q in range(_ROWS // dseq)]

            def past_row(k):
                return jnp.concatenate(
                    [jnp.broadcast_to(past_ref[s, pl.ds(k, 1), cols], (dseq, _LANES)) for s in seqs], axis=0)
            prev1 = jnp.where(pos >= 1, prev1, past_row(1))
            prev2 = jnp.where(pos >= 2, prev2, jnp.where(pos == 0, past_row(0), past_row(1)))
            for q, s in enumerate(seqs):
                state_ref[s, :, cols] = ext_ref[l, pl.ds(r0 + halo + (q + 1) * dseq - keep, keep), :]
        acc = w[0] * prev2 + w[1] * prev1 + w[2] * cur
        y_ref[rows, cols] = (bg_ref[rows, cols].astype(_F32) * acc).astype(y_ref.dtype)
        return carry

    def lane_block(l, carry):
        cols = pl.ds(pl.multiple_of(l * _LANES, _LANES), _LANES)

        @pl.when(seq_start)
        def _():
            ext_ref[l, 0:halo, :] = jnp.zeros((halo, _LANES), _F32)

        @pl.when(jnp.logical_not(seq_start))
        def _():
            ext_ref[l, 0:halo, :] = ext_ref[l, tm:tm + halo, :]

        ext_ref[l, halo:halo + tm, :] = cx_ref[:, cols]
        w = [jnp.broadcast_to(cw_ref[k:k + 1, cols], (_ROWS, _LANES)) for k in range(3)]

        @pl.when(i < n_pt)
        def _():
            lax.fori_loop(0, tm // _ROWS, functools.partial(conv, l, cols, w, False), 0,
                          unroll=_ROW_BLOCKS_IN_FLIGHT)

        @pl.when(i >= n_pt)
        def _():
            lax.fori_loop(0, tm // _ROWS, functools.partial(conv, l, cols, w, True), 0,
                          unroll=_ROW_BLOCKS_IN_FLIGHT)
        return carry
    lax.fori_loop(0, d // _LANES, lane_block, 0)


def _mix_conv3(bg, cx, past, layer, conv_w, n_prompt, seq, dseq):
    m, d = cx.shape
    keep = past.shape[2]
    assert conv_w.shape[0] == keep + 1 == 3 and keep <= dseq and _ROWS % dseq == 0
    tm = _TM_MIX
    n_pt = n_prompt // tm
    n_seq = tm // dseq
    windows = [((tm, d), _BF16)] * 2 + [((tm, d), _F32)] + [((n_seq, _SUBLANES, d), _F32)] * 2
    seq_block = lambda i: jnp.maximum(i - n_pt, 0)
    return pl.pallas_call(
        functools.partial(_mix_conv3_kernel, n_pt, seq // tm, dseq),
        grid=(m // tm,),
        in_specs=[
            pl.BlockSpec((tm, d), lambda i: (i, 0)),
            pl.BlockSpec((tm, d), lambda i: (i, 0)),
            pl.BlockSpec((None, n_seq, keep, d), lambda i: (layer, seq_block(i), 0, 0)),
            pl.BlockSpec(conv_w.shape, lambda i: (0, 0)),
        ],
        out_specs=[pl.BlockSpec((tm, d), lambda i: (i, 0)),
                   pl.BlockSpec((n_seq, keep, d), lambda i: (seq_block(i), 0, 0))],
        out_shape=[jax.ShapeDtypeStruct((m, d), _BF16), jax.ShapeDtypeStruct((past.shape[1], keep, d), _F32)],
        scratch_shapes=[pltpu.VMEM((d // _LANES, tm + _SUBLANES, _LANES), _F32)],
        compiler_params=_params(("arbitrary",), windows, _nbytes((tm + _SUBLANES, d), _F32)),
        name="mix_conv3",
    )(bg, cx, past, conv_w)


def _depthwise_taps(width, fill, slab_of, ext_ref, cw_ref, c_ref, n_blocks, in_row, out_row):
    d = c_ref.shape[1]

    def lane_block(l, carry):
        cols = pl.ds(pl.multiple_of(l * _LANES, _LANES), _LANES)
        fill(l, cols)
        slab = slab_of(l)
        w = [jnp.broadcast_to(cw_ref[k:k + 1, cols], (_SUBLANES, _LANES)) for k in range(width)]

        def row_block(b, carry2):
            base = in_row(b)
            partial = [None] * _TAP_CHAINS
            for k in range(width):
                term = w[k] * ext_ref[slab, pl.ds(base + k, _SUBLANES), :]
                c = k % _TAP_CHAINS
                partial[c] = term if partial[c] is None else partial[c] + term
            while len(partial) > 1:
                partial = [a + b_ for a, b_ in zip(partial[0::2], partial[1::2])] + (
                    [partial[-1]] if len(partial) % 2 else [])
            c_ref[pl.ds(pl.multiple_of(out_row(b), _SUBLANES), _SUBLANES), cols] = partial[0]
            return carry2
        lax.fori_loop(0, n_blocks, row_block, 0, unroll=_TAP_BLOCKS_IN_FLIGHT)
        return carry
    lax.fori_loop(0, d // _LANES, lane_block, 0)


def _conformer_tail(c_ref, cb_ref, lng_ref, lnb_ref, y_ref):
    def tail(rows):
        c = _ln_rows(c_ref[rows, :] + cb_ref[...], lng_ref[...], lnb_ref[...])
        y_ref[rows, :] = (c * jax.nn.sigmoid(c)).astype(y_ref.dtype)
    _for_row_blocks(y_ref.shape[0], tail)


def _mix_conv31_kernel(width, n_pt, seq_tiles, dseq, glu_ref, past_ref, cw_ref, cb_ref, lng_ref, lnb_ref, y_ref,
                       state_ref, ext_ref, sext_ref, c_ref):
    i = pl.program_id(0)
    tm, d = y_ref.shape
    halo = _CONV_HALO
    lead = halo - (width - 1)

    @pl.when(i < n_pt)
    def _():
        def fill(l, cols):
            @pl.when(lax.rem(i, seq_tiles) == 0)
            def _():
                ext_ref[l, 0:halo, :] = jnp.zeros((halo, _LANES), _F32)

            @pl.when(lax.rem(i, seq_tiles) != 0)
            def _():
                ext_ref[l, 0:halo, :] = ext_ref[l, tm:tm + halo, :]

            ext_ref[l, halo:halo + tm, :] = glu_ref[:, cols]

        _depthwise_taps(width, fill, lambda l: l, ext_ref, cw_ref, c_ref, tm // _SUBLANES,
                        lambda b: b * _SUBLANES + lead, lambda b: b * _SUBLANES)

    @pl.when(i >= n_pt)
    def _():
        slab = halo + dseq

        def fill(l, cols):
            def one_sequence(s, carry):
                sext_ref[0, pl.ds(s * slab + lead, width - 1), :] = past_ref[s, :, cols]
                sext_ref[0, pl.ds(pl.multiple_of(s * slab + halo, _SUBLANES), dseq), :] = (
                    glu_ref[pl.ds(pl.multiple_of(s * dseq, _SUBLANES), dseq), cols])
                state_ref[s, :, cols] = sext_ref[0, pl.ds(s * slab + lead + dseq, width - 1), :]
                return carry
            lax.fori_loop(0, tm // dseq, one_sequence, 0)

        _depthwise_taps(width, fill, lambda l: 0, sext_ref, cw_ref, c_ref, tm // dseq,
                        lambda s: s * slab + lead, lambda s: s * dseq)

    _conformer_tail(c_ref, cb_ref, lng_ref, lnb_ref, y_ref)


def _mix_conv31(glu, past, layer, conv_w, conv_b, ln_g, ln_b, n_prompt, seq, dseq):
    m, d = glu.shape
    width = conv_w.shape[0]
    assert width - 1 <= _CONV_HALO and dseq == _SUBLANES
    tm = _TM_CONV
    n_pt = n_prompt // tm
    n_seq = tm // dseq
    slab_rows = n_seq * (_CONV_HALO + dseq)
    row = pl.BlockSpec((1, d), lambda i: (0, 0))
    scratch = [((d // _LANES, tm + _CONV_HALO, _LANES), _F32), ((1, slab_rows, _LANES), _F32), ((tm, d), _F32)]
    windows = [((tm, d), _F32), ((n_seq, _CONV_HALO, d), _F32), ((tm, d), _BF16), ((n_seq, _CONV_HALO, d), _F32)]
    seq_block = lambda i: jnp.maximum(i - n_pt, 0)
    return pl.pallas_call(
        functools.partial(_mix_conv31_kernel, width, n_pt, seq // tm, dseq),
        grid=(m // tm,),
        in_specs=[pl.BlockSpec((tm, d), lambda i: (i, 0)),
                  pl.BlockSpec((None, n_seq, width - 1, d), lambda i: (layer, seq_block(i), 0, 0)),
                  pl.BlockSpec(conv_w.shape, lambda i: (0, 0)), row, row, row],
        out_specs=[pl.BlockSpec((tm, d), lambda i: (i, 0)),
                   pl.BlockSpec((n_seq, width - 1, d), lambda i: (seq_block(i), 0, 0))],
        out_shape=[jax.ShapeDtypeStruct((m, d), _BF16),
                   jax.ShapeDtypeStruct((past.shape[1], width - 1, d), _F32)],
        scratch_shapes=[pltpu.VMEM(s, dt) for s, dt in scratch],
        compiler_params=_params(("arbitrary",), windows, sum(_nbytes(s, dt) for s, dt in scratch)),
        name="mix_conv31",
    )(glu, past, conv_w, conv_b.reshape(1, d), ln_g.reshape(1, d), ln_b.reshape(1, d))


def _outproj_kernel(has_bias, y_ref, w_ref, *refs):
    if has_bias:
        b_ref, x_ref, xo_ref, wbf_ref = refs
    else:
        x_ref, xo_ref, wbf_ref = refs
    (w,) = _resident_weights([w_ref], [wbf_ref])
    acc = _dot(y_ref[...], w)
    if has_bias:
        acc = acc + b_ref[...]
    xo_ref[...] = x_ref[...] + acc


def _outproj(y, layer, w, bias, x):
    m, k = y.shape
    n = w.shape[2]
    tm, tn = _TM, _TN_OUT
    nb = n // tn
    in_specs = [pl.BlockSpec((tm, k), lambda i, j: (i, 0)),
                pl.BlockSpec((None, k, tn), _resident_weight_block(layer, 0, nb))]
    operands = [y, w]
    if bias is not None:
        in_specs.append(pl.BlockSpec((None, 1, tn), lambda i, j: (layer, 0, j)))
        operands.append(bias.reshape(bias.shape[0], 1, n))
    in_specs.append(pl.BlockSpec((tm, tn), lambda i, j: (i, j)))
    operands.append(x)
    windows = [((tm, k), _BF16), ((k, tn), _F32), ((tm, tn), _F32), ((tm, tn), _F32)]
    return pl.pallas_call(
        functools.partial(_outproj_kernel, bias is not None),
        grid=(m // tm, nb),
        in_specs=in_specs,
        out_specs=pl.BlockSpec((tm, tn), lambda i, j: (i, j)),
        out_shape=jax.ShapeDtypeStruct((m, n), _F32),
        scratch_shapes=[pltpu.VMEM((nb, k, tn), _BF16)],
        compiler_params=_params(("arbitrary", "arbitrary"), windows, _nbytes((nb, k, tn), _BF16)),
        name="outproj",
    )(*operands)


def _ffn_kernel(final, split_tile, x_hbm, g_ref, wu_ref, wd_ref, *refs):
    if final:
        gf_ref, o_head_hbm, o_tail_hbm, acc_ref, h_ref, in_sem, out_sem = refs
    else:
        o_head_hbm, acc_ref, h_ref, in_sem, out_sem = refs
    i, f = pl.program_id(0), pl.program_id(1)
    n_tiles, n_steps = pl.num_programs(0), pl.num_programs(1)
    tm = acc_ref.shape[1]
    slot = lax.rem(i, 2)

    def x_copy(tile, s):
        return pltpu.make_async_copy(x_hbm.at[pl.ds(tile * tm, tm)], acc_ref.at[s], in_sem.at[s])

    def o_copy(tile, s):
        def for_each(act):
            if final:
                @pl.when(tile < split_tile)
                def _():
                    act(pltpu.make_async_copy(acc_ref.at[s], o_head_hbm.at[pl.ds(tile * tm, tm)], out_sem.at[s]))

                @pl.when(tile >= split_tile)
                def _():
                    act(pltpu.make_async_copy(acc_ref.at[s], o_tail_hbm.at[pl.ds((tile - split_tile) * tm, tm)],
                                              out_sem.at[s]))
            else:
                act(pltpu.make_async_copy(acc_ref.at[s], o_head_hbm.at[pl.ds(tile * tm, tm)], out_sem.at[s]))
        return for_each

    def start(copy):
        copy.start()

    def wait(copy):
        copy.wait()

    @pl.when(f == 0)
    def _():
        @pl.when(i == 0)
        def _():
            x_copy(0, 0).start()
        x_copy(i, slot).wait()

        def norm(rows):
            h_ref[rows, :] = _rms_rows(acc_ref[slot, rows, :], g_ref[...]).astype(h_ref.dtype)
        _for_row_blocks(tm, norm)

    @pl.when(jnp.logical_and(f == _FFN_PREFETCH_STEP, i + 1 < n_tiles))
    def _():
        @pl.when(i >= 1)
        def _():
            o_copy(i - 1, 1 - slot)(wait)
        x_copy(i + 1, 1 - slot).start()

    a = jnp.maximum(_dot(h_ref[...], wu_ref[...].astype(_BF16)), 0.0)
    acc_ref[slot] += _dot((a * a).astype(_BF16), wd_ref[...].astype(_BF16))

    @pl.when(f == n_steps - 1)
    def _():
        if final:
            def finish(rows):
                acc_ref[slot, rows, :] = _rms_rows(acc_ref[slot, rows, :], gf_ref[...])
            _for_row_blocks(tm, finish, rows=_ROWS_IN_PLACE, unroll=1)
        o_copy(i, slot)(start)

        @pl.when(i == n_tiles - 1)
        def _():
            o_copy(i, slot)(wait)

            @pl.when(i >= 1)
            def _():
                o_copy(i - 1, 1 - slot)(wait)


def _ffn(x, g, layer, w_up, w_down, g_final=None, split=None):
    rows, d = x.shape
    dff = w_up.shape[2]
    tm, tf = _TM, _TF
    assert dff // tf > _FFN_PREFETCH_STEP >= 1 and rows % tm == 0
    final = g_final is not None
    if final:
        assert split % tm == 0
        out_shape = [jax.ShapeDtypeStruct((split, d), _F32), jax.ShapeDtypeStruct((rows - split, d), _F32)]
    else:
        out_shape = [jax.ShapeDtypeStruct((rows, d), _F32)]
    in_specs = [pl.BlockSpec(memory_space=pl.ANY),
                pl.BlockSpec((None, 1, d), lambda i, f: (layer, 0, 0)),
                pl.BlockSpec((None, d, tf), lambda i, f: (layer, 0, f)),
                pl.BlockSpec((None, tf, d), lambda i, f: (layer, f, 0))]
    operands = [x, g.reshape(g.shape[0], 1, d), w_up, w_down]
    if final:
        in_specs.append(pl.BlockSpec((1, d), lambda i, f: (0, 0)))
        operands.append(g_final.reshape(1, d))
    windows = [((d, tf), _F32), ((tf, d), _F32)]
    scratch = [((2, tm, d), _F32), ((tm, d), _BF16)]
    outs = pl.pallas_call(
        functools.partial(_ffn_kernel, final, split // tm if final else None),
        grid=(rows // tm, dff // tf),
        in_specs=in_specs,
        out_specs=[pl.BlockSpec(memory_space=pl.ANY)] * len(out_shape),
        out_shape=out_shape,
        scratch_shapes=[pltpu.VMEM(s, dt) for s, dt in scratch]
                       + [pltpu.SemaphoreType.DMA((2,)), pltpu.SemaphoreType.DMA((2,))],
        compiler_params=_params(("arbitrary", "arbitrary"), windows, sum(_nbytes(s, dt) for s, dt in scratch)),
        name="ffn",
    )(*operands)
    return outs if final else outs[0]


def _gating_operands(w_s, b_s, dseq, d):
    groups, chunk, _ = w_s.shape
    t = jnp.arange(chunk)
    causal = t[:, None] >= t[None, :]
    same_seq = (t[:, None] // dseq) == (t[None, :] // dseq)
    reps = chunk // dseq
    ws_prompt = jnp.where(causal, w_s, 0)
    ws_sample = jnp.where(causal & same_seq, jnp.tile(w_s[:, :dseq, :dseq], (1, reps, reps)), 0)
    gd = d // groups
    bias_prompt = jnp.repeat(b_s.T, gd, axis=1)
    bias_sample = jnp.repeat(jnp.tile(b_s[:, :dseq].T, (reps, 1)), gd, axis=1)
    return jnp.stack([ws_prompt, ws_sample]).astype(_BF16), jnp.stack([bias_prompt, bias_sample])


def _prompt_conv_state(t, bsz, seq, keep):
    return jnp.stack([lax.slice_in_dim(t, (b + 1) * seq - keep, (b + 1) * seq, axis=0) for b in range(bsz)])


def kernel(x_prompt, x_sample, state_b_conv, state_c_conv, norm_mix_g, norm_ffn_g, final_norm_g, a_w_in, a_ln_g, a_ln_b, a_w_s, a_b_s, a_w_out, b_w_in, b_conv_w, b_w_out, c_w_pw1, c_b_pw1, c_conv_w, c_conv_b, c_ln_g, c_ln_b, c_w_pw2, c_b_pw2, ffn_w_up, ffn_w_down):
    bsz, seq, d = x_prompt.shape
    dbsz, dseq, _ = x_sample.shape
    n_p, n_s = bsz * seq, dbsz * dseq
    depth = norm_mix_g.shape[0]
    chunk = a_w_s.shape[-1]
    assert seq % chunk == 0 and chunk % dseq == 0 and seq & (seq - 1) == 0 and dseq & (dseq - 1) == 0
    assert seq % _TM_MIX == 0 and n_s % _TM_MIX == 0 and seq % _TM_CONV == 0 and n_s % _TM_CONV == 0
    assert n_p % _TM == 0 and n_s % _TM == 0 and (n_p + n_s) % _TM_IN == 0

    x = jnp.concatenate([x_prompt.reshape(n_p, d), x_sample.reshape(n_s, d)], axis=0)
    a_v, b_prompt, b_sample, c_prompt, c_sample = [], [], [], [], []
    for i in range(depth):
        j, kind = divmod(i, _N_MIXERS)
        if kind == 0:
            z = _inproj(_gelu_epilogue, x, norm_mix_g, i, a_w_in, None, j, 0, 1, 2 * d, _TM_IN, _TN_IN, _BF16,
                        "inproj_gelu")
            ws_all, bias_all = _gating_operands(a_w_s[j], a_b_s[j], dseq, d)
            y, vn = _mix_gmlp(z, a_ln_g[j], a_ln_b[j], ws_all, bias_all, n_p, n_s)
            a_v.append(vn.reshape(dbsz, dseq, d))
            x = _outproj(y, j, a_w_out, None, x)
        elif kind == 1:
            bg, h = _inproj(_plain_epilogue, x, norm_mix_g, i, b_w_in, None, j, 0, 1, d, _TM_IN, _TN_IN, _BF16,
                            "inproj_gate", emit_h=True)
            cx = _inproj(_product_epilogue, h, None, None, b_w_in, None, j, 1, 2, d, _TM_IN, _TN_IN_PAIR, _F32,
                         "inproj_product")
            y, sample_state = _mix_conv3(bg, cx, state_b_conv, j, b_conv_w[j], n_p, seq, dseq)
            b_prompt.append(_prompt_conv_state(cx, bsz, seq, state_b_conv.shape[2]))
            b_sample.append(sample_state)
            x = _outproj(y, j, b_w_out, None, x)
        else:
            glu = _inproj(_glu_epilogue, x, norm_mix_g, i, c_w_pw1, c_b_pw1, j, 0, 2, d, _TM_IN, _TN_IN_PAIR, _F32,
                          "inproj_glu")
            y, sample_state = _mix_conv31(glu, state_c_conv, j, c_conv_w[j], c_conv_b[j], c_ln_g[j], c_ln_b[j],
                                          n_p, seq, dseq)
            c_prompt.append(_prompt_conv_state(glu, bsz, seq, state_c_conv.shape[2]))
            c_sample.append(sample_state)
            x = _outproj(y, j, c_w_pw2, c_b_pw2, x)
        if i + 1 < depth:
            x = _ffn(x, norm_ffn_g, i, ffn_w_up, ffn_w_down)
        else:
            y_prompt, y_sample = _ffn(x, norm_ffn_g, i, ffn_w_up, ffn_w_down, final_norm_g, n_p)
    return (y_prompt.reshape(bsz, seq, d), y_sample.reshape(dbsz, dseq, d), jnp.stack(a_v),
            jnp.stack(b_prompt), jnp.stack(b_sample), jnp.stack(c_prompt), jnp.stack(c_sample))
```

```python
import functools

import jax
import jax.numpy as jnp
from jax import lax
from jax.experimental import pallas as pl
from jax.experimental.pallas import tpu as pltpu

_F32 = jnp.float32
_BF16 = jnp.bfloat16
_EPS = 1e-6
_N_MIXERS = 3

_LANES = 128
_SUBLANES = 8
_VMEM_CAP_BYTES = 60000 * 1024
_VMEM_SLACK_BYTES = 12 * 1024 * 1024

_TM = 1024
_TM_IN = 768
_TN_IN = 1024
_TN_IN_PAIR = 512
_TM_MIX = 512
_TN_OUT = 1024
_TF = 1024
_FFN_PREFETCH_STEP = 2
_TM_CONV = 256
_ROWS = 16
_ROWS_IN_PLACE = 64
_ROW_BLOCKS_IN_FLIGHT = 8
_CONV_HALO = 32
_TAP_CHAINS = 2
_TAP_BLOCKS_IN_FLIGHT = 16


def _nbytes(shape, dtype):
    n = 1
    for s in shape:
        n *= s
    return n * jnp.dtype(dtype).itemsize


def _params(semantics, windows, scratch=0):
    limit = 2 * sum(_nbytes(s, d) for s, d in windows) + scratch + _VMEM_SLACK_BYTES
    return pltpu.CompilerParams(dimension_semantics=semantics, vmem_limit_bytes=min(limit, _VMEM_CAP_BYTES))


def _rms_rows(x, g):
    return x * lax.rsqrt(jnp.mean(x * x, axis=-1, keepdims=True) + _EPS) * g


def _ln_rows(x, g, b):
    xc = x - jnp.mean(x, axis=-1, keepdims=True)
    return xc * lax.rsqrt(jnp.mean(xc * xc, axis=-1, keepdims=True) + _EPS) * g + b


def _for_row_blocks(n_rows, fn, rows=_ROWS, unroll=_ROW_BLOCKS_IN_FLIGHT):
    def body(r, carry):
        fn(pl.ds(pl.multiple_of(r * rows, rows), rows))
        return carry
    lax.fori_loop(0, n_rows // rows, body, 0, unroll=unroll)


def _dot(a, b):
    return jnp.dot(a, b, preferred_element_type=_F32)


def _gelu_epilogue(accs, out_ref):
    (acc,) = accs
    out_ref[...] = (0.5 * acc * (1.0 + lax.erf(acc * 0.7071067811865476))).astype(out_ref.dtype)


def _plain_epilogue(accs, out_ref):
    (acc,) = accs
    out_ref[...] = acc.astype(out_ref.dtype)


def _product_epilogue(accs, out_ref):
    out_ref[...] = (accs[0] * accs[1]).astype(out_ref.dtype)


def _glu_epilogue(accs, out_ref):
    out_ref[...] = (accs[0] * jax.nn.sigmoid(accs[1])).astype(out_ref.dtype)


def _resident_weight_block(layer, first_block, nb):
    return lambda i, j: (layer, 0, first_block + jnp.where(i == 0, j, nb - 1))


def _resident_weights(w_refs, wbf_refs):
    j = pl.program_id(1)

    @pl.when(pl.program_id(0) == 0)
    def _():
        for w_ref, wbf_ref in zip(w_refs, wbf_refs):
            wbf_ref[j] = w_ref[...].astype(wbf_ref.dtype)
    return [wbf_ref[j] for wbf_ref in wbf_refs]


def _inproj_kernel(epilogue, parts, has_bias, normed, x_ref, *refs):
    if not normed:
        g_ref, refs = refs[0], refs[1:]
    w_refs, refs = refs[:parts], refs[parts:]
    b_refs, refs = (refs[:parts], refs[parts:]) if has_bias else ((), refs)
    out_ref, refs = refs[0], refs[1:]
    if normed:
        h_ref = x_ref
    else:
        h_ref, refs = refs[0], refs[1:]

        @pl.when(pl.program_id(1) == 0)
        def _():
            def norm(rows):
                h_ref[rows, :] = _rms_rows(x_ref[rows, :], g_ref[...]).astype(h_ref.dtype)
            _for_row_blocks(h_ref.shape[0], norm)

    ws = _resident_weights(w_refs, refs)
    h = h_ref[...]
    accs = [_dot(h, w) for w in ws]
    if has_bias:
        accs = [acc + b_ref[...] for acc, b_ref in zip(accs, b_refs)]
    epilogue(accs, out_ref)


def _inproj(epilogue, x, g, g_layer, w, bias, layer, first_part, parts, n, tm, tn, out_dtype, name, emit_h=False):
    m, k = x.shape
    nb = n // tn
    normed = g is None
    in_specs = [pl.BlockSpec((tm, k), lambda i, j: (i, 0))]
    operands = [x]
    if not normed:
        in_specs.append(pl.BlockSpec((None, 1, k), lambda i, j: (g_layer, 0, 0)))
        operands.append(g.reshape(g.shape[0], 1, k))
    for p in range(parts):
        in_specs.append(pl.BlockSpec((None, k, tn), _resident_weight_block(layer, (first_part + p) * nb, nb)))
        operands.append(w)
    if bias is not None:
        for p in range(parts):
            in_specs.append(pl.BlockSpec((None, 1, tn), functools.partial(
                lambda i, j, p: (layer, 0, (first_part + p) * nb + j), p=p)))
            operands.append(bias.reshape(bias.shape[0], 1, -1))
    windows = [((tm, k), x.dtype)] + [((k, tn), _F32)] * parts + [((tm, tn), out_dtype)]
    out_specs = [pl.BlockSpec((tm, tn), lambda i, j: (i, j))]
    out_shape = [jax.ShapeDtypeStruct((m, n), out_dtype)]
    scratch = [((nb, k, tn), _BF16)] * parts
    if emit_h:
        out_specs.append(pl.BlockSpec((tm, k), lambda i, j: (i, 0)))
        out_shape.append(jax.ShapeDtypeStruct((m, k), _BF16))
        windows.append(((tm, k), _BF16))
    elif not normed:
        scratch = [((tm, k), _BF16)] + scratch
    outs = pl.pallas_call(
        functools.partial(_inproj_kernel, epilogue, parts, bias is not None, normed),
        grid=(m // tm, nb),
        in_specs=in_specs,
        out_specs=out_specs,
        out_shape=out_shape,
        scratch_shapes=[pltpu.VMEM(s, dt) for s, dt in scratch],
        compiler_params=_params(("arbitrary", "arbitrary"), windows, sum(_nbytes(s, dt) for s, dt in scratch)),
        name=name,
    )(*operands)
    return outs if emit_h else outs[0]


def _mix_gmlp_kernel(groups, chunk, u_ref, v_ref, lng_ref, lnb_ref, ws_ref, bias_ref, y_ref, vn_ref, vnb_ref):
    tm, d = y_ref.shape
    gd = d // groups

    def normalise(rows):
        vn = _ln_rows(v_ref[rows, :].astype(_F32), lng_ref[...], lnb_ref[...])
        vn_ref[rows, :] = vn
        vnb_ref[rows, :] = vn.astype(vnb_ref.dtype)
    _for_row_blocks(tm, normalise)

    def mix_chunk(c, carry):
        rows = pl.ds(pl.multiple_of(c * chunk, chunk), chunk)
        for g in range(groups):
            cols = slice(g * gd, (g + 1) * gd)
            mixed = _dot(ws_ref[0, g], vnb_ref[rows, cols]) + bias_ref[0, :, cols]
            y_ref[rows, cols] = (u_ref[rows, cols].astype(_F32) * mixed).astype(y_ref.dtype)
        return carry
    lax.fori_loop(0, tm // chunk, mix_chunk, 0, unroll=True)


def _mix_gmlp(z, ln_g, ln_b, ws_all, bias_all, n_prompt, n_sample):
    m, d2 = z.shape
    d = d2 // 2
    tm = _TM_MIX
    n_pt = n_prompt // tm
    groups, chunk = ws_all.shape[1], ws_all.shape[2]
    s_tiles = n_sample // tm
    windows = [((tm, d), _BF16)] * 3 + [((tm, d), _F32), ((1, groups, chunk, chunk), _BF16), ((1, chunk, d), _F32)]
    y, vn = pl.pallas_call(
        functools.partial(_mix_gmlp_kernel, groups, chunk),
        grid=(m // tm,),
        in_specs=[
            pl.BlockSpec((tm, d), lambda i: (i, 0)),
            pl.BlockSpec((tm, d), lambda i: (i, 1)),
            pl.BlockSpec((1, d), lambda i: (0, 0)),
            pl.BlockSpec((1, d), lambda i: (0, 0)),
            pl.BlockSpec((1, groups, chunk, chunk), lambda i: (jnp.where(i < n_pt, 0, 1), 0, 0, 0)),
            pl.BlockSpec((1, chunk, d), lambda i: (jnp.where(i < n_pt, 0, 1), 0, 0)),
        ],
        out_specs=[
            pl.BlockSpec((tm, d), lambda i: (i, 0)),
            pl.BlockSpec((tm, d), lambda i: (jnp.maximum(i - n_pt, 0), 0)),
        ],
        out_shape=[jax.ShapeDtypeStruct((m, d), _BF16), jax.ShapeDtypeStruct((s_tiles * tm, d), _F32)],
        scratch_shapes=[pltpu.VMEM((tm, d), _BF16)],
        compiler_params=_params(("arbitrary",), windows, _nbytes((tm, d), _BF16)),
        name="mix_gmlp",
    )(z, z, ln_g.reshape(1, d), ln_b.reshape(1, d), ws_all, bias_all)
    return y, vn


def _mix_conv3_kernel(n_pt, seq_tiles, dseq, bg_ref, cx_ref, past_ref, cw_ref, y_ref, state_ref, ext_ref):
    i = pl.program_id(0)
    tm, d = y_ref.shape
    halo = _SUBLANES
    seq_start = lax.rem(i, seq_tiles) == 0
    keep = past_ref.shape[1]

    def conv(l, cols, w, sample, b, carry):
        r0 = pl.multiple_of(b * _ROWS, _ROWS)
        rows = pl.ds(r0, _ROWS)
        cur = ext_ref[l, pl.ds(r0 + halo, _ROWS), :]
        prev1 = ext_ref[l, pl.ds(r0 + halo - 1, _ROWS), :]
        prev2 = ext_ref[l, pl.ds(r0 + halo - 2, _ROWS), :]
        if sample:
            pos = lax.broadcasted_iota(jnp.int32, (_ROWS, 1), 0) & (dseq - 1)
            seqs = [b * (_ROWS // dseq) + q for q in range(_ROWS // dseq)]

            def past_row(k):
                return jnp.concatenate(
                    [jnp.broadcast_to(past_ref[s, pl.ds(k, 1), cols], (dseq, _LANES)) for s in seqs], axis=0)
            prev1 = jnp.where(pos >= 1, prev1, past_row(1))
            prev2 = jnp.where(pos >= 2, prev2, jnp.where(pos == 0, past_row(0), past_row(1)))
            for q, s in enumerate(seqs):
                state_ref[s, :, cols] = ext_ref[l, pl.ds(r0 + halo + (q + 1) * dseq - keep, keep), :]
        acc = w[0] * prev2 + w[1] * prev1 + w[2] * cur
        y_ref[rows, cols] = (bg_ref[rows, cols].astype(_F32) * acc).astype(y_ref.dtype)
        return carry

    def lane_block(l, carry):
        cols = pl.ds(pl.multiple_of(l * _LANES, _LANES), _LANES)

        @pl.when(seq_start)
        def _():
            ext_ref[l, 0:halo, :] = jnp.zeros((halo, _LANES), _F32)

        @pl.when(jnp.logical_not(seq_start))
        def _():
            ext_ref[l, 0:halo, :] = ext_ref[l, tm:tm + halo, :]

        ext_ref[l, halo:halo + tm, :] = cx_ref[:, cols]
        w = [jnp.broadcast_to(cw_ref[k:k + 1, cols], (_ROWS, _LANES)) for k in range(3)]

        @pl.when(i < n_pt)
        def _():
            lax.fori_loop(0, tm // _ROWS, functools.partial(conv, l, cols, w, False), 0,
                          unroll=_ROW_BLOCKS_IN_FLIGHT)

        @pl.when(i >= n_pt)
        def _():
            lax.fori_loop(0, tm // _ROWS, functools.partial(conv, l, cols, w, True), 0,
                          unroll=_ROW_BLOCKS_IN_FLIGHT)
        return carry
    lax.fori_loop(0, d // _LANES, lane_block, 0)


def _mix_conv3(bg, cx, past, layer, conv_w, n_prompt, seq, dseq):
    m, d = cx.shape
    keep = past.shape[2]
    assert conv_w.shape[0] == keep + 1 == 3 and keep <= dseq and _ROWS % dseq == 0
    tm = _TM_MIX
    n_pt = n_prompt // tm
    n_seq = tm // dseq
    windows = [((tm, d), _BF16)] * 2 + [((tm, d), _F32)] + [((n_seq, _SUBLANES, d), _F32)] * 2
    seq_block = lambda i: jnp.maximum(i - n_pt, 0)
    return pl.pallas_call(
        functools.partial(_mix_conv3_kernel, n_pt, seq // tm, dseq),
        grid=(m // tm,),
        in_specs=[
            pl.BlockSpec((tm, d), lambda i: (i, 0)),
            pl.BlockSpec((tm, d), lambda i: (i, 0)),
            pl.BlockSpec((None, n_seq, keep, d), lambda i: (layer, seq_block(i), 0, 0)),
            pl.BlockSpec(conv_w.shape, lambda i: (0, 0)),
        ],
        out_specs=[pl.BlockSpec((tm, d), lambda i: (i, 0)),
                   pl.BlockSpec((n_seq, keep, d), lambda i: (seq_block(i), 0, 0))],
        out_shape=[jax.ShapeDtypeStruct((m, d), _BF16), jax.ShapeDtypeStruct((past.shape[1], keep, d), _F32)],
        scratch_shapes=[pltpu.VMEM((d // _LANES, tm + _SUBLANES, _LANES), _F32)],
        compiler_params=_params(("arbitrary",), windows, _nbytes((tm + _SUBLANES, d), _F32)),
        name="mix_conv3",
    )(bg, cx, past, conv_w)


def _depthwise_taps(width, fill, slab_of, ext_ref, cw_ref, c_ref, n_blocks, in_row, out_row):
    d = c_ref.shape[1]

    def lane_block(l, carry):
        cols = pl.ds(pl.multiple_of(l * _LANES, _LANES), _LANES)
        fill(l, cols)
        slab = slab_of(l)
        w = [jnp.broadcast_to(cw_ref[k:k + 1, cols], (_SUBLANES, _LANES)) for k in range(width)]

        def row_block(b, carry2):
            base = in_row(b)
            partial = [None] * _TAP_CHAINS
            for k in range(width):
                term = w[k] * ext_ref[slab, pl.ds(base + k, _SUBLANES), :]
                c = k % _TAP_CHAINS
                partial[c] = term if partial[c] is None else partial[c] + term
            while len(partial) > 1:
                partial = [a + b_ for a, b_ in zip(partial[0::2], partial[1::2])] + (
                    [partial[-1]] if len(partial) % 2 else [])
            c_ref[pl.ds(pl.multiple_of(out_row(b), _SUBLANES), _SUBLANES), cols] = partial[0]
            return carry2
        lax.fori_loop(0, n_blocks, row_block, 0, unroll=_TAP_BLOCKS_IN_FLIGHT)
        return carry
    lax.fori_loop(0, d // _LANES, lane_block, 0)


def _conformer_tail(c_ref, cb_ref, lng_ref, lnb_ref, y_ref):
    def tail(rows):
        c = _ln_rows(c_ref[rows, :] + cb_ref[...], lng_ref[...], lnb_ref[...])
        y_ref[rows, :] = (c * jax.nn.sigmoid(c)).astype(y_ref.dtype)
    _for_row_blocks(y_ref.shape[0], tail)


def _mix_conv31_kernel(width, n_pt, seq_tiles, dseq, glu_ref, past_ref, cw_ref, cb_ref, lng_ref, lnb_ref, y_ref,
                       state_ref, ext_ref, sext_ref, c_ref):
    i = pl.program_id(0)
    tm, d = y_ref.shape
    halo = _CONV_HALO
    lead = halo - (width - 1)

    @pl.when(i < n_pt)
    def _():
        def fill(l, cols):
            @pl.when(lax.rem(i, seq_tiles) == 0)
            def _():
                ext_ref[l, 0:halo, :] = jnp.zeros((halo, _LANES), _F32)

            @pl.when(lax.rem(i, seq_tiles) != 0)
            def _():
                ext_ref[l, 0:halo, :] = ext_ref[l, tm:tm + halo, :]

            ext_ref[l, halo:halo + tm, :] = glu_ref[:, cols]

        _depthwise_taps(width, fill, lambda l: l, ext_ref, cw_ref, c_ref, tm // _SUBLANES,
                        lambda b: b * _SUBLANES + lead, lambda b: b * _SUBLANES)

    @pl.when(i >= n_pt)
    def _():
        keep = width - 1

        def lane_block(l, carry):
            cols = pl.ds(pl.multiple_of(l * _LANES, _LANES), _LANES)
            sext_ref[0, 0:tm, :] = glu_ref[:, cols]
            w = [jnp.broadcast_to(cw_ref[k:k + 1, cols], (_SUBLANES, _LANES)) for k in range(width)]

            def sequences(q, carry2):
                s0 = pl.multiple_of(q * _SUBLANES, _SUBLANES)

                def conv_row(m):
                    if m < keep:
                        return past_ref[m, pl.ds(s0, _SUBLANES), cols]
                    return sext_ref[0, pl.ds(s0 * dseq + (m - keep), _SUBLANES, stride=dseq), :]

                for t in range(dseq):
                    partial = [None] * _TAP_CHAINS
                    for k in range(width):
                        term = w[k] * conv_row(t + k)
                        c = k % _TAP_CHAINS
                        partial[c] = term if partial[c] is None else partial[c] + term
                    acc = partial[0]
                    for p in partial[1:]:
                        acc = acc + p
                    sext_ref[0, pl.ds(tm + s0 * dseq + t, _SUBLANES, stride=dseq), :] = acc
                for m in range(keep):
                    state_ref[m, pl.ds(s0, _SUBLANES), cols] = conv_row(m + dseq)
                return carry2
            lax.fori_loop(0, tm // dseq // _SUBLANES, sequences, 0)
            c_ref[:, cols] = sext_ref[0, tm:2 * tm, :]
            return carry
        lax.fori_loop(0, d // _LANES, lane_block, 0)

    _conformer_tail(c_ref, cb_ref, lng_ref, lnb_ref, y_ref)


def _mix_conv31(glu, past, layer, conv_w, conv_b, ln_g, ln_b, n_prompt, seq, dseq):
    m, d = glu.shape
    width = conv_w.shape[0]
    keep = width - 1
    assert keep <= _CONV_HALO and dseq == _SUBLANES
    tm = _TM_CONV
    n_pt = n_prompt // tm
    n_seq = tm // dseq
    assert n_seq % _SUBLANES == 0
    row = pl.BlockSpec((1, d), lambda i: (0, 0))
    scratch = [((d // _LANES, tm + _CONV_HALO, _LANES), _F32), ((1, 2 * tm, _LANES), _F32), ((tm, d), _F32)]
    windows = [((tm, d), _F32), ((keep, n_seq, d), _F32), ((tm, d), _BF16), ((keep, n_seq, d), _F32)]
    seq_block = lambda i: jnp.maximum(i - n_pt, 0)
    y, state = pl.pallas_call(
        functools.partial(_mix_conv31_kernel, width, n_pt, seq // tm, dseq),
        grid=(m // tm,),
        in_specs=[pl.BlockSpec((tm, d), lambda i: (i, 0)),
                  pl.BlockSpec((None, keep, n_seq, d), lambda i: (layer, 0, seq_block(i), 0)),
                  pl.BlockSpec(conv_w.shape, lambda i: (0, 0)), row, row, row],
        out_specs=[pl.BlockSpec((tm, d), lambda i: (i, 0)),
                   pl.BlockSpec((keep, n_seq, d), lambda i: (0, seq_block(i), 0))],
        out_shape=[jax.ShapeDtypeStruct((m, d), _BF16), jax.ShapeDtypeStruct((keep, past.shape[1], d), _F32)],
        scratch_shapes=[pltpu.VMEM(s, dt) for s, dt in scratch],
        compiler_params=_params(("arbitrary",), windows, sum(_nbytes(s, dt) for s, dt in scratch)),
        name="mix_conv31",
    )(glu, jnp.swapaxes(past, 1, 2), conv_w, conv_b.reshape(1, d), ln_g.reshape(1, d), ln_b.reshape(1, d))
    return y, jnp.swapaxes(state, 0, 1)


def _outproj_kernel(has_bias, y_ref, w_ref, *refs):
    if has_bias:
        b_ref, x_ref, xo_ref, wbf_ref = refs
    else:
        x_ref, xo_ref, wbf_ref = refs
    (w,) = _resident_weights([w_ref], [wbf_ref])
    acc = _dot(y_ref[...], w)
    if has_bias:
        acc = acc + b_ref[...]
    xo_ref[...] = x_ref[...] + acc


def _outproj(y, layer, w, bias, x):
    m, k = y.shape
    n = w.shape[2]
    tm, tn = _TM, _TN_OUT
    nb = n // tn
    in_specs = [pl.BlockSpec((tm, k), lambda i, j: (i, 0)),
                pl.BlockSpec((None, k, tn), _resident_weight_block(layer, 0, nb))]
    operands = [y, w]
    if bias is not None:
        in_specs.append(pl.BlockSpec((None, 1, tn), lambda i, j: (layer, 0, j)))
        operands.append(bias.reshape(bias.shape[0], 1, n))
    in_specs.append(pl.BlockSpec((tm, tn), lambda i, j: (i, j)))
    operands.append(x)
    windows = [((tm, k), _BF16), ((k, tn), _F32), ((tm, tn), _F32), ((tm, tn), _F32)]
    return pl.pallas_call(
        functools.partial(_outproj_kernel, bias is not None),
        grid=(m // tm, nb),
        in_specs=in_specs,
        out_specs=pl.BlockSpec((tm, tn), lambda i, j: (i, j)),
        out_shape=jax.ShapeDtypeStruct((m, n), _F32),
        scratch_shapes=[pltpu.VMEM((nb, k, tn), _BF16)],
        compiler_params=_params(("arbitrary", "arbitrary"), windows, _nbytes((nb, k, tn), _BF16)),
        name="outproj",
    )(*operands)


def _ffn_kernel(final, split_tile, x_hbm, g_ref, wu_ref, wd_ref, *refs):
    if final:
        gf_ref, o_head_hbm, o_tail_hbm, acc_ref, h_ref, in_sem, out_sem = refs
    else:
        o_head_hbm, acc_ref, h_ref, in_sem, out_sem = refs
    i, f = pl.program_id(0), pl.program_id(1)
    n_tiles, n_steps = pl.num_programs(0), pl.num_programs(1)
    tm = acc_ref.shape[1]
    slot = lax.rem(i, 2)

    def x_copy(tile, s):
        return pltpu.make_async_copy(x_hbm.at[pl.ds(tile * tm, tm)], acc_ref.at[s], in_sem.at[s])

    def o_copy(tile, s):
        def for_each(act):
            if final:
                @pl.when(tile < split_tile)
                def _():
                    act(pltpu.make_async_copy(acc_ref.at[s], o_head_hbm.at[pl.ds(tile * tm, tm)], out_sem.at[s]))

                @pl.when(tile >= split_tile)
                def _():
                    act(pltpu.make_async_copy(acc_ref.at[s], o_tail_hbm.at[pl.ds((tile - split_tile) * tm, tm)],
                                              out_sem.at[s]))
            else:
                act(pltpu.make_async_copy(acc_ref.at[s], o_head_hbm.at[pl.ds(tile * tm, tm)], out_sem.at[s]))
        return for_each

    def start(copy):
        copy.start()

    def wait(copy):
        copy.wait()

    @pl.when(f == 0)
    def _():
        @pl.when(i == 0)
        def _():
            x_copy(0, 0).start()
        x_copy(i, slot).wait()

        def norm(rows):
            h_ref[rows, :] = _rms_rows(acc_ref[slot, rows, :], g_ref[...]).astype(h_ref.dtype)
        _for_row_blocks(tm, norm)

    @pl.when(jnp.logical_and(f == _FFN_PREFETCH_STEP, i + 1 < n_tiles))
    def _():
        @pl.when(i >= 1)
        def _():
            o_copy(i - 1, 1 - slot)(wait)
        x_copy(i + 1, 1 - slot).start()

    a = jnp.maximum(_dot(h_ref[...], wu_ref[...].astype(_BF16)), 0.0)
    acc_ref[slot] += _dot((a * a).astype(_BF16), wd_ref[...].astype(_BF16))

    @pl.when(f == n_steps - 1)
    def _():
        if final:
            def finish(rows):
                acc_ref[slot, rows, :] = _rms_rows(acc_ref[slot, rows, :], gf_ref[...])
            _for_row_blocks(tm, finish, rows=_ROWS_IN_PLACE, unroll=1)
        o_copy(i, slot)(start)

        @pl.when(i == n_tiles - 1)
        def _():
            o_copy(i, slot)(wait)

            @pl.when(i >= 1)
            def _():
                o_copy(i - 1, 1 - slot)(wait)


def _ffn(x, g, layer, w_up, w_down, g_final=None, split=None):
    rows, d = x.shape
    dff = w_up.shape[2]
    tm, tf = _TM, _TF
    assert dff // tf > _FFN_PREFETCH_STEP >= 1 and rows % tm == 0
    final = g_final is not None
    if final:
        assert split % tm == 0
        out_shape = [jax.ShapeDtypeStruct((split, d), _F32), jax.ShapeDtypeStruct((rows - split, d), _F32)]
    else:
        out_shape = [jax.ShapeDtypeStruct((rows, d), _F32)]
    in_specs = [pl.BlockSpec(memory_space=pl.ANY),
                pl.BlockSpec((None, 1, d), lambda i, f: (layer, 0, 0)),
                pl.BlockSpec((None, d, tf), lambda i, f: (layer, 0, f)),
                pl.BlockSpec((None, tf, d), lambda i, f: (layer, f, 0))]
    operands = [x, g.reshape(g.shape[0], 1, d), w_up, w_down]
    if final:
        in_specs.append(pl.BlockSpec((1, d), lambda i, f: (0, 0)))
        operands.append(g_final.reshape(1, d))
    windows = [((d, tf), _F32), ((tf, d), _F32)]
    scratch = [((2, tm, d), _F32), ((tm, d), _BF16)]
    outs = pl.pallas_call(
        functools.partial(_ffn_kernel, final, split // tm if final else None),
        grid=(rows // tm, dff // tf),
        in_specs=in_specs,
        out_specs=[pl.BlockSpec(memory_space=pl.ANY)] * len(out_shape),
        out_shape=out_shape,
        scratch_shapes=[pltpu.VMEM(s, dt) for s, dt in scratch]
                       + [pltpu.SemaphoreType.DMA((2,)), pltpu.SemaphoreType.DMA((2,))],
        compiler_params=_params(("arbitrary", "arbitrary"), windows, sum(_nbytes(s, dt) for s, dt in scratch)),
        name="ffn",
    )(*operands)
    return outs if final else outs[0]


def _gating_operands(w_s, b_s, dseq, d):
    groups, chunk, _ = w_s.shape
    t = jnp.arange(chunk)
    causal = t[:, None] >= t[None, :]
    same_seq = (t[:, None] // dseq) == (t[None, :] // dseq)
    reps = chunk // dseq
    ws_prompt = jnp.where(causal, w_s, 0)
    ws_sample = jnp.where(causal & same_seq, jnp.tile(w_s[:, :dseq, :dseq], (1, reps, reps)), 0)
    gd = d // groups
    bias_prompt = jnp.repeat(b_s.T, gd, axis=1)
    bias_sample = jnp.repeat(jnp.tile(b_s[:, :dseq].T, (reps, 1)), gd, axis=1)
    return jnp.stack([ws_prompt, ws_sample]).astype(_BF16), jnp.stack([bias_prompt, bias_sample])


def _prompt_conv_state(t, bsz, seq, keep):
    return jnp.stack([lax.slice_in_dim(t, (b + 1) * seq - keep, (b + 1) * seq, axis=0) for b in range(bsz)])


def kernel(x_prompt, x_sample, state_b_conv, state_c_conv, norm_mix_g, norm_ffn_g, final_norm_g, a_w_in, a_ln_g, a_ln_b, a_w_s, a_b_s, a_w_out, b_w_in, b_conv_w, b_w_out, c_w_pw1, c_b_pw1, c_conv_w, c_conv_b, c_ln_g, c_ln_b, c_w_pw2, c_b_pw2, ffn_w_up, ffn_w_down):
    bsz, seq, d = x_prompt.shape
    dbsz, dseq, _ = x_sample.shape
    n_p, n_s = bsz * seq, dbsz * dseq
    depth = norm_mix_g.shape[0]
    chunk = a_w_s.shape[-1]
    assert seq % chunk == 0 and chunk % dseq == 0 and seq & (seq - 1) == 0 and dseq & (dseq - 1) == 0
    assert seq % _TM_MIX == 0 and n_s % _TM_MIX == 0 and seq % _TM_CONV == 0 and n_s % _TM_CONV == 0
    assert n_p % _TM == 0 and n_s % _TM == 0 and (n_p + n_s) % _TM_IN == 0

    x = jnp.concatenate([x_prompt.reshape(n_p, d), x_sample.reshape(n_s, d)], axis=0)
    a_v, b_prompt, b_sample, c_prompt, c_sample = [], [], [], [], []
    for i in range(depth):
        j, kind = divmod(i, _N_MIXERS)
        if kind == 0:
            z = _inproj(_gelu_epilogue, x, norm_mix_g, i, a_w_in, None, j, 0, 1, 2 * d, _TM_IN, _TN_IN, _BF16,
                        "inproj_gelu")
            ws_all, bias_all = _gating_operands(a_w_s[j], a_b_s[j], dseq, d)
            y, vn = _mix_gmlp(z, a_ln_g[j], a_ln_b[j], ws_all, bias_all, n_p, n_s)
            a_v.append(vn.reshape(dbsz, dseq, d))
            x = _outproj(y, j, a_w_out, None, x)
        elif kind == 1:
            bg, h = _inproj(_plain_epilogue, x, norm_mix_g, i, b_w_in, None, j, 0, 1, d, _TM_IN, _TN_IN, _BF16,
                            "inproj_gate", emit_h=True)
            cx = _inproj(_product_epilogue, h, None, None, b_w_in, None, j, 1, 2, d, _TM_IN, _TN_IN_PAIR, _F32,
                         "inproj_product")
            y, sample_state = _mix_conv3(bg, cx, state_b_conv, j, b_conv_w[j], n_p, seq, dseq)
            b_prompt.append(_prompt_conv_state(cx, bsz, seq, state_b_conv.shape[2]))
            b_sample.append(sample_state)
            x = _outproj(y, j, b_w_out, None, x)
        else:
            glu = _inproj(_glu_epilogue, x, norm_mix_g, i, c_w_pw1, c_b_pw1, j, 0, 2, d, _TM_IN, _TN_IN_PAIR, _F32,
                          "inproj_glu")
            y, sample_state = _mix_conv31(glu, state_c_conv, j, c_conv_w[j], c_conv_b[j], c_ln_g[j], c_ln_b[j],
                                          n_p, seq, dseq)
            c_prompt.append(_prompt_conv_state(glu, bsz, seq, state_c_conv.shape[2]))
            c_sample.append(sample_state)
            x = _outproj(y, j, c_w_pw2, c_b_pw2, x)
        if i + 1 < depth:
            x = _ffn(x, norm_ffn_g, i, ffn_w_up, ffn_w_down)
        else:
            y_prompt, y_sample = _ffn(x, norm_ffn_g, i, ffn_w_up, ffn_w_down, final_norm_g, n_p)
    return (y_prompt.reshape(bsz, seq, d), y_sample.reshape(dbsz, dseq, d), jnp.stack(a_v),
            jnp.stack(b_prompt), jnp.stack(b_sample), jnp.stack(c_prompt), jnp.stack(c_sample))
```

```python
import functools

import jax
import jax.numpy as jnp
from jax import lax
from jax.experimental import pallas as pl
from jax.experimental.pallas import tpu as pltpu

_F32 = jnp.float32
_BF16 = jnp.bfloat16
_EPS = 1e-6
_N_MIXERS = 3

_LANES = 128
_SUBLANES = 8
_VMEM_CAP_BYTES = 60000 * 1024
_VMEM_SLACK_BYTES = 12 * 1024 * 1024

_TM = 1024
_TM_IN = 768
_TN_IN = 1024
_TN_IN_PAIR = 512
_TM_MIX = 512
_TN_OUT = 1024
_TF = 1024
_FFN_PREFETCH_STEP = 2
_TM_CONV = 256
_ROWS = 16
_ROWS_IN_PLACE = 64
_ROW_BLOCKS_IN_FLIGHT = 16
_CONV_HALO = 32
_TAP_CHAINS = 2
_TAP_BLOCKS_IN_FLIGHT = 16


def _nbytes(shape, dtype):
    n = 1
    for s in shape:
        n *= s
    return n * jnp.dtype(dtype).itemsize


def _params(semantics, windows, scratch=0):
    limit = 2 * sum(_nbytes(s, d) for s, d in windows) + scratch + _VMEM_SLACK_BYTES
    return pltpu.CompilerParams(dimension_semantics=semantics, vmem_limit_bytes=min(limit, _VMEM_CAP_BYTES))


def _rms_rows(x, g):
    return x * lax.rsqrt(jnp.mean(x * x, axis=-1, keepdims=True) + _EPS) * g


def _ln_rows(x, g, b):
    xc = x - jnp.mean(x, axis=-1, keepdims=True)
    return xc * lax.rsqrt(jnp.mean(xc * xc, axis=-1, keepdims=True) + _EPS) * g + b


def _for_row_blocks(n_rows, fn, rows=_ROWS, unroll=_ROW_BLOCKS_IN_FLIGHT):
    def body(r, carry):
        fn(pl.ds(pl.multiple_of(r * rows, rows), rows))
        return carry
    lax.fori_loop(0, n_rows // rows, body, 0, unroll=unroll)


def _dot(a, b):
    return jnp.dot(a, b, preferred_element_type=_F32)


def _gelu_epilogue(accs, out_ref):
    (acc,) = accs
    out_ref[...] = (0.5 * acc * (1.0 + lax.erf(acc * 0.7071067811865476))).astype(out_ref.dtype)


def _plain_epilogue(accs, out_ref):
    (acc,) = accs
    out_ref[...] = acc.astype(out_ref.dtype)


def _product_epilogue(accs, out_ref):
    out_ref[...] = (accs[0] * accs[1]).astype(out_ref.dtype)


def _glu_epilogue(accs, out_ref):
    out_ref[...] = (accs[0] * jax.nn.sigmoid(accs[1])).astype(out_ref.dtype)


def _resident_weight_block(layer, first_block, nb):
    return lambda i, j: (layer, 0, first_block + jnp.where(i == 0, j, nb - 1))


def _resident_weights(w_refs, wbf_refs):
    j = pl.program_id(1)

    @pl.when(pl.program_id(0) == 0)
    def _():
        for w_ref, wbf_ref in zip(w_refs, wbf_refs):
            wbf_ref[j] = w_ref[...].astype(wbf_ref.dtype)
    return [wbf_ref[j] for wbf_ref in wbf_refs]


def _inproj_kernel(epilogue, parts, has_bias, normed, x_ref, *refs):
    if not normed:
        g_ref, refs = refs[0], refs[1:]
    w_refs, refs = refs[:parts], refs[parts:]
    b_refs, refs = (refs[:parts], refs[parts:]) if has_bias else ((), refs)
    out_ref, refs = refs[0], refs[1:]
    if normed:
        h_ref = x_ref
    else:
        h_ref, refs = refs[0], refs[1:]

        @pl.when(pl.program_id(1) == 0)
        def _():
            def norm(rows):
                h_ref[rows, :] = _rms_rows(x_ref[rows, :], g_ref[...]).astype(h_ref.dtype)
            _for_row_blocks(h_ref.shape[0], norm)

    ws = _resident_weights(w_refs, refs)
    h = h_ref[...]
    accs = [_dot(h, w) for w in ws]
    if has_bias:
        accs = [acc + b_ref[...] for acc, b_ref in zip(accs, b_refs)]
    epilogue(accs, out_ref)


def _inproj(epilogue, x, g, g_layer, w, bias, layer, first_part, parts, n, tm, tn, out_dtype, name, emit_h=False):
    m, k = x.shape
    nb = n // tn
    normed = g is None
    in_specs = [pl.BlockSpec((tm, k), lambda i, j: (i, 0))]
    operands = [x]
    if not normed:
        in_specs.append(pl.BlockSpec((None, 1, k), lambda i, j: (g_layer, 0, 0)))
        operands.append(g.reshape(g.shape[0], 1, k))
    for p in range(parts):
        in_specs.append(pl.BlockSpec((None, k, tn), _resident_weight_block(layer, (first_part + p) * nb, nb)))
        operands.append(w)
    if bias is not None:
        for p in range(parts):
            in_specs.append(pl.BlockSpec((None, 1, tn), functools.partial(
                lambda i, j, p: (layer, 0, (first_part + p) * nb + j), p=p)))
            operands.append(bias.reshape(bias.shape[0], 1, -1))
    windows = [((tm, k), x.dtype)] + [((k, tn), _F32)] * parts + [((tm, tn), out_dtype)]
    out_specs = [pl.BlockSpec((tm, tn), lambda i, j: (i, j))]
    out_shape = [jax.ShapeDtypeStruct((m, n), out_dtype)]
    scratch = [((nb, k, tn), _BF16)] * parts
    if emit_h:
        out_specs.append(pl.BlockSpec((tm, k), lambda i, j: (i, 0)))
        out_shape.append(jax.ShapeDtypeStruct((m, k), _BF16))
        windows.append(((tm, k), _BF16))
    elif not normed:
        scratch = [((tm, k), _BF16)] + scratch
    outs = pl.pallas_call(
        functools.partial(_inproj_kernel, epilogue, parts, bias is not None, normed),
        grid=(m // tm, nb),
        in_specs=in_specs,
        out_specs=out_specs,
        out_shape=out_shape,
        scratch_shapes=[pltpu.VMEM(s, dt) for s, dt in scratch],
        compiler_params=_params(("arbitrary", "arbitrary"), windows, sum(_nbytes(s, dt) for s, dt in scratch)),
        name=name,
    )(*operands)
    return outs if emit_h else outs[0]


def _mix_gmlp_kernel(groups, chunk, u_ref, v_ref, lng_ref, lnb_ref, ws_ref, bias_ref, y_ref, vn_ref, vnb_ref):
    tm, d = y_ref.shape
    gd = d // groups

    def normalise(rows):
        vn = _ln_rows(v_ref[rows, :].astype(_F32), lng_ref[...], lnb_ref[...])
        vn_ref[rows, :] = vn
        vnb_ref[rows, :] = vn.astype(vnb_ref.dtype)
    _for_row_blocks(tm, normalise)

    def mix_chunk(c, carry):
        rows = pl.ds(pl.multiple_of(c * chunk, chunk), chunk)
        for g in range(groups):
            cols = slice(g * gd, (g + 1) * gd)
            mixed = _dot(ws_ref[0, g], vnb_ref[rows, cols]) + bias_ref[0, :, cols]
            y_ref[rows, cols] = (u_ref[rows, cols].astype(_F32) * mixed).astype(y_ref.dtype)
        return carry
    lax.fori_loop(0, tm // chunk, mix_chunk, 0, unroll=True)


def _mix_gmlp(z, ln_g, ln_b, ws_all, bias_all, n_prompt, n_sample):
    m, d2 = z.shape
    d = d2 // 2
    tm = _TM_MIX
    n_pt = n_prompt // tm
    groups, chunk = ws_all.shape[1], ws_all.shape[2]
    s_tiles = n_sample // tm
    windows = [((tm, d), _BF16)] * 3 + [((tm, d), _F32), ((1, groups, chunk, chunk), _BF16), ((1, chunk, d), _F32)]
    y, vn = pl.pallas_call(
        functools.partial(_mix_gmlp_kernel, groups, chunk),
        grid=(m // tm,),
        in_specs=[
            pl.BlockSpec((tm, d), lambda i: (i, 0)),
            pl.BlockSpec((tm, d), lambda i: (i, 1)),
            pl.BlockSpec((1, d), lambda i: (0, 0)),
            pl.BlockSpec((1, d), lambda i: (0, 0)),
            pl.BlockSpec((1, groups, chunk, chunk), lambda i: (jnp.where(i < n_pt, 0, 1), 0, 0, 0)),
            pl.BlockSpec((1, chunk, d), lambda i: (jnp.where(i < n_pt, 0, 1), 0, 0)),
        ],
        out_specs=[
            pl.BlockSpec((tm, d), lambda i: (i, 0)),
            pl.BlockSpec((tm, d), lambda i: (jnp.maximum(i - n_pt, 0), 0)),
        ],
        out_shape=[jax.ShapeDtypeStruct((m, d), _BF16), jax.ShapeDtypeStruct((s_tiles * tm, d), _F32)],
        scratch_shapes=[pltpu.VMEM((tm, d), _BF16)],
        compiler_params=_params(("arbitrary",), windows, _nbytes((tm, d), _BF16)),
        name="mix_gmlp",
    )(z, z, ln_g.reshape(1, d), ln_b.reshape(1, d), ws_all, bias_all)
    return y, vn


def _mix_conv3_kernel(n_pt, seq_tiles, dseq, bg_ref, cx_ref, past_ref, cw_ref, y_ref, state_ref, ext_ref):
    i = pl.program_id(0)
    tm, d = y_ref.shape
    halo = _SUBLANES
    seq_start = lax.rem(i, seq_tiles) == 0
    keep = past_ref.shape[1]

    def conv(l, cols, w, sample, b, carry):
        r0 = pl.multiple_of(b * _ROWS, _ROWS)
        rows = pl.ds(r0, _ROWS)
        cur = ext_ref[l, pl.ds(r0 + halo, _ROWS), :]
        prev1 = ext_ref[l, pl.ds(r0 + halo - 1, _ROWS), :]
        prev2 = ext_ref[l, pl.ds(r0 + halo - 2, _ROWS), :]
        if sample:
            pos = lax.broadcasted_iota(jnp.int32, (_ROWS, 1), 0) & (dseq - 1)
            seqs = [b * (_ROWS // dseq) + q for q in range(_ROWS // dseq)]

            def past_row(k):
                return jnp.concatenate(
                    [jnp.broadcast_to(past_ref[s, pl.ds(k, 1), cols], (dseq, _LANES)) for s in seqs], axis=0)
            prev1 = jnp.where(pos >= 1, prev1, past_row(1))
            prev2 = jnp.where(pos >= 2, prev2, jnp.where(pos == 0, past_row(0), past_row(1)))
            for q, s in enumerate(seqs):
                state_ref[s, :, cols] = ext_ref[l, pl.ds(r0 + halo + (q + 1) * dseq - keep, keep), :]
        acc = w[0] * prev2 + w[1] * prev1 + w[2] * cur
        y_ref[rows, cols] = (bg_ref[rows, cols].astype(_F32) * acc).astype(y_ref.dtype)
        return carry

    def lane_block(l, carry):
        cols = pl.ds(pl.multiple_of(l * _LANES, _LANES), _LANES)

        @pl.when(seq_start)
        def _():
            ext_ref[l, 0:halo, :] = jnp.zeros((halo, _LANES), _F32)

        @pl.when(jnp.logical_not(seq_start))
        def _():
            ext_ref[l, 0:halo, :] = ext_ref[l, tm:tm + halo, :]

        ext_ref[l, halo:halo + tm, :] = cx_ref[:, cols]
        w = [jnp.broadcast_to(cw_ref[k:k + 1, cols], (_ROWS, _LANES)) for k in range(3)]

        @pl.when(i < n_pt)
        def _():
            lax.fori_loop(0, tm // _ROWS, functools.partial(conv, l, cols, w, False), 0,
                          unroll=_ROW_BLOCKS_IN_FLIGHT)

        @pl.when(i >= n_pt)
        def _():
            lax.fori_loop(0, tm // _ROWS, functools.partial(conv, l, cols, w, True), 0,
                          unroll=_ROW_BLOCKS_IN_FLIGHT)
        return carry
    lax.fori_loop(0, d // _LANES, lane_block, 0)


def _mix_conv3(bg, cx, past, layer, conv_w, n_prompt, seq, dseq):
    m, d = cx.shape
    keep = past.shape[2]
    assert conv_w.shape[0] == keep + 1 == 3 and keep <= dseq and _ROWS % dseq == 0
    tm = _TM_MIX
    n_pt = n_prompt // tm
    n_seq = tm // dseq
    windows = [((tm, d), _BF16)] * 2 + [((tm, d), _F32)] + [((n_seq, _SUBLANES, d), _F32)] * 2
    seq_block = lambda i: jnp.maximum(i - n_pt, 0)
    return pl.pallas_call(
        functools.partial(_mix_conv3_kernel, n_pt, seq // tm, dseq),
        grid=(m // tm,),
        in_specs=[
            pl.BlockSpec((tm, d), lambda i: (i, 0)),
            pl.BlockSpec((tm, d), lambda i: (i, 0)),
            pl.BlockSpec((None, n_seq, keep, d), lambda i: (layer, seq_block(i), 0, 0)),
            pl.BlockSpec(conv_w.shape, lambda i: (0, 0)),
        ],
        out_specs=[pl.BlockSpec((tm, d), lambda i: (i, 0)),
                   pl.BlockSpec((n_seq, keep, d), lambda i: (seq_block(i), 0, 0))],
        out_shape=[jax.ShapeDtypeStruct((m, d), _BF16), jax.ShapeDtypeStruct((past.shape[1], keep, d), _F32)],
        scratch_shapes=[pltpu.VMEM((d // _LANES, tm + _SUBLANES, _LANES), _F32)],
        compiler_params=_params(("arbitrary",), windows, _nbytes((tm + _SUBLANES, d), _F32)),
        name="mix_conv3",
    )(bg, cx, past, conv_w)


def _depthwise_taps(width, fill, slab_of, ext_ref, cw_ref, c_ref, n_blocks, in_row, out_row):
    d = c_ref.shape[1]

    def lane_block(l, carry):
        cols = pl.ds(pl.multiple_of(l * _LANES, _LANES), _LANES)
        fill(l, cols)
        slab = slab_of(l)
        w = [jnp.broadcast_to(cw_ref[k:k + 1, cols], (_SUBLANES, _LANES)) for k in range(width)]

        def row_block(b, carry2):
            base = in_row(b)
            partial = [None] * _TAP_CHAINS
            for k in range(width):
                term = w[k] * ext_ref[slab, pl.ds(base + k, _SUBLANES), :]
                c = k % _TAP_CHAINS
                partial[c] = term if partial[c] is None else partial[c] + term
            while len(partial) > 1:
                partial = [a + b_ for a, b_ in zip(partial[0::2], partial[1::2])] + (
                    [partial[-1]] if len(partial) % 2 else [])
            c_ref[pl.ds(pl.multiple_of(out_row(b), _SUBLANES), _SUBLANES), cols] = partial[0]
            return carry2
        lax.fori_loop(0, n_blocks, row_block, 0, unroll=_TAP_BLOCKS_IN_FLIGHT)
        return carry
    lax.fori_loop(0, d // _LANES, lane_block, 0)


def _conformer_tail(c_ref, cb_ref, lng_ref, lnb_ref, y_ref):
    def tail(rows):
        c = _ln_rows(c_ref[rows, :] + cb_ref[...], lng_ref[...], lnb_ref[...])
        y_ref[rows, :] = (c * jax.nn.sigmoid(c)).astype(y_ref.dtype)
    _for_row_blocks(y_ref.shape[0], tail)


def _mix_conv31_kernel(width, n_pt, seq_tiles, dseq, glu_ref, past_ref, cw_ref, cb_ref, lng_ref, lnb_ref, y_ref,
                       state_ref, ext_ref, sext_ref, c_ref):
    i = pl.program_id(0)
    tm, d = y_ref.shape
    halo = _CONV_HALO
    lead = halo - (width - 1)

    @pl.when(i < n_pt)
    def _():
        def fill(l, cols):
            @pl.when(lax.rem(i, seq_tiles) == 0)
            def _():
                ext_ref[l, 0:halo, :] = jnp.zeros((halo, _LANES), _F32)

            @pl.when(lax.rem(i, seq_tiles) != 0)
            def _():
                ext_ref[l, 0:halo, :] = ext_ref[l, tm:tm + halo, :]

            ext_ref[l, halo:halo + tm, :] = glu_ref[:, cols]

        _depthwise_taps(width, fill, lambda l: l, ext_ref, cw_ref, c_ref, tm // _SUBLANES,
                        lambda b: b * _SUBLANES + lead, lambda b: b * _SUBLANES)

    @pl.when(i >= n_pt)
    def _():
        keep = width - 1

        def lane_block(l, carry):
            cols = pl.ds(pl.multiple_of(l * _LANES, _LANES), _LANES)
            sext_ref[0, 0:tm, :] = glu_ref[:, cols]
            w = [jnp.broadcast_to(cw_ref[k:k + 1, cols], (_SUBLANES, _LANES)) for k in range(width)]

            def sequences(q, carry2):
                s0 = pl.multiple_of(q * _SUBLANES, _SUBLANES)

                def conv_row(m):
                    if m < keep:
                        return past_ref[m, pl.ds(s0, _SUBLANES), cols]
                    return sext_ref[0, pl.ds(s0 * dseq + (m - keep), _SUBLANES, stride=dseq), :]

                for t in range(dseq):
                    partial = [None] * _TAP_CHAINS
                    for k in range(width):
                        term = w[k] * conv_row(t + k)
                        c = k % _TAP_CHAINS
                        partial[c] = term if partial[c] is None else partial[c] + term
                    acc = partial[0]
                    for p in partial[1:]:
                        acc = acc + p
                    sext_ref[0, pl.ds(tm + s0 * dseq + t, _SUBLANES, stride=dseq), :] = acc
                for m in range(keep):
                    state_ref[m, pl.ds(s0, _SUBLANES), cols] = conv_row(m + dseq)
                return carry2
            lax.fori_loop(0, tm // dseq // _SUBLANES, sequences, 0)
            c_ref[:, cols] = sext_ref[0, tm:2 * tm, :]
            return carry
        lax.fori_loop(0, d // _LANES, lane_block, 0)

    _conformer_tail(c_ref, cb_ref, lng_ref, lnb_ref, y_ref)


def _mix_conv31(glu, past, layer, conv_w, conv_b, ln_g, ln_b, n_prompt, seq, dseq):
    m, d = glu.shape
    width = conv_w.shape[0]
    keep = width - 1
    assert keep <= _CONV_HALO and dseq == _SUBLANES
    tm = _TM_CONV
    n_pt = n_prompt // tm
    n_seq = tm // dseq
    assert n_seq % _SUBLANES == 0
    row = pl.BlockSpec((1, d), lambda i: (0, 0))
    scratch = [((d // _LANES, tm + _CONV_HALO, _LANES), _F32), ((1, 2 * tm, _LANES), _F32), ((tm, d), _F32)]
    windows = [((tm, d), _F32), ((keep, n_seq, d), _F32), ((tm, d), _BF16), ((keep, n_seq, d), _F32)]
    seq_block = lambda i: jnp.maximum(i - n_pt, 0)
    y, state = pl.pallas_call(
        functools.partial(_mix_conv31_kernel, width, n_pt, seq // tm, dseq),
        grid=(m // tm,),
        in_specs=[pl.BlockSpec((tm, d), lambda i: (i, 0)),
                  pl.BlockSpec((None, keep, n_seq, d), lambda i: (layer, 0, seq_block(i), 0)),
                  pl.BlockSpec(conv_w.shape, lambda i: (0, 0)), row, row, row],
        out_specs=[pl.BlockSpec((tm, d), lambda i: (i, 0)),
                   pl.BlockSpec((keep, n_seq, d), lambda i: (0, seq_block(i), 0))],
        out_shape=[jax.ShapeDtypeStruct((m, d), _BF16), jax.ShapeDtypeStruct((keep, past.shape[1], d), _F32)],
        scratch_shapes=[pltpu.VMEM(s, dt) for s, dt in scratch],
        compiler_params=_params(("arbitrary",), windows, sum(_nbytes(s, dt) for s, dt in scratch)),
        name="mix_conv31",
    )(glu, jnp.swapaxes(past, 1, 2), conv_w, conv_b.reshape(1, d), ln_g.reshape(1, d), ln_b.reshape(1, d))
    return y, jnp.swapaxes(state, 0, 1)


def _outproj_kernel(has_bias, y_ref, w_ref, *refs):
    if has_bias:
        b_ref, x_ref, xo_ref, wbf_ref = refs
    else:
        x_ref, xo_ref, wbf_ref = refs
    (w,) = _resident_weights([w_ref], [wbf_ref])
    acc = _dot(y_ref[...], w)
    if has_bias:
        acc = acc + b_ref[...]
    xo_ref[...] = x_ref[...] + acc


def _outproj(y, layer, w, bias, x):
    m, k = y.shape
    n = w.shape[2]
    tm, tn = _TM, _TN_OUT
    nb = n // tn
    in_specs = [pl.BlockSpec((tm, k), lambda i, j: (i, 0)),
                pl.BlockSpec((None, k, tn), _resident_weight_block(layer, 0, nb))]
    operands = [y, w]
    if bias is not None:
        in_specs.append(pl.BlockSpec((None, 1, tn), lambda i, j: (layer, 0, j)))
        operands.append(bias.reshape(bias.shape[0], 1, n))
    in_specs.append(pl.BlockSpec((tm, tn), lambda i, j: (i, j)))
    operands.append(x)
    windows = [((tm, k), _BF16), ((k, tn), _F32), ((tm, tn), _F32), ((tm, tn), _F32)]
    return pl.pallas_call(
        functools.partial(_outproj_kernel, bias is not None),
        grid=(m // tm, nb),
        in_specs=in_specs,
        out_specs=pl.BlockSpec((tm, tn), lambda i, j: (i, j)),
        out_shape=jax.ShapeDtypeStruct((m, n), _F32),
        scratch_shapes=[pltpu.VMEM((nb, k, tn), _BF16)],
        compiler_params=_params(("arbitrary", "arbitrary"), windows, _nbytes((nb, k, tn), _BF16)),
        name="outproj",
    )(*operands)


def _ffn_kernel(final, split_tile, x_hbm, g_ref, wu_ref, wd_ref, *refs):
    if final:
        gf_ref, o_head_hbm, o_tail_hbm, acc_ref, h_ref, in_sem, out_sem = refs
    else:
        o_head_hbm, acc_ref, h_ref, in_sem, out_sem = refs
    i, f = pl.program_id(0), pl.program_id(1)
    n_tiles, n_steps = pl.num_programs(0), pl.num_programs(1)
    tm = acc_ref.shape[1]
    slot = lax.rem(i, 2)

    def x_copy(tile, s):
        return pltpu.make_async_copy(x_hbm.at[pl.ds(tile * tm, tm)], acc_ref.at[s], in_sem.at[s])

    def o_copy(tile, s):
        def for_each(act):
            if final:
                @pl.when(tile < split_tile)
                def _():
                    act(pltpu.make_async_copy(acc_ref.at[s], o_head_hbm.at[pl.ds(tile * tm, tm)], out_sem.at[s]))

                @pl.when(tile >= split_tile)
                def _():
                    act(pltpu.make_async_copy(acc_ref.at[s], o_tail_hbm.at[pl.ds((tile - split_tile) * tm, tm)],
                                              out_sem.at[s]))
            else:
                act(pltpu.make_async_copy(acc_ref.at[s], o_head_hbm.at[pl.ds(tile * tm, tm)], out_sem.at[s]))
        return for_each

    def start(copy):
        copy.start()

    def wait(copy):
        copy.wait()

    @pl.when(f == 0)
    def _():
        @pl.when(i == 0)
        def _():
            x_copy(0, 0).start()
        x_copy(i, slot).wait()

        def norm(rows):
            h_ref[rows, :] = _rms_rows(acc_ref[slot, rows, :], g_ref[...]).astype(h_ref.dtype)
        _for_row_blocks(tm, norm)

    @pl.when(jnp.logical_and(f == _FFN_PREFETCH_STEP, i + 1 < n_tiles))
    def _():
        @pl.when(i >= 1)
        def _():
            o_copy(i - 1, 1 - slot)(wait)
        x_copy(i + 1, 1 - slot).start()

    a = jnp.maximum(_dot(h_ref[...], wu_ref[...].astype(_BF16)), 0.0)
    acc_ref[slot] += _dot((a * a).astype(_BF16), wd_ref[...].astype(_BF16))

    @pl.when(f == n_steps - 1)
    def _():
        if final:
            def finish(rows):
                acc_ref[slot, rows, :] = _rms_rows(acc_ref[slot, rows, :], gf_ref[...])
            _for_row_blocks(tm, finish, rows=_ROWS_IN_PLACE, unroll=1)
        o_copy(i, slot)(start)

        @pl.when(i == n_tiles - 1)
        def _():
            o_copy(i, slot)(wait)

            @pl.when(i >= 1)
            def _():
                o_copy(i - 1, 1 - slot)(wait)


def _ffn(x, g, layer, w_up, w_down, g_final=None, split=None):
    rows, d = x.shape
    dff = w_up.shape[2]
    tm, tf = _TM, _TF
    assert dff // tf > _FFN_PREFETCH_STEP >= 1 and rows % tm == 0
    final = g_final is not None
    if final:
        assert split % tm == 0
        out_shape = [jax.ShapeDtypeStruct((split, d), _F32), jax.ShapeDtypeStruct((rows - split, d), _F32)]
    else:
        out_shape = [jax.ShapeDtypeStruct((rows, d), _F32)]
    in_specs = [pl.BlockSpec(memory_space=pl.ANY),
                pl.BlockSpec((None, 1, d), lambda i, f: (layer, 0, 0)),
                pl.BlockSpec((None, d, tf), lambda i, f: (layer, 0, f)),
                pl.BlockSpec((None, tf, d), lambda i, f: (layer, f, 0))]
    operands = [x, g.reshape(g.shape[0], 1, d), w_up, w_down]
    if final:
        in_specs.append(pl.BlockSpec((1, d), lambda i, f: (0, 0)))
        operands.append(g_final.reshape(1, d))
    windows = [((d, tf), _F32), ((tf, d), _F32)]
    scratch = [((2, tm, d), _F32), ((tm, d), _BF16)]
    outs = pl.pallas_call(
        functools.partial(_ffn_kernel, final, split // tm if final else None),
        grid=(rows // tm, dff // tf),
        in_specs=in_specs,
        out_specs=[pl.BlockSpec(memory_space=pl.ANY)] * len(out_shape),
        out_shape=out_shape,
        scratch_shapes=[pltpu.VMEM(s, dt) for s, dt in scratch]
                       + [pltpu.SemaphoreType.DMA((2,)), pltpu.SemaphoreType.DMA((2,))],
        compiler_params=_params(("arbitrary", "arbitrary"), windows, sum(_nbytes(s, dt) for s, dt in scratch)),
        name="ffn",
    )(*operands)
    return outs if final else outs[0]


def _gating_operands(w_s, b_s, dseq, d):
    groups, chunk, _ = w_s.shape
    t = jnp.arange(chunk)
    causal = t[:, None] >= t[None, :]
    same_seq = (t[:, None] // dseq) == (t[None, :] // dseq)
    reps = chunk // dseq
    ws_prompt = jnp.where(causal, w_s, 0)
    ws_sample = jnp.where(causal & same_seq, jnp.tile(w_s[:, :dseq, :dseq], (1, reps, reps)), 0)
    gd = d // groups
    bias_prompt = jnp.repeat(b_s.T, gd, axis=1)
    bias_sample = jnp.repeat(jnp.tile(b_s[:, :dseq].T, (reps, 1)), gd, axis=1)
    return jnp.stack([ws_prompt, ws_sample]).astype(_BF16), jnp.stack([bias_prompt, bias_sample])


def _prompt_conv_state(t, bsz, seq, keep):
    return jnp.stack([lax.slice_in_dim(t, (b + 1) * seq - keep, (b + 1) * seq, axis=0) for b in range(bsz)])


def kernel(x_prompt, x_sample, state_b_conv, state_c_conv, norm_mix_g, norm_ffn_g, final_norm_g, a_w_in, a_ln_g, a_ln_b, a_w_s, a_b_s, a_w_out, b_w_in, b_conv_w, b_w_out, c_w_pw1, c_b_pw1, c_conv_w, c_conv_b, c_ln_g, c_ln_b, c_w_pw2, c_b_pw2, ffn_w_up, ffn_w_down):
    bsz, seq, d = x_prompt.shape
    dbsz, dseq, _ = x_sample.shape
    n_p, n_s = bsz * seq, dbsz * dseq
    depth = norm_mix_g.shape[0]
    chunk = a_w_s.shape[-1]
    assert seq % chunk == 0 and chunk % dseq == 0 and seq & (seq - 1) == 0 and dseq & (dseq - 1) == 0
    assert seq % _TM_MIX == 0 and n_s % _TM_MIX == 0 and seq % _TM_CONV == 0 and n_s % _TM_CONV == 0
    assert n_p % _TM == 0 and n_s % _TM == 0 and (n_p + n_s) % _TM_IN == 0

    x = jnp.concatenate([x_prompt.reshape(n_p, d), x_sample.reshape(n_s, d)], axis=0)
    a_v, b_prompt, b_sample, c_prompt, c_sample = [], [], [], [], []
    for i in range(depth):
        j, kind = divmod(i, _N_MIXERS)
        if kind == 0:
            z = _inproj(_gelu_epilogue, x, norm_mix_g, i, a_w_in, None, j, 0, 1, 2 * d, _TM_IN, _TN_IN, _BF16,
                        "inproj_gelu")
            ws_all, bias_all = _gating_operands(a_w_s[j], a_b_s[j], dseq, d)
            y, vn = _mix_gmlp(z, a_ln_g[j], a_ln_b[j], ws_all, bias_all, n_p, n_s)
            a_v.append(vn.reshape(dbsz, dseq, d))
            x = _outproj(y, j, a_w_out, None, x)
        elif kind == 1:
            bg, h = _inproj(_plain_epilogue, x, norm_mix_g, i, b_w_in, None, j, 0, 1, d, _TM_IN, _TN_IN, _BF16,
                            "inproj_gate", emit_h=True)
            cx = _inproj(_product_epilogue, h, None, None, b_w_in, None, j, 1, 2, d, _TM_IN, _TN_IN_PAIR, _F32,
                         "inproj_product")
            y, sample_state = _mix_conv3(bg, cx, state_b_conv, j, b_conv_w[j], n_p, seq, dseq)
            b_prompt.append(_prompt_conv_state(cx, bsz, seq, state_b_conv.shape[2]))
            b_sample.append(sample_state)
            x = _outproj(y, j, b_w_out, None, x)
        else:
            glu = _inproj(_glu_epilogue, x, norm_mix_g, i, c_w_pw1, c_b_pw1, j, 0, 2, d, _TM_IN, _TN_IN_PAIR, _F32,
                          "inproj_glu")
            y, sample_state = _mix_conv31(glu, state_c_conv, j, c_conv_w[j], c_conv_b[j], c_ln_g[j], c_ln_b[j],
                                          n_p, seq, dseq)
            c_prompt.append(_prompt_conv_state(glu, bsz, seq, state_c_conv.shape[2]))
            c_sample.append(sample_state)
            x = _outproj(y, j, c_w_pw2, c_b_pw2, x)
        if i + 1 < depth:
            x = _ffn(x, norm_ffn_g, i, ffn_w_up, ffn_w_down)
        else:
            y_prompt, y_sample = _ffn(x, norm_ffn_g, i, ffn_w_up, ffn_w_down, final_norm_g, n_p)
    return (y_prompt.reshape(bsz, seq, d), y_sample.reshape(dbsz, dseq, d), jnp.stack(a_v),
            jnp.stack(b_prompt), jnp.stack(b_sample), jnp.stack(c_prompt), jnp.stack(c_sample))
```

```python
import functools

import jax
import jax.numpy as jnp
from jax import lax
from jax.experimental import pallas as pl
from jax.experimental.pallas import tpu as pltpu

_F32 = jnp.float32
_BF16 = jnp.bfloat16
_EPS = 1e-6
_N_MIXERS = 3

_LANES = 128
_SUBLANES = 8
_VMEM_CAP_BYTES = 60000 * 1024
_VMEM_SLACK_BYTES = 12 * 1024 * 1024

_TM = 1024
_TM_IN = 768
_TN_IN = 1024
_TN_IN_PAIR = 512
_TM_MIX = 512
_TN_OUT = 1024
_TF = 1024
_FFN_PREFETCH_STEP = 2
_FFN_NORM_BLOCKS_IN_FLIGHT = 8
_TM_CONV = 256
_ROWS = 16
_ROWS_IN_PLACE = 64
_ROW_BLOCKS_IN_FLIGHT = 16
_CONV_HALO = 32
_TAP_CHAINS = 2
_TAP_BLOCKS_IN_FLIGHT = 16


def _nbytes(shape, dtype):
    n = 1
    for s in shape:
        n *= s
    return n * jnp.dtype(dtype).itemsize


def _params(semantics, windows, scratch=0):
    limit = 2 * sum(_nbytes(s, d) for s, d in windows) + scratch + _VMEM_SLACK_BYTES
    return pltpu.CompilerParams(dimension_semantics=semantics, vmem_limit_bytes=min(limit, _VMEM_CAP_BYTES))


def _rms_rows(x, g):
    return x * lax.rsqrt(jnp.mean(x * x, axis=-1, keepdims=True) + _EPS) * g


def _ln_rows(x, g, b):
    xc = x - jnp.mean(x, axis=-1, keepdims=True)
    return xc * lax.rsqrt(jnp.mean(xc * xc, axis=-1, keepdims=True) + _EPS) * g + b


def _for_row_blocks(n_rows, fn, rows=_ROWS, unroll=_ROW_BLOCKS_IN_FLIGHT):
    def body(r, carry):
        fn(pl.ds(pl.multiple_of(r * rows, rows), rows))
        return carry
    lax.fori_loop(0, n_rows // rows, body, 0, unroll=unroll)


def _dot(a, b):
    return jnp.dot(a, b, preferred_element_type=_F32)


def _gelu_epilogue(accs, out_ref):
    (acc,) = accs
    out_ref[...] = (0.5 * acc * (1.0 + lax.erf(acc * 0.7071067811865476))).astype(out_ref.dtype)


def _plain_epilogue(accs, out_ref):
    (acc,) = accs
    out_ref[...] = acc.astype(out_ref.dtype)


def _product_epilogue(accs, out_ref):
    out_ref[...] = (accs[0] * accs[1]).astype(out_ref.dtype)


def _glu_epilogue(accs, out_ref):
    out_ref[...] = (accs[0] * jax.nn.sigmoid(accs[1])).astype(out_ref.dtype)


def _resident_weight_block(layer, first_block, nb):
    return lambda i, j: (layer, 0, first_block + jnp.where(i == 0, j, nb - 1))


def _resident_weights(w_refs, wbf_refs):
    j = pl.program_id(1)

    @pl.when(pl.program_id(0) == 0)
    def _():
        for w_ref, wbf_ref in zip(w_refs, wbf_refs):
            wbf_ref[j] = w_ref[...].astype(wbf_ref.dtype)
    return [wbf_ref[j] for wbf_ref in wbf_refs]


def _inproj_kernel(epilogue, parts, has_bias, normed, x_ref, *refs):
    if not normed:
        g_ref, refs = refs[0], refs[1:]
    w_refs, refs = refs[:parts], refs[parts:]
    b_refs, refs = (refs[:parts], refs[parts:]) if has_bias else ((), refs)
    out_ref, refs = refs[0], refs[1:]
    if normed:
        h_ref = x_ref
    else:
        h_ref, refs = refs[0], refs[1:]

        @pl.when(pl.program_id(1) == 0)
        def _():
            def norm(rows):
                h_ref[rows, :] = _rms_rows(x_ref[rows, :], g_ref[...]).astype(h_ref.dtype)
            _for_row_blocks(h_ref.shape[0], norm)

    ws = _resident_weights(w_refs, refs)
    h = h_ref[...]
    accs = [_dot(h, w) for w in ws]
    if has_bias:
        accs = [acc + b_ref[...] for acc, b_ref in zip(accs, b_refs)]
    epilogue(accs, out_ref)


def _inproj(epilogue, x, g, g_layer, w, bias, layer, first_part, parts, n, tm, tn, out_dtype, name, emit_h=False):
    m, k = x.shape
    nb = n // tn
    normed = g is None
    in_specs = [pl.BlockSpec((tm, k), lambda i, j: (i, 0))]
    operands = [x]
    if not normed:
        in_specs.append(pl.BlockSpec((None, 1, k), lambda i, j: (g_layer, 0, 0)))
        operands.append(g.reshape(g.shape[0], 1, k))
    for p in range(parts):
        in_specs.append(pl.BlockSpec((None, k, tn), _resident_weight_block(layer, (first_part + p) * nb, nb)))
        operands.append(w)
    if bias is not None:
        for p in range(parts):
            in_specs.append(pl.BlockSpec((None, 1, tn), functools.partial(
                lambda i, j, p: (layer, 0, (first_part + p) * nb + j), p=p)))
            operands.append(bias.reshape(bias.shape[0], 1, -1))
    windows = [((tm, k), x.dtype)] + [((k, tn), _F32)] * parts + [((tm, tn), out_dtype)]
    out_specs = [pl.BlockSpec((tm, tn), lambda i, j: (i, j))]
    out_shape = [jax.ShapeDtypeStruct((m, n), out_dtype)]
    scratch = [((nb, k, tn), _BF16)] * parts
    if emit_h:
        out_specs.append(pl.BlockSpec((tm, k), lambda i, j: (i, 0)))
        out_shape.append(jax.ShapeDtypeStruct((m, k), _BF16))
        windows.append(((tm, k), _BF16))
    elif not normed:
        scratch = [((tm, k), _BF16)] + scratch
    outs = pl.pallas_call(
        functools.partial(_inproj_kernel, epilogue, parts, bias is not None, normed),
        grid=(m // tm, nb),
        in_specs=in_specs,
        out_specs=out_specs,
        out_shape=out_shape,
        scratch_shapes=[pltpu.VMEM(s, dt) for s, dt in scratch],
        compiler_params=_params(("arbitrary", "arbitrary"), windows, sum(_nbytes(s, dt) for s, dt in scratch)),
        name=name,
    )(*operands)
    return outs if emit_h else outs[0]


def _mix_gmlp_kernel(groups, chunk, u_ref, v_ref, lng_ref, lnb_ref, ws_ref, bias_ref, y_ref, vn_ref, vnb_ref):
    tm, d = y_ref.shape
    gd = d // groups

    def normalise(rows):
        vn = _ln_rows(v_ref[rows, :].astype(_F32), lng_ref[...], lnb_ref[...])
        vn_ref[rows, :] = vn
        vnb_ref[rows, :] = vn.astype(vnb_ref.dtype)
    _for_row_blocks(tm, normalise)

    def mix_chunk(c, carry):
        rows = pl.ds(pl.multiple_of(c * chunk, chunk), chunk)
        for g in range(groups):
            cols = slice(g * gd, (g + 1) * gd)
            mixed = _dot(ws_ref[0, g], vnb_ref[rows, cols]) + bias_ref[0, :, cols]
            y_ref[rows, cols] = (u_ref[rows, cols].astype(_F32) * mixed).astype(y_ref.dtype)
        return carry
    lax.fori_loop(0, tm // chunk, mix_chunk, 0, unroll=True)


def _mix_gmlp(z, ln_g, ln_b, ws_all, bias_all, n_prompt, n_sample):
    m, d2 = z.shape
    d = d2 // 2
    tm = _TM_MIX
    n_pt = n_prompt // tm
    groups, chunk = ws_all.shape[1], ws_all.shape[2]
    s_tiles = n_sample // tm
    windows = [((tm, d), _BF16)] * 3 + [((tm, d), _F32), ((1, groups, chunk, chunk), _BF16), ((1, chunk, d), _F32)]
    y, vn = pl.pallas_call(
        functools.partial(_mix_gmlp_kernel, groups, chunk),
        grid=(m // tm,),
        in_specs=[
            pl.BlockSpec((tm, d), lambda i: (i, 0)),
            pl.BlockSpec((tm, d), lambda i: (i, 1)),
            pl.BlockSpec((1, d), lambda i: (0, 0)),
            pl.BlockSpec((1, d), lambda i: (0, 0)),
            pl.BlockSpec((1, groups, chunk, chunk), lambda i: (jnp.where(i < n_pt, 0, 1), 0, 0, 0)),
            pl.BlockSpec((1, chunk, d), lambda i: (jnp.where(i < n_pt, 0, 1), 0, 0)),
        ],
        out_specs=[
            pl.BlockSpec((tm, d), lambda i: (i, 0)),
            pl.BlockSpec((tm, d), lambda i: (jnp.maximum(i - n_pt, 0), 0)),
        ],
        out_shape=[jax.ShapeDtypeStruct((m, d), _BF16), jax.ShapeDtypeStruct((s_tiles * tm, d), _F32)],
        scratch_shapes=[pltpu.VMEM((tm, d), _BF16)],
        compiler_params=_params(("arbitrary",), windows, _nbytes((tm, d), _BF16)),
        name="mix_gmlp",
    )(z, z, ln_g.reshape(1, d), ln_b.reshape(1, d), ws_all, bias_all)
    return y, vn


def _mix_conv3_kernel(n_pt, seq_tiles, dseq, bg_ref, cx_ref, past_ref, cw_ref, y_ref, state_ref, ext_ref):
    i = pl.program_id(0)
    tm, d = y_ref.shape
    halo = _SUBLANES
    seq_start = lax.rem(i, seq_tiles) == 0
    keep = past_ref.shape[1]

    def conv(l, cols, w, sample, b, carry):
        r0 = pl.multiple_of(b * _ROWS, _ROWS)
        rows = pl.ds(r0, _ROWS)
        cur = ext_ref[l, pl.ds(r0 + halo, _ROWS), :]
        prev1 = ext_ref[l, pl.ds(r0 + halo - 1, _ROWS), :]
        prev2 = ext_ref[l, pl.ds(r0 + halo - 2, _ROWS), :]
        if sample:
            pos = lax.broadcasted_iota(jnp.int32, (_ROWS, 1), 0) & (dseq - 1)
            seqs = [b * (_ROWS // dseq) + q for q in range(_ROWS // dseq)]

            def past_row(k):
                return jnp.concatenate(
                    [jnp.broadcast_to(past_ref[s, pl.ds(k, 1), cols], (dseq, _LANES)) for s in seqs], axis=0)
            prev1 = jnp.where(pos >= 1, prev1, past_row(1))
            prev2 = jnp.where(pos >= 2, prev2, jnp.where(pos == 0, past_row(0), past_row(1)))
            for q, s in enumerate(seqs):
                state_ref[s, :, cols] = ext_ref[l, pl.ds(r0 + halo + (q + 1) * dseq - keep, keep), :]
        acc = w[0] * prev2 + w[1] * prev1 + w[2] * cur
        y_ref[rows, cols] = (bg_ref[rows, cols].astype(_F32) * acc).astype(y_ref.dtype)
        return carry

    def lane_block(l, carry):
        cols = pl.ds(pl.multiple_of(l * _LANES, _LANES), _LANES)

        @pl.when(seq_start)
        def _():
            ext_ref[l, 0:halo, :] = jnp.zeros((halo, _LANES), _F32)

        @pl.when(jnp.logical_not(seq_start))
        def _():
            ext_ref[l, 0:halo, :] = ext_ref[l, tm:tm + halo, :]

        ext_ref[l, halo:halo + tm, :] = cx_ref[:, cols]
        w = [jnp.broadcast_to(cw_ref[k:k + 1, cols], (_ROWS, _LANES)) for k in range(3)]

        @pl.when(i < n_pt)
        def _():
            lax.fori_loop(0, tm // _ROWS, functools.partial(conv, l, cols, w, False), 0,
                          unroll=_ROW_BLOCKS_IN_FLIGHT)

        @pl.when(i >= n_pt)
        def _():
            lax.fori_loop(0, tm // _ROWS, functools.partial(conv, l, cols, w, True), 0,
                          unroll=_ROW_BLOCKS_IN_FLIGHT)
        return carry
    lax.fori_loop(0, d // _LANES, lane_block, 0)


def _mix_conv3(bg, cx, past, layer, conv_w, n_prompt, seq, dseq):
    m, d = cx.shape
    keep = past.shape[2]
    assert conv_w.shape[0] == keep + 1 == 3 and keep <= dseq and _ROWS % dseq == 0
    tm = _TM_MIX
    n_pt = n_prompt // tm
    n_seq = tm // dseq
    windows = [((tm, d), _BF16)] * 2 + [((tm, d), _F32)] + [((n_seq, _SUBLANES, d), _F32)] * 2
    seq_block = lambda i: jnp.maximum(i - n_pt, 0)
    return pl.pallas_call(
        functools.partial(_mix_conv3_kernel, n_pt, seq // tm, dseq),
        grid=(m // tm,),
        in_specs=[
            pl.BlockSpec((tm, d), lambda i: (i, 0)),
            pl.BlockSpec((tm, d), lambda i: (i, 0)),
            pl.BlockSpec((None, n_seq, keep, d), lambda i: (layer, seq_block(i), 0, 0)),
            pl.BlockSpec(conv_w.shape, lambda i: (0, 0)),
        ],
        out_specs=[pl.BlockSpec((tm, d), lambda i: (i, 0)),
                   pl.BlockSpec((n_seq, keep, d), lambda i: (seq_block(i), 0, 0))],
        out_shape=[jax.ShapeDtypeStruct((m, d), _BF16), jax.ShapeDtypeStruct((past.shape[1], keep, d), _F32)],
        scratch_shapes=[pltpu.VMEM((d // _LANES, tm + _SUBLANES, _LANES), _F32)],
        compiler_params=_params(("arbitrary",), windows, _nbytes((tm + _SUBLANES, d), _F32)),
        name="mix_conv3",
    )(bg, cx, past, conv_w)


def _depthwise_taps(width, fill, slab_of, ext_ref, cw_ref, c_ref, n_blocks, in_row, out_row):
    d = c_ref.shape[1]

    def lane_block(l, carry):
        cols = pl.ds(pl.multiple_of(l * _LANES, _LANES), _LANES)
        fill(l, cols)
        slab = slab_of(l)
        w = [jnp.broadcast_to(cw_ref[k:k + 1, cols], (_SUBLANES, _LANES)) for k in range(width)]

        def row_block(b, carry2):
            base = in_row(b)
            partial = [None] * _TAP_CHAINS
            for k in range(width):
                term = w[k] * ext_ref[slab, pl.ds(base + k, _SUBLANES), :]
                c = k % _TAP_CHAINS
                partial[c] = term if partial[c] is None else partial[c] + term
            while len(partial) > 1:
                partial = [a + b_ for a, b_ in zip(partial[0::2], partial[1::2])] + (
                    [partial[-1]] if len(partial) % 2 else [])
            c_ref[pl.ds(pl.multiple_of(out_row(b), _SUBLANES), _SUBLANES), cols] = partial[0]
            return carry2
        lax.fori_loop(0, n_blocks, row_block, 0, unroll=_TAP_BLOCKS_IN_FLIGHT)
        return carry
    lax.fori_loop(0, d // _LANES, lane_block, 0)


def _conformer_tail(c_ref, cb_ref, lng_ref, lnb_ref, y_ref):
    def tail(rows):
        c = _ln_rows(c_ref[rows, :] + cb_ref[...], lng_ref[...], lnb_ref[...])
        y_ref[rows, :] = (c * jax.nn.sigmoid(c)).astype(y_ref.dtype)
    _for_row_blocks(y_ref.shape[0], tail)


def _mix_conv31_kernel(width, n_pt, seq_tiles, dseq, glu_ref, past_ref, cw_ref, cb_ref, lng_ref, lnb_ref, y_ref,
                       state_ref, ext_ref, sext_ref, c_ref):
    i = pl.program_id(0)
    tm, d = y_ref.shape
    halo = _CONV_HALO
    lead = halo - (width - 1)

    @pl.when(i < n_pt)
    def _():
        def fill(l, cols):
            @pl.when(lax.rem(i, seq_tiles) == 0)
            def _():
                ext_ref[l, 0:halo, :] = jnp.zeros((halo, _LANES), _F32)

            @pl.when(lax.rem(i, seq_tiles) != 0)
            def _():
                ext_ref[l, 0:halo, :] = ext_ref[l, tm:tm + halo, :]

            ext_ref[l, halo:halo + tm, :] = glu_ref[:, cols]

        _depthwise_taps(width, fill, lambda l: l, ext_ref, cw_ref, c_ref, tm // _SUBLANES,
                        lambda b: b * _SUBLANES + lead, lambda b: b * _SUBLANES)

    @pl.when(i >= n_pt)
    def _():
        keep = width - 1

        def lane_block(l, carry):
            cols = pl.ds(pl.multiple_of(l * _LANES, _LANES), _LANES)
            sext_ref[0, 0:tm, :] = glu_ref[:, cols]
            w = [jnp.broadcast_to(cw_ref[k:k + 1, cols], (_SUBLANES, _LANES)) for k in range(width)]

            def sequences(q, carry2):
                s0 = pl.multiple_of(q * _SUBLANES, _SUBLANES)

                def conv_row(m):
                    if m < keep:
                        return past_ref[m, pl.ds(s0, _SUBLANES), cols]
                    return sext_ref[0, pl.ds(s0 * dseq + (m - keep), _SUBLANES, stride=dseq), :]

                for t in range(dseq):
                    partial = [None] * _TAP_CHAINS
                    for k in range(width):
                        term = w[k] * conv_row(t + k)
                        c = k % _TAP_CHAINS
                        partial[c] = term if partial[c] is None else partial[c] + term
                    acc = partial[0]
                    for p in partial[1:]:
                        acc = acc + p
                    sext_ref[0, pl.ds(tm + s0 * dseq + t, _SUBLANES, stride=dseq), :] = acc
                for m in range(keep):
                    state_ref[m, pl.ds(s0, _SUBLANES), cols] = conv_row(m + dseq)
                return carry2
            lax.fori_loop(0, tm // dseq // _SUBLANES, sequences, 0)
            c_ref[:, cols] = sext_ref[0, tm:2 * tm, :]
            return carry
        lax.fori_loop(0, d // _LANES, lane_block, 0)

    _conformer_tail(c_ref, cb_ref, lng_ref, lnb_ref, y_ref)


def _mix_conv31(glu, past, layer, conv_w, conv_b, ln_g, ln_b, n_prompt, seq, dseq):
    m, d = glu.shape
    width = conv_w.shape[0]
    keep = width - 1
    assert keep <= _CONV_HALO and dseq == _SUBLANES
    tm = _TM_CONV
    n_pt = n_prompt // tm
    n_seq = tm // dseq
    assert n_seq % _SUBLANES == 0
    row = pl.BlockSpec((1, d), lambda i: (0, 0))
    scratch = [((d // _LANES, tm + _CONV_HALO, _LANES), _F32), ((1, 2 * tm, _LANES), _F32), ((tm, d), _F32)]
    windows = [((tm, d), _F32), ((keep, n_seq, d), _F32), ((tm, d), _BF16), ((keep, n_seq, d), _F32)]
    seq_block = lambda i: jnp.maximum(i - n_pt, 0)
    y, state = pl.pallas_call(
        functools.partial(_mix_conv31_kernel, width, n_pt, seq // tm, dseq),
        grid=(m // tm,),
        in_specs=[pl.BlockSpec((tm, d), lambda i: (i, 0)),
                  pl.BlockSpec((None, keep, n_seq, d), lambda i: (layer, 0, seq_block(i), 0)),
                  pl.BlockSpec(conv_w.shape, lambda i: (0, 0)), row, row, row],
        out_specs=[pl.BlockSpec((tm, d), lambda i: (i, 0)),
                   pl.BlockSpec((keep, n_seq, d), lambda i: (0, seq_block(i), 0))],
        out_shape=[jax.ShapeDtypeStruct((m, d), _BF16), jax.ShapeDtypeStruct((keep, past.shape[1], d), _F32)],
        scratch_shapes=[pltpu.VMEM(s, dt) for s, dt in scratch],
        compiler_params=_params(("arbitrary",), windows, sum(_nbytes(s, dt) for s, dt in scratch)),
        name="mix_conv31",
    )(glu, jnp.swapaxes(past, 1, 2), conv_w, conv_b.reshape(1, d), ln_g.reshape(1, d), ln_b.reshape(1, d))
    return y, jnp.swapaxes(state, 0, 1)


def _outproj_kernel(has_bias, y_ref, w_ref, *refs):
    if has_bias:
        b_ref, x_ref, xo_ref, wbf_ref = refs
    else:
        x_ref, xo_ref, wbf_ref = refs
    (w,) = _resident_weights([w_ref], [wbf_ref])
    acc = _dot(y_ref[...], w)
    if has_bias:
        acc = acc + b_ref[...]
    xo_ref[...] = x_ref[...] + acc


def _outproj(y, layer, w, bias, x):
    m, k = y.shape
    n = w.shape[2]
    tm, tn = _TM, _TN_OUT
    nb = n // tn
    in_specs = [pl.BlockSpec((tm, k), lambda i, j: (i, 0)),
                pl.BlockSpec((None, k, tn), _resident_weight_block(layer, 0, nb))]
    operands = [y, w]
    if bias is not None:
        in_specs.append(pl.BlockSpec((None, 1, tn), lambda i, j: (layer, 0, j)))
        operands.append(bias.reshape(bias.shape[0], 1, n))
    in_specs.append(pl.BlockSpec((tm, tn), lambda i, j: (i, j)))
    operands.append(x)
    windows = [((tm, k), _BF16), ((k, tn), _F32), ((tm, tn), _F32), ((tm, tn), _F32)]
    return pl.pallas_call(
        functools.partial(_outproj_kernel, bias is not None),
        grid=(m // tm, nb),
        in_specs=in_specs,
        out_specs=pl.BlockSpec((tm, tn), lambda i, j: (i, j)),
        out_shape=jax.ShapeDtypeStruct((m, n), _F32),
        scratch_shapes=[pltpu.VMEM((nb, k, tn), _BF16)],
        compiler_params=_params(("arbitrary", "arbitrary"), windows, _nbytes((nb, k, tn), _BF16)),
        name="outproj",
    )(*operands)


def _ffn_kernel(final, split_tile, x_hbm, g_ref, wu_ref, wd_ref, *refs):
    if final:
        gf_ref, o_head_hbm, o_tail_hbm, acc_ref, h_ref, in_sem, out_sem = refs
    else:
        o_head_hbm, acc_ref, h_ref, in_sem, out_sem = refs
    i, f = pl.program_id(0), pl.program_id(1)
    n_tiles, n_steps = pl.num_programs(0), pl.num_programs(1)
    tm = acc_ref.shape[1]
    slot = lax.rem(i, 2)

    def x_copy(tile, s):
        return pltpu.make_async_copy(x_hbm.at[pl.ds(tile * tm, tm)], acc_ref.at[s], in_sem.at[s])

    def o_copy(tile, s):
        def for_each(act):
            if final:
                @pl.when(tile < split_tile)
                def _():
                    act(pltpu.make_async_copy(acc_ref.at[s], o_head_hbm.at[pl.ds(tile * tm, tm)], out_sem.at[s]))

                @pl.when(tile >= split_tile)
                def _():
                    act(pltpu.make_async_copy(acc_ref.at[s], o_tail_hbm.at[pl.ds((tile - split_tile) * tm, tm)],
                                              out_sem.at[s]))
            else:
                act(pltpu.make_async_copy(acc_ref.at[s], o_head_hbm.at[pl.ds(tile * tm, tm)], out_sem.at[s]))
        return for_each

    def start(copy):
        copy.start()

    def wait(copy):
        copy.wait()

    @pl.when(f == 0)
    def _():
        @pl.when(i == 0)
        def _():
            x_copy(0, 0).start()
        x_copy(i, slot).wait()

        def norm(rows):
            h_ref[rows, :] = _rms_rows(acc_ref[slot, rows, :], g_ref[...]).astype(h_ref.dtype)
        _for_row_blocks(tm, norm, unroll=_FFN_NORM_BLOCKS_IN_FLIGHT)

    @pl.when(jnp.logical_and(f == _FFN_PREFETCH_STEP, i + 1 < n_tiles))
    def _():
        @pl.when(i >= 1)
        def _():
            o_copy(i - 1, 1 - slot)(wait)
        x_copy(i + 1, 1 - slot).start()

    a = jnp.maximum(_dot(h_ref[...], wu_ref[...].astype(_BF16)), 0.0)
    acc_ref[slot] += _dot((a * a).astype(_BF16), wd_ref[...].astype(_BF16))

    @pl.when(f == n_steps - 1)
    def _():
        if final:
            def finish(rows):
                acc_ref[slot, rows, :] = _rms_rows(acc_ref[slot, rows, :], gf_ref[...])
            _for_row_blocks(tm, finish, rows=_ROWS_IN_PLACE, unroll=1)
        o_copy(i, slot)(start)

        @pl.when(i == n_tiles - 1)
        def _():
            o_copy(i, slot)(wait)

            @pl.when(i >= 1)
            def _():
                o_copy(i - 1, 1 - slot)(wait)


def _ffn(x, g, layer, w_up, w_down, g_final=None, split=None):
    rows, d = x.shape
    dff = w_up.shape[2]
    tm, tf = _TM, _TF
    assert dff // tf > _FFN_PREFETCH_STEP >= 1 and rows % tm == 0
    final = g_final is not None
    if final:
        assert split % tm == 0
        out_shape = [jax.ShapeDtypeStruct((split, d), _F32), jax.ShapeDtypeStruct((rows - split, d), _F32)]
    else:
        out_shape = [jax.ShapeDtypeStruct((rows, d), _F32)]
    in_specs = [pl.BlockSpec(memory_space=pl.ANY),
                pl.BlockSpec((None, 1, d), lambda i, f: (layer, 0, 0)),
                pl.BlockSpec((None, d, tf), lambda i, f: (layer, 0, f)),
                pl.BlockSpec((None, tf, d), lambda i, f: (layer, f, 0))]
    operands = [x, g.reshape(g.shape[0], 1, d), w_up, w_down]
    if final:
        in_specs.append(pl.BlockSpec((1, d), lambda i, f: (0, 0)))
        operands.append(g_final.reshape(1, d))
    windows = [((d, tf), _F32), ((tf, d), _F32)]
    scratch = [((2, tm, d), _F32), ((tm, d), _BF16)]
    outs = pl.pallas_call(
        functools.partial(_ffn_kernel, final, split // tm if final else None),
        grid=(rows // tm, dff // tf),
        in_specs=in_specs,
        out_specs=[pl.BlockSpec(memory_space=pl.ANY)] * len(out_shape),
        out_shape=out_shape,
        scratch_shapes=[pltpu.VMEM(s, dt) for s, dt in scratch]
                       + [pltpu.SemaphoreType.DMA((2,)), pltpu.SemaphoreType.DMA((2,))],
        compiler_params=_params(("arbitrary", "arbitrary"), windows, sum(_nbytes(s, dt) for s, dt in scratch)),
        name="ffn",
    )(*operands)
    return outs if final else outs[0]


def _gating_operands(w_s, b_s, dseq, d):
    groups, chunk, _ = w_s.shape
    t = jnp.arange(chunk)
    causal = t[:, None] >= t[None, :]
    same_seq = (t[:, None] // dseq) == (t[None, :] // dseq)
    reps = chunk // dseq
    ws_prompt = jnp.where(causal, w_s, 0)
    ws_sample = jnp.where(causal & same_seq, jnp.tile(w_s[:, :dseq, :dseq], (1, reps, reps)), 0)
    gd = d // groups
    bias_prompt = jnp.repeat(b_s.T, gd, axis=1)
    bias_sample = jnp.repeat(jnp.tile(b_s[:, :dseq].T, (reps, 1)), gd, axis=1)
    return jnp.stack([ws_prompt, ws_sample]).astype(_BF16), jnp.stack([bias_prompt, bias_sample])


def _prompt_conv_state(t, bsz, seq, keep):
    return jnp.stack([lax.slice_in_dim(t, (b + 1) * seq - keep, (b + 1) * seq, axis=0) for b in range(bsz)])


def kernel(x_prompt, x_sample, state_b_conv, state_c_conv, norm_mix_g, norm_ffn_g, final_norm_g, a_w_in, a_ln_g, a_ln_b, a_w_s, a_b_s, a_w_out, b_w_in, b_conv_w, b_w_out, c_w_pw1, c_b_pw1, c_conv_w, c_conv_b, c_ln_g, c_ln_b, c_w_pw2, c_b_pw2, ffn_w_up, ffn_w_down):
    bsz, seq, d = x_prompt.shape
    dbsz, dseq, _ = x_sample.shape
    n_p, n_s = bsz * seq, dbsz * dseq
    depth = norm_mix_g.shape[0]
    chunk = a_w_s.shape[-1]
    assert seq % chunk == 0 and chunk % dseq == 0 and seq & (seq - 1) == 0 and dseq & (dseq - 1) == 0
    assert seq % _TM_MIX == 0 and n_s % _TM_MIX == 0 and seq % _TM_CONV == 0 and n_s % _TM_CONV == 0
    assert n_p % _TM == 0 and n_s % _TM == 0 and (n_p + n_s) % _TM_IN == 0

    x = jnp.concatenate([x_prompt.reshape(n_p, d), x_sample.reshape(n_s, d)], axis=0)
    a_v, b_prompt, b_sample, c_prompt, c_sample = [], [], [], [], []
    for i in range(depth):
        j, kind = divmod(i, _N_MIXERS)
        if kind == 0:
            z = _inproj(_gelu_epilogue, x, norm_mix_g, i, a_w_in, None, j, 0, 1, 2 * d, _TM_IN, _TN_IN, _BF16,
                        "inproj_gelu")
            ws_all, bias_all = _gating_operands(a_w_s[j], a_b_s[j], dseq, d)
            y, vn = _mix_gmlp(z, a_ln_g[j], a_ln_b[j], ws_all, bias_all, n_p, n_s)
            a_v.append(vn.reshape(dbsz, dseq, d))
            x = _outproj(y, j, a_w_out, None, x)
        elif kind == 1:
            bg, h = _inproj(_plain_epilogue, x, norm_mix_g, i, b_w_in, None, j, 0, 1, d, _TM_IN, _TN_IN, _BF16,
                            "inproj_gate", emit_h=True)
            cx = _inproj(_product_epilogue, h, None, None, b_w_in, None, j, 1, 2, d, _TM_IN, _TN_IN_PAIR, _F32,
                         "inproj_product")
            y, sample_state = _mix_conv3(bg, cx, state_b_conv, j, b_conv_w[j], n_p, seq, dseq)
            b_prompt.append(_prompt_conv_state(cx, bsz, seq, state_b_conv.shape[2]))
            b_sample.append(sample_state)
            x = _outproj(y, j, b_w_out, None, x)
        else:
            glu = _inproj(_glu_epilogue, x, norm_mix_g, i, c_w_pw1, c_b_pw1, j, 0, 2, d, _TM_IN, _TN_IN_PAIR, _F32,
                          "inproj_glu")
            y, sample_state = _mix_conv31(glu, state_c_conv, j, c_conv_w[j], c_conv_b[j], c_ln_g[j], c_ln_b[j],
                                          n_p, seq, dseq)
            c_prompt.append(_prompt_conv_state(glu, bsz, seq, state_c_conv.shape[2]))
            c_sample.append(sample_state)
            x = _outproj(y, j, c_w_pw2, c_b_pw2, x)
        if i + 1 < depth:
            x = _ffn(x, norm_ffn_g, i, ffn_w_up, ffn_w_down)
        else:
            y_prompt, y_sample = _ffn(x, norm_ffn_g, i, ffn_w_up, ffn_w_down, final_norm_g, n_p)
    return (y_prompt.reshape(bsz, seq, d), y_sample.reshape(dbsz, dseq, d), jnp.stack(a_v),
            jnp.stack(b_prompt), jnp.stack(b_sample), jnp.stack(c_prompt), jnp.stack(c_sample))
```

```python
import functools

import jax
import jax.numpy as jnp
from jax import lax
from jax.experimental import pallas as pl
from jax.experimental.pallas import tpu as pltpu

_F32 = jnp.float32
_BF16 = jnp.bfloat16
_EPS = 1e-6
_N_MIXERS = 3

_LANES = 128
_SUBLANES = 8
_VMEM_CAP_BYTES = 60000 * 1024
_VMEM_SLACK_BYTES = 12 * 1024 * 1024

_TM = 1024
_TM_IN = 768
_TN_IN = 1024
_TN_IN_PAIR = 512
_TM_MIX = 512
_TN_OUT = 1024
_TF = 1024
_FFN_PREFETCH_STEP = 2
_FFN_NORM_BLOCKS_IN_FLIGHT = 8
_TM_CONV = 256
_ROWS = 16
_ROWS_IN_PLACE = 64
_ROW_BLOCKS_IN_FLIGHT = 16
_CONV_HALO = 32
_TAP_CHAINS = 2
_TAP_BLOCKS_IN_FLIGHT = 16


def _nbytes(shape, dtype):
    n = 1
    for s in shape:
        n *= s
    return n * jnp.dtype(dtype).itemsize


def _params(semantics, windows, scratch=0):
    limit = 2 * sum(_nbytes(s, d) for s, d in windows) + scratch + _VMEM_SLACK_BYTES
    return pltpu.CompilerParams(dimension_semantics=semantics, vmem_limit_bytes=min(limit, _VMEM_CAP_BYTES))


def _rms_rows(x, g):
    return x * lax.rsqrt(jnp.mean(x * x, axis=-1, keepdims=True) + _EPS) * g


def _ln_rows(x, g, b):
    xc = x - jnp.mean(x, axis=-1, keepdims=True)
    return xc * lax.rsqrt(jnp.mean(xc * xc, axis=-1, keepdims=True) + _EPS) * g + b


def _for_row_blocks(n_rows, fn, rows=_ROWS, unroll=_ROW_BLOCKS_IN_FLIGHT):
    def body(r, carry):
        fn(pl.ds(pl.multiple_of(r * rows, rows), rows))
        return carry
    lax.fori_loop(0, n_rows // rows, body, 0, unroll=unroll)


def _dot(a, b):
    return jnp.dot(a, b, preferred_element_type=_F32)


def _gelu_epilogue(accs, out_ref):
    (acc,) = accs
    out_ref[...] = (0.5 * acc * (1.0 + lax.erf(acc * 0.7071067811865476))).astype(out_ref.dtype)


def _plain_epilogue(accs, out_ref):
    (acc,) = accs
    out_ref[...] = acc.astype(out_ref.dtype)


def _product_epilogue(accs, out_ref):
    out_ref[...] = (accs[0] * accs[1]).astype(out_ref.dtype)


def _glu_epilogue(accs, out_ref):
    out_ref[...] = (accs[0] * jax.nn.sigmoid(accs[1])).astype(out_ref.dtype)


def _resident_weight_block(layer, first_block, nb):
    return lambda i, j: (layer, 0, first_block + jnp.where(i == 0, j, nb - 1))


def _resident_weights(w_refs, wbf_refs):
    j = pl.program_id(1)

    @pl.when(pl.program_id(0) == 0)
    def _():
        for w_ref, wbf_ref in zip(w_refs, wbf_refs):
            wbf_ref[j] = w_ref[...].astype(wbf_ref.dtype)
    return [wbf_ref[j] for wbf_ref in wbf_refs]


def _inproj_kernel(epilogue, parts, has_bias, normed, x_ref, *refs):
    if not normed:
        g_ref, refs = refs[0], refs[1:]
    w_refs, refs = refs[:parts], refs[parts:]
    b_refs, refs = (refs[:parts], refs[parts:]) if has_bias else ((), refs)
    out_ref, refs = refs[0], refs[1:]
    if normed:
        h_ref = x_ref
    else:
        h_ref, refs = refs[0], refs[1:]

        @pl.when(pl.program_id(1) == 0)
        def _():
            def norm(rows):
                h_ref[rows, :] = _rms_rows(x_ref[rows, :], g_ref[...]).astype(h_ref.dtype)
            _for_row_blocks(h_ref.shape[0], norm)

    ws = _resident_weights(w_refs, refs)
    h = h_ref[...]
    accs = [_dot(h, w) for w in ws]
    if has_bias:
        accs = [acc + b_ref[...] for acc, b_ref in zip(accs, b_refs)]
    epilogue(accs, out_ref)


def _inproj(epilogue, x, g, g_layer, w, bias, layer, first_part, parts, n, tm, tn, out_dtype, name, emit_h=False):
    m, k = x.shape
    nb = n // tn
    normed = g is None
    in_specs = [pl.BlockSpec((tm, k), lambda i, j: (i, 0))]
    operands = [x]
    if not normed:
        in_specs.append(pl.BlockSpec((None, 1, k), lambda i, j: (g_layer, 0, 0)))
        operands.append(g.reshape(g.shape[0], 1, k))
    for p in range(parts):
        in_specs.append(pl.BlockSpec((None, k, tn), _resident_weight_block(layer, (first_part + p) * nb, nb)))
        operands.append(w)
    if bias is not None:
        for p in range(parts):
            in_specs.append(pl.BlockSpec((None, 1, tn), functools.partial(
                lambda i, j, p: (layer, 0, (first_part + p) * nb + j), p=p)))
            operands.append(bias.reshape(bias.shape[0], 1, -1))
    windows = [((tm, k), x.dtype)] + [((k, tn), _F32)] * parts + [((tm, tn), out_dtype)]
    out_specs = [pl.BlockSpec((tm, tn), lambda i, j: (i, j))]
    out_shape = [jax.ShapeDtypeStruct((m, n), out_dtype)]
    scratch = [((nb, k, tn), _BF16)] * parts
    if emit_h:
        out_specs.append(pl.BlockSpec((tm, k), lambda i, j: (i, 0)))
        out_shape.append(jax.ShapeDtypeStruct((m, k), _BF16))
        windows.append(((tm, k), _BF16))
    elif not normed:
        scratch = [((tm, k), _BF16)] + scratch
    outs = pl.pallas_call(
        functools.partial(_inproj_kernel, epilogue, parts, bias is not None, normed),
        grid=(m // tm, nb),
        in_specs=in_specs,
        out_specs=out_specs,
        out_shape=out_shape,
        scratch_shapes=[pltpu.VMEM(s, dt) for s, dt in scratch],
        compiler_params=_params(("arbitrary", "arbitrary"), windows, sum(_nbytes(s, dt) for s, dt in scratch)),
        name=name,
    )(*operands)
    return outs if emit_h else outs[0]


def _mix_gmlp_kernel(groups, chunk, aliased, u_ref, v_ref, lng_ref, lnb_ref, ws_ref, bias_ref, *refs):
    y_ref, vn_ref, vnb_ref = refs[1:] if aliased else refs
    tm, d = y_ref.shape
    gd = d // groups

    def normalise(rows):
        vn = _ln_rows(v_ref[rows, :].astype(_F32), lng_ref[...], lnb_ref[...])
        for slot in range(vn_ref.shape[0]):
            vn_ref[slot, rows, :] = vn
        vnb_ref[rows, :] = vn.astype(vnb_ref.dtype)
    _for_row_blocks(tm, normalise)

    def mix_chunk(c, carry):
        rows = pl.ds(pl.multiple_of(c * chunk, chunk), chunk)
        for g in range(groups):
            cols = slice(g * gd, (g + 1) * gd)
            mixed = _dot(ws_ref[0, g], vnb_ref[rows, cols]) + bias_ref[0, :, cols]
            y_ref[rows, cols] = (u_ref[rows, cols].astype(_F32) * mixed).astype(y_ref.dtype)
        return carry
    lax.fori_loop(0, tm // chunk, mix_chunk, 0, unroll=True)


def _mix_gmlp(z, ln_g, ln_b, ws_all, bias_all, n_prompt, n_sample, layer, n_layers, stacked_vn):
    m, d2 = z.shape
    d = d2 // 2
    tm = _TM_MIX
    n_pt = n_prompt // tm
    groups, chunk = ws_all.shape[1], ws_all.shape[2]
    aliased = stacked_vn is not None
    slots = 1 if aliased else n_layers
    first_slot = layer if aliased else 0
    windows = [((tm, d), _BF16)] * 3 + [((slots, tm, d), _F32), ((1, groups, chunk, chunk), _BF16), ((1, chunk, d), _F32)]
    in_specs = [
        pl.BlockSpec((tm, d), lambda i: (i, 0)),
        pl.BlockSpec((tm, d), lambda i: (i, 1)),
        pl.BlockSpec((1, d), lambda i: (0, 0)),
        pl.BlockSpec((1, d), lambda i: (0, 0)),
        pl.BlockSpec((1, groups, chunk, chunk), lambda i: (jnp.where(i < n_pt, 0, 1), 0, 0, 0)),
        pl.BlockSpec((1, chunk, d), lambda i: (jnp.where(i < n_pt, 0, 1), 0, 0)),
    ]
    operands = [z, z, ln_g.reshape(1, d), ln_b.reshape(1, d), ws_all, bias_all]
    if aliased:
        in_specs.append(pl.BlockSpec(memory_space=pl.ANY))
        operands.append(stacked_vn)
    return pl.pallas_call(
        functools.partial(_mix_gmlp_kernel, groups, chunk, aliased),
        grid=(m // tm,),
        in_specs=in_specs,
        out_specs=[
            pl.BlockSpec((tm, d), lambda i: (i, 0)),
            pl.BlockSpec((slots, tm, d), lambda i: (first_slot, jnp.maximum(i - n_pt, 0), 0)),
        ],
        out_shape=[jax.ShapeDtypeStruct((m, d), _BF16), jax.ShapeDtypeStruct((n_layers, n_sample, d), _F32)],
        scratch_shapes=[pltpu.VMEM((tm, d), _BF16)],
        input_output_aliases={len(operands) - 1: 1} if aliased else {},
        compiler_params=_params(("arbitrary",), windows, _nbytes((tm, d), _BF16)),
        name="mix_gmlp",
    )(*operands)


def _mix_conv3_kernel(n_pt, seq_tiles, dseq, bg_ref, cx_ref, past_ref, cw_ref, y_ref, state_ref, ext_ref):
    i = pl.program_id(0)
    tm, d = y_ref.shape
    halo = _SUBLANES
    seq_start = lax.rem(i, seq_tiles) == 0
    keep = past_ref.shape[1]

    def conv(l, cols, w, sample, b, carry):
        r0 = pl.multiple_of(b * _ROWS, _ROWS)
        rows = pl.ds(r0, _ROWS)
        cur = ext_ref[l, pl.ds(r0 + halo, _ROWS), :]
        prev1 = ext_ref[l, pl.ds(r0 + halo - 1, _ROWS), :]
        prev2 = ext_ref[l, pl.ds(r0 + halo - 2, _ROWS), :]
        if sample:
            pos = lax.broadcasted_iota(jnp.int32, (_ROWS, 1), 0) & (dseq - 1)
            seqs = [b * (_ROWS // dseq) + q for q in range(_ROWS // dseq)]

            def past_row(k):
                return jnp.concatenate(
                    [jnp.broadcast_to(past_ref[s, pl.ds(k, 1), cols], (dseq, _LANES)) for s in seqs], axis=0)
            prev1 = jnp.where(pos >= 1, prev1, past_row(1))
            prev2 = jnp.where(pos >= 2, prev2, jnp.where(pos == 0, past_row(0), past_row(1)))
            for q, s in enumerate(seqs):
                state_ref[s, :, cols] = ext_ref[l, pl.ds(r0 + halo + (q + 1) * dseq - keep, keep), :]
        acc = w[0] * prev2 + w[1] * prev1 + w[2] * cur
        y_ref[rows, cols] = (bg_ref[rows, cols].astype(_F32) * acc).astype(y_ref.dtype)
        return carry

    def lane_block(l, carry):
        cols = pl.ds(pl.multiple_of(l * _LANES, _LANES), _LANES)

        @pl.when(seq_start)
        def _():
            ext_ref[l, 0:halo, :] = jnp.zeros((halo, _LANES), _F32)

        @pl.when(jnp.logical_not(seq_start))
        def _():
            ext_ref[l, 0:halo, :] = ext_ref[l, tm:tm + halo, :]

        ext_ref[l, halo:halo + tm, :] = cx_ref[:, cols]
        w = [jnp.broadcast_to(cw_ref[k:k + 1, cols], (_ROWS, _LANES)) for k in range(3)]

        @pl.when(i < n_pt)
        def _():
            lax.fori_loop(0, tm // _ROWS, functools.partial(conv, l, cols, w, False), 0,
                          unroll=_ROW_BLOCKS_IN_FLIGHT)

        @pl.when(i >= n_pt)
        def _():
            lax.fori_loop(0, tm // _ROWS, functools.partial(conv, l, cols, w, True), 0,
                          unroll=_ROW_BLOCKS_IN_FLIGHT)
        return carry
    lax.fori_loop(0, d // _LANES, lane_block, 0)


def _mix_conv3(bg, cx, past, layer, conv_w, n_prompt, seq, dseq):
    m, d = cx.shape
    keep = past.shape[2]
    assert conv_w.shape[0] == keep + 1 == 3 and keep <= dseq and _ROWS % dseq == 0
    tm = _TM_MIX
    n_pt = n_prompt // tm
    n_seq = tm // dseq
    windows = [((tm, d), _BF16)] * 2 + [((tm, d), _F32)] + [((n_seq, _SUBLANES, d), _F32)] * 2
    seq_block = lambda i: jnp.maximum(i - n_pt, 0)
    return pl.pallas_call(
        functools.partial(_mix_conv3_kernel, n_pt, seq // tm, dseq),
        grid=(m // tm,),
        in_specs=[
            pl.BlockSpec((tm, d), lambda i: (i, 0)),
            pl.BlockSpec((tm, d), lambda i: (i, 0)),
            pl.BlockSpec((None, n_seq, keep, d), lambda i: (layer, seq_block(i), 0, 0)),
            pl.BlockSpec(conv_w.shape, lambda i: (0, 0)),
        ],
        out_specs=[pl.BlockSpec((tm, d), lambda i: (i, 0)),
                   pl.BlockSpec((n_seq, keep, d), lambda i: (seq_block(i), 0, 0))],
        out_shape=[jax.ShapeDtypeStruct((m, d), _BF16), jax.ShapeDtypeStruct((past.shape[1], keep, d), _F32)],
        scratch_shapes=[pltpu.VMEM((d // _LANES, tm + _SUBLANES, _LANES), _F32)],
        compiler_params=_params(("arbitrary",), windows, _nbytes((tm + _SUBLANES, d), _F32)),
        name="mix_conv3",
    )(bg, cx, past, conv_w)


def _depthwise_taps(width, fill, slab_of, ext_ref, cw_ref, c_ref, n_blocks, in_row, out_row):
    d = c_ref.shape[1]

    def lane_block(l, carry):
        cols = pl.ds(pl.multiple_of(l * _LANES, _LANES), _LANES)
        fill(l, cols)
        slab = slab_of(l)
        w = [jnp.broadcast_to(cw_ref[k:k + 1, cols], (_SUBLANES, _LANES)) for k in range(width)]

        def row_block(b, carry2):
            base = in_row(b)
            partial = [None] * _TAP_CHAINS
            for k in range(width):
                term = w[k] * ext_ref[slab, pl.ds(base + k, _SUBLANES), :]
                c = k % _TAP_CHAINS
                partial[c] = term if partial[c] is None else partial[c] + term
            while len(partial) > 1:
                partial = [a + b_ for a, b_ in zip(partial[0::2], partial[1::2])] + (
                    [partial[-1]] if len(partial) % 2 else [])
            c_ref[pl.ds(pl.multiple_of(out_row(b), _SUBLANES), _SUBLANES), cols] = partial[0]
            return carry2
        lax.fori_loop(0, n_blocks, row_block, 0, unroll=_TAP_BLOCKS_IN_FLIGHT)
        return carry
    lax.fori_loop(0, d // _LANES, lane_block, 0)


def _conformer_tail(c_ref, cb_ref, lng_ref, lnb_ref, y_ref):
    def tail(rows):
        c = _ln_rows(c_ref[rows, :] + cb_ref[...], lng_ref[...], lnb_ref[...])
        y_ref[rows, :] = (c * jax.nn.sigmoid(c)).astype(y_ref.dtype)
    _for_row_blocks(y_ref.shape[0], tail)


def _mix_conv31_kernel(width, n_pt, seq_tiles, dseq, glu_ref, past_ref, cw_ref, cb_ref, lng_ref, lnb_ref, y_ref,
                       state_ref, ext_ref, sext_ref, c_ref):
    i = pl.program_id(0)
    tm, d = y_ref.shape
    halo = _CONV_HALO
    lead = halo - (width - 1)

    @pl.when(i < n_pt)
    def _():
        def fill(l, cols):
            @pl.when(lax.rem(i, seq_tiles) == 0)
            def _():
                ext_ref[l, 0:halo, :] = jnp.zeros((halo, _LANES), _F32)

            @pl.when(lax.rem(i, seq_tiles) != 0)
            def _():
                ext_ref[l, 0:halo, :] = ext_ref[l, tm:tm + halo, :]

            ext_ref[l, halo:halo + tm, :] = glu_ref[:, cols]

        _depthwise_taps(width, fill, lambda l: l, ext_ref, cw_ref, c_ref, tm // _SUBLANES,
                        lambda b: b * _SUBLANES + lead, lambda b: b * _SUBLANES)

    @pl.when(i >= n_pt)
    def _():
        keep = width - 1

        def lane_block(l, carry):
            cols = pl.ds(pl.multiple_of(l * _LANES, _LANES), _LANES)
            sext_ref[0, 0:tm, :] = glu_ref[:, cols]
            w = [jnp.broadcast_to(cw_ref[k:k + 1, cols], (_SUBLANES, _LANES)) for k in range(width)]

            def sequences(q, carry2):
                s0 = pl.multiple_of(q * _SUBLANES, _SUBLANES)

                def conv_row(m):
                    if m < keep:
                        return past_ref[m, pl.ds(s0, _SUBLANES), cols]
                    return sext_ref[0, pl.ds(s0 * dseq + (m - keep), _SUBLANES, stride=dseq), :]

                for t in range(dseq):
                    partial = [None] * _TAP_CHAINS
                    for k in range(width):
                        term = w[k] * conv_row(t + k)
                        c = k % _TAP_CHAINS
                        partial[c] = term if partial[c] is None else partial[c] + term
                    acc = partial[0]
                    for p in partial[1:]:
                        acc = acc + p
                    sext_ref[0, pl.ds(tm + s0 * dseq + t, _SUBLANES, stride=dseq), :] = acc
                for m in range(keep):
                    state_ref[m, pl.ds(s0, _SUBLANES), cols] = conv_row(m + dseq)
                return carry2
            lax.fori_loop(0, tm // dseq // _SUBLANES, sequences, 0)
            c_ref[:, cols] = sext_ref[0, tm:2 * tm, :]
            return carry
        lax.fori_loop(0, d // _LANES, lane_block, 0)

    _conformer_tail(c_ref, cb_ref, lng_ref, lnb_ref, y_ref)


def _mix_conv31(glu, past, layer, conv_w, conv_b, ln_g, ln_b, n_prompt, seq, dseq):
    m, d = glu.shape
    width = conv_w.shape[0]
    keep = width - 1
    assert keep <= _CONV_HALO and dseq == _SUBLANES
    tm = _TM_CONV
    n_pt = n_prompt // tm
    n_seq = tm // dseq
    assert n_seq % _SUBLANES == 0
    row = pl.BlockSpec((1, d), lambda i: (0, 0))
    scratch = [((d // _LANES, tm + _CONV_HALO, _LANES), _F32), ((1, 2 * tm, _LANES), _F32), ((tm, d), _F32)]
    windows = [((tm, d), _F32), ((keep, n_seq, d), _F32), ((tm, d), _BF16), ((keep, n_seq, d), _F32)]
    seq_block = lambda i: jnp.maximum(i - n_pt, 0)
    y, state = pl.pallas_call(
        functools.partial(_mix_conv31_kernel, width, n_pt, seq // tm, dseq),
        grid=(m // tm,),
        in_specs=[pl.BlockSpec((tm, d), lambda i: (i, 0)),
                  pl.BlockSpec((None, keep, n_seq, d), lambda i: (layer, 0, seq_block(i), 0)),
                  pl.BlockSpec(conv_w.shape, lambda i: (0, 0)), row, row, row],
        out_specs=[pl.BlockSpec((tm, d), lambda i: (i, 0)),
                   pl.BlockSpec((keep, n_seq, d), lambda i: (0, seq_block(i), 0))],
        out_shape=[jax.ShapeDtypeStruct((m, d), _BF16), jax.ShapeDtypeStruct((keep, past.shape[1], d), _F32)],
        scratch_shapes=[pltpu.VMEM(s, dt) for s, dt in scratch],
        compiler_params=_params(("arbitrary",), windows, sum(_nbytes(s, dt) for s, dt in scratch)),
        name="mix_conv31",
    )(glu, jnp.swapaxes(past, 1, 2), conv_w, conv_b.reshape(1, d), ln_g.reshape(1, d), ln_b.reshape(1, d))
    return y, jnp.swapaxes(state, 0, 1)


def _outproj_kernel(has_bias, y_ref, w_ref, *refs):
    if has_bias:
        b_ref, x_ref, xo_ref, wbf_ref = refs
    else:
        x_ref, xo_ref, wbf_ref = refs
    (w,) = _resident_weights([w_ref], [wbf_ref])
    acc = _dot(y_ref[...], w)
    if has_bias:
        acc = acc + b_ref[...]
    xo_ref[...] = x_ref[...] + acc


def _outproj(y, layer, w, bias, x):
    m, k = y.shape
    n = w.shape[2]
    tm, tn = _TM, _TN_OUT
    nb = n // tn
    in_specs = [pl.BlockSpec((tm, k), lambda i, j: (i, 0)),
                pl.BlockSpec((None, k, tn), _resident_weight_block(layer, 0, nb))]
    operands = [y, w]
    if bias is not None:
        in_specs.append(pl.BlockSpec((None, 1, tn), lambda i, j: (layer, 0, j)))
        operands.append(bias.reshape(bias.shape[0], 1, n))
    in_specs.append(pl.BlockSpec((tm, tn), lambda i, j: (i, j)))
    operands.append(x)
    windows = [((tm, k), _BF16), ((k, tn), _F32), ((tm, tn), _F32), ((tm, tn), _F32)]
    return pl.pallas_call(
        functools.partial(_outproj_kernel, bias is not None),
        grid=(m // tm, nb),
        in_specs=in_specs,
        out_specs=pl.BlockSpec((tm, tn), lambda i, j: (i, j)),
        out_shape=jax.ShapeDtypeStruct((m, n), _F32),
        scratch_shapes=[pltpu.VMEM((nb, k, tn), _BF16)],
        compiler_params=_params(("arbitrary", "arbitrary"), windows, _nbytes((nb, k, tn), _BF16)),
        name="outproj",
    )(*operands)


def _ffn_kernel(final, split_tile, x_hbm, g_ref, wu_ref, wd_ref, *refs):
    if final:
        gf_ref, o_head_hbm, o_tail_hbm, acc_ref, h_ref, in_sem, out_sem = refs
    else:
        o_head_hbm, acc_ref, h_ref, in_sem, out_sem = refs
    i, f = pl.program_id(0), pl.program_id(1)
    n_tiles, n_steps = pl.num_programs(0), pl.num_programs(1)
    tm = acc_ref.shape[1]
    slot = lax.rem(i, 2)

    def x_copy(tile, s):
        return pltpu.make_async_copy(x_hbm.at[pl.ds(tile * tm, tm)], acc_ref.at[s], in_sem.at[s])

    def o_copy(tile, s):
        def for_each(act):
            if final:
                @pl.when(tile < split_tile)
                def _():
                    act(pltpu.make_async_copy(acc_ref.at[s], o_head_hbm.at[pl.ds(tile * tm, tm)], out_sem.at[s]))

                @pl.when(tile >= split_tile)
                def _():
                    act(pltpu.make_async_copy(acc_ref.at[s], o_tail_hbm.at[pl.ds((tile - split_tile) * tm, tm)],
                                              out_sem.at[s]))
            else:
                act(pltpu.make_async_copy(acc_ref.at[s], o_head_hbm.at[pl.ds(tile * tm, tm)], out_sem.at[s]))
        return for_each

    def start(copy):
        copy.start()

    def wait(copy):
        copy.wait()

    @pl.when(f == 0)
    def _():
        @pl.when(i == 0)
        def _():
            x_copy(0, 0).start()
        x_copy(i, slot).wait()

        def norm(rows):
            h_ref[rows, :] = _rms_rows(acc_ref[slot, rows, :], g_ref[...]).astype(h_ref.dtype)
        _for_row_blocks(tm, norm, unroll=_FFN_NORM_BLOCKS_IN_FLIGHT)

    @pl.when(jnp.logical_and(f == _FFN_PREFETCH_STEP, i + 1 < n_tiles))
    def _():
        @pl.when(i >= 1)
        def _():
            o_copy(i - 1, 1 - slot)(wait)
        x_copy(i + 1, 1 - slot).start()

    a = jnp.maximum(_dot(h_ref[...], wu_ref[...].astype(_BF16)), 0.0)
    acc_ref[slot] += _dot((a * a).astype(_BF16), wd_ref[...].astype(_BF16))

    @pl.when(f == n_steps - 1)
    def _():
        if final:
            def finish(rows):
                acc_ref[slot, rows, :] = _rms_rows(acc_ref[slot, rows, :], gf_ref[...])
            _for_row_blocks(tm, finish, rows=_ROWS_IN_PLACE, unroll=1)
        o_copy(i, slot)(start)

        @pl.when(i == n_tiles - 1)
        def _():
            o_copy(i, slot)(wait)

            @pl.when(i >= 1)
            def _():
                o_copy(i - 1, 1 - slot)(wait)


def _ffn(x, g, layer, w_up, w_down, g_final=None, split=None):
    rows, d = x.shape
    dff = w_up.shape[2]
    tm, tf = _TM, _TF
    assert dff // tf > _FFN_PREFETCH_STEP >= 1 and rows % tm == 0
    final = g_final is not None
    if final:
        assert split % tm == 0
        out_shape = [jax.ShapeDtypeStruct((split, d), _F32), jax.ShapeDtypeStruct((rows - split, d), _F32)]
    else:
        out_shape = [jax.ShapeDtypeStruct((rows, d), _F32)]
    in_specs = [pl.BlockSpec(memory_space=pl.ANY),
                pl.BlockSpec((None, 1, d), lambda i, f: (layer, 0, 0)),
                pl.BlockSpec((None, d, tf), lambda i, f: (layer, 0, f)),
                pl.BlockSpec((None, tf, d), lambda i, f: (layer, f, 0))]
    operands = [x, g.reshape(g.shape[0], 1, d), w_up, w_down]
    if final:
        in_specs.append(pl.BlockSpec((1, d), lambda i, f: (0, 0)))
        operands.append(g_final.reshape(1, d))
    windows = [((d, tf), _F32), ((tf, d), _F32)]
    scratch = [((2, tm, d), _F32), ((tm, d), _BF16)]
    outs = pl.pallas_call(
        functools.partial(_ffn_kernel, final, split // tm if final else None),
        grid=(rows // tm, dff // tf),
        in_specs=in_specs,
        out_specs=[pl.BlockSpec(memory_space=pl.ANY)] * len(out_shape),
        out_shape=out_shape,
        scratch_shapes=[pltpu.VMEM(s, dt) for s, dt in scratch]
                       + [pltpu.SemaphoreType.DMA((2,)), pltpu.SemaphoreType.DMA((2,))],
        compiler_params=_params(("arbitrary", "arbitrary"), windows, sum(_nbytes(s, dt) for s, dt in scratch)),
        name="ffn",
    )(*operands)
    return outs if final else outs[0]


def _gating_operands(w_s, b_s, dseq, d):
    groups, chunk, _ = w_s.shape
    t = jnp.arange(chunk)
    causal = t[:, None] >= t[None, :]
    same_seq = (t[:, None] // dseq) == (t[None, :] // dseq)
    reps = chunk // dseq
    ws_prompt = jnp.where(causal, w_s, 0)
    ws_sample = jnp.where(causal & same_seq, jnp.tile(w_s[:, :dseq, :dseq], (1, reps, reps)), 0)
    gd = d // groups
    bias_prompt = jnp.repeat(b_s.T, gd, axis=1)
    bias_sample = jnp.repeat(jnp.tile(b_s[:, :dseq].T, (reps, 1)), gd, axis=1)
    return jnp.stack([ws_prompt, ws_sample]).astype(_BF16), jnp.stack([bias_prompt, bias_sample])


def _prompt_conv_state(t, bsz, seq, keep):
    return jnp.stack([lax.slice_in_dim(t, (b + 1) * seq - keep, (b + 1) * seq, axis=0) for b in range(bsz)])


def kernel(x_prompt, x_sample, state_b_conv, state_c_conv, norm_mix_g, norm_ffn_g, final_norm_g, a_w_in, a_ln_g, a_ln_b, a_w_s, a_b_s, a_w_out, b_w_in, b_conv_w, b_w_out, c_w_pw1, c_b_pw1, c_conv_w, c_conv_b, c_ln_g, c_ln_b, c_w_pw2, c_b_pw2, ffn_w_up, ffn_w_down):
    bsz, seq, d = x_prompt.shape
    dbsz, dseq, _ = x_sample.shape
    n_p, n_s = bsz * seq, dbsz * dseq
    depth = norm_mix_g.shape[0]
    chunk = a_w_s.shape[-1]
    assert seq % chunk == 0 and chunk % dseq == 0 and seq & (seq - 1) == 0 and dseq & (dseq - 1) == 0
    assert seq % _TM_MIX == 0 and n_s % _TM_MIX == 0 and seq % _TM_CONV == 0 and n_s % _TM_CONV == 0
    assert n_p % _TM == 0 and n_s % _TM == 0 and (n_p + n_s) % _TM_IN == 0

    x = jnp.concatenate([x_prompt.reshape(n_p, d), x_sample.reshape(n_s, d)], axis=0)
    a_v, b_prompt, b_sample, c_prompt, c_sample = None, [], [], [], []
    for i in range(depth):
        j, kind = divmod(i, _N_MIXERS)
        if kind == 0:
            z = _inproj(_gelu_epilogue, x, norm_mix_g, i, a_w_in, None, j, 0, 1, 2 * d, _TM_IN, _TN_IN, _BF16,
                        "inproj_gelu")
            ws_all, bias_all = _gating_operands(a_w_s[j], a_b_s[j], dseq, d)
            y, a_v = _mix_gmlp(z, a_ln_g[j], a_ln_b[j], ws_all, bias_all, n_p, n_s, j, a_w_in.shape[0], a_v)
            x = _outproj(y, j, a_w_out, None, x)
        elif kind == 1:
            bg, h = _inproj(_plain_epilogue, x, norm_mix_g, i, b_w_in, None, j, 0, 1, d, _TM_IN, _TN_IN, _BF16,
                            "inproj_gate", emit_h=True)
            cx = _inproj(_product_epilogue, h, None, None, b_w_in, None, j, 1, 2, d, _TM_IN, _TN_IN_PAIR, _F32,
                         "inproj_product")
            y, sample_state = _mix_conv3(bg, cx, state_b_conv, j, b_conv_w[j], n_p, seq, dseq)
            b_prompt.append(_prompt_conv_state(cx, bsz, seq, state_b_conv.shape[2]))
            b_sample.append(sample_state)
            x = _outproj(y, j, b_w_out, None, x)
        else:
            glu = _inproj(_glu_epilogue, x, norm_mix_g, i, c_w_pw1, c_b_pw1, j, 0, 2, d, _TM_IN, _TN_IN_PAIR, _F32,
                          "inproj_glu")
            y, sample_state = _mix_conv31(glu, state_c_conv, j, c_conv_w[j], c_conv_b[j], c_ln_g[j], c_ln_b[j],
                                          n_p, seq, dseq)
            c_prompt.append(_prompt_conv_state(glu, bsz, seq, state_c_conv.shape[2]))
            c_sample.append(sample_state)
            x = _outproj(y, j, c_w_pw2, c_b_pw2, x)
        if i + 1 < depth:
            x = _ffn(x, norm_ffn_g, i, ffn_w_up, ffn_w_down)
        else:
            y_prompt, y_sample = _ffn(x, norm_ffn_g, i, ffn_w_up, ffn_w_down, final_norm_g, n_p)
    return (y_prompt.reshape(bsz, seq, d), y_sample.reshape(dbsz, dseq, d), a_v.reshape(-1, dbsz, dseq, d),
            jnp.stack(b_prompt), jnp.stack(b_sample), jnp.stack(c_prompt), jnp.stack(c_sample))
```

```python
import functools

import jax
import jax.numpy as jnp
from jax import lax
from jax.experimental import pallas as pl
from jax.experimental.pallas import tpu as pltpu

_F32 = jnp.float32
_BF16 = jnp.bfloat16
_EPS = 1e-6
_N_MIXERS = 3

_LANES = 128
_SUBLANES = 8
_VMEM_CAP_BYTES = 60000 * 1024
_VMEM_SLACK_BYTES = 12 * 1024 * 1024

_TM = 1024
_TM_IN = 768
_TM_IN_NORMED = 1024
_TN_IN = 1024
_TN_IN_PAIR = 512
_TM_MIX = 512
_TN_OUT = 1024
_TF = 1024
_FFN_PREFETCH_STEP = 2
_FFN_NORM_BLOCKS_IN_FLIGHT = 8
_TM_CONV = 256
_ROWS = 16
_ROWS_IN_PLACE = 64
_ROW_BLOCKS_IN_FLIGHT = 16
_CONV_HALO = 32
_TAP_CHAINS = 2
_TAP_BLOCKS_IN_FLIGHT = 16


def _nbytes(shape, dtype):
    n = 1
    for s in shape:
        n *= s
    return n * jnp.dtype(dtype).itemsize


def _params(semantics, windows, scratch=0):
    limit = 2 * sum(_nbytes(s, d) for s, d in windows) + scratch + _VMEM_SLACK_BYTES
    return pltpu.CompilerParams(dimension_semantics=semantics, vmem_limit_bytes=min(limit, _VMEM_CAP_BYTES))


def _rms_rows(x, g):
    return x * lax.rsqrt(jnp.mean(x * x, axis=-1, keepdims=True) + _EPS) * g


def _ln_rows(x, g, b):
    xc = x - jnp.mean(x, axis=-1, keepdims=True)
    return xc * lax.rsqrt(jnp.mean(xc * xc, axis=-1, keepdims=True) + _EPS) * g + b


def _for_row_blocks(n_rows, fn, rows=_ROWS, unroll=_ROW_BLOCKS_IN_FLIGHT):
    def body(r, carry):
        fn(pl.ds(pl.multiple_of(r * rows, rows), rows))
        return carry
    lax.fori_loop(0, n_rows // rows, body, 0, unroll=unroll)


def _dot(a, b):
    return jnp.dot(a, b, preferred_element_type=_F32)


def _gelu_epilogue(accs, out_ref):
    (acc,) = accs
    out_ref[...] = (0.5 * acc * (1.0 + lax.erf(acc * 0.7071067811865476))).astype(out_ref.dtype)


def _plain_epilogue(accs, out_ref):
    (acc,) = accs
    out_ref[...] = acc.astype(out_ref.dtype)


def _product_epilogue(accs, out_ref):
    out_ref[...] = (accs[0] * accs[1]).astype(out_ref.dtype)


def _glu_epilogue(accs, out_ref):
    out_ref[...] = (accs[0] * jax.nn.sigmoid(accs[1])).astype(out_ref.dtype)


def _resident_weight_block(layer, first_block, nb):
    return lambda i, j: (layer, 0, first_block + jnp.where(i == 0, j, nb - 1))


def _resident_weights(w_refs, wbf_refs):
    j = pl.program_id(1)

    @pl.when(pl.program_id(0) == 0)
    def _():
        for w_ref, wbf_ref in zip(w_refs, wbf_refs):
            wbf_ref[j] = w_ref[...].astype(wbf_ref.dtype)
    return [wbf_ref[j] for wbf_ref in wbf_refs]


def _inproj_kernel(epilogue, parts, has_bias, normed, x_ref, *refs):
    if not normed:
        g_ref, refs = refs[0], refs[1:]
    w_refs, refs = refs[:parts], refs[parts:]
    b_refs, refs = (refs[:parts], refs[parts:]) if has_bias else ((), refs)
    out_ref, refs = refs[0], refs[1:]
    if normed:
        h_ref = x_ref
    else:
        h_ref, refs = refs[0], refs[1:]

        @pl.when(pl.program_id(1) == 0)
        def _():
            def norm(rows):
                h_ref[rows, :] = _rms_rows(x_ref[rows, :], g_ref[...]).astype(h_ref.dtype)
            _for_row_blocks(h_ref.shape[0], norm)

    ws = _resident_weights(w_refs, refs)
    h = h_ref[...]
    accs = [_dot(h, w) for w in ws]
    if has_bias:
        accs = [acc + b_ref[...] for acc, b_ref in zip(accs, b_refs)]
    epilogue(accs, out_ref)


def _inproj(epilogue, x, g, g_layer, w, bias, layer, first_part, parts, n, tm, tn, out_dtype, name, emit_h=False):
    m, k = x.shape
    nb = n // tn
    normed = g is None
    in_specs = [pl.BlockSpec((tm, k), lambda i, j: (i, 0))]
    operands = [x]
    if not normed:
        in_specs.append(pl.BlockSpec((None, 1, k), lambda i, j: (g_layer, 0, 0)))
        operands.append(g.reshape(g.shape[0], 1, k))
    for p in range(parts):
        in_specs.append(pl.BlockSpec((None, k, tn), _resident_weight_block(layer, (first_part + p) * nb, nb)))
        operands.append(w)
    if bias is not None:
        for p in range(parts):
            in_specs.append(pl.BlockSpec((None, 1, tn), functools.partial(
                lambda i, j, p: (layer, 0, (first_part + p) * nb + j), p=p)))
            operands.append(bias.reshape(bias.shape[0], 1, -1))
    windows = [((tm, k), x.dtype)] + [((k, tn), _F32)] * parts + [((tm, tn), out_dtype)]
    out_specs = [pl.BlockSpec((tm, tn), lambda i, j: (i, j))]
    out_shape = [jax.ShapeDtypeStruct((m, n), out_dtype)]
    scratch = [((nb, k, tn), _BF16)] * parts
    if emit_h:
        out_specs.append(pl.BlockSpec((tm, k), lambda i, j: (i, 0)))
        out_shape.append(jax.ShapeDtypeStruct((m, k), _BF16))
        windows.append(((tm, k), _BF16))
    elif not normed:
        scratch = [((tm, k), _BF16)] + scratch
    outs = pl.pallas_call(
        functools.partial(_inproj_kernel, epilogue, parts, bias is not None, normed),
        grid=(m // tm, nb),
        in_specs=in_specs,
        out_specs=out_specs,
        out_shape=out_shape,
        scratch_shapes=[pltpu.VMEM(s, dt) for s, dt in scratch],
        compiler_params=_params(("arbitrary", "arbitrary"), windows, sum(_nbytes(s, dt) for s, dt in scratch)),
        name=name,
    )(*operands)
    return outs if emit_h else outs[0]


def _mix_gmlp_kernel(groups, chunk, aliased, u_ref, v_ref, lng_ref, lnb_ref, ws_ref, bias_ref, *refs):
    y_ref, vn_ref, vnb_ref = refs[1:] if aliased else refs
    tm, d = y_ref.shape
    gd = d // groups

    def normalise(rows):
        vn = _ln_rows(v_ref[rows, :].astype(_F32), lng_ref[...], lnb_ref[...])
        for slot in range(vn_ref.shape[0]):
            vn_ref[slot, rows, :] = vn
        vnb_ref[rows, :] = vn.astype(vnb_ref.dtype)
    _for_row_blocks(tm, normalise)

    def mix_chunk(c, carry):
        rows = pl.ds(pl.multiple_of(c * chunk, chunk), chunk)
        for g in range(groups):
            cols = slice(g * gd, (g + 1) * gd)
            mixed = _dot(ws_ref[0, g], vnb_ref[rows, cols]) + bias_ref[0, :, cols]
            y_ref[rows, cols] = (u_ref[rows, cols].astype(_F32) * mixed).astype(y_ref.dtype)
        return carry
    lax.fori_loop(0, tm // chunk, mix_chunk, 0, unroll=True)


def _mix_gmlp(z, ln_g, ln_b, ws_all, bias_all, n_prompt, n_sample, layer, n_layers, stacked_vn):
    m, d2 = z.shape
    d = d2 // 2
    tm = _TM_MIX
    n_pt = n_prompt // tm
    groups, chunk = ws_all.shape[1], ws_all.shape[2]
    aliased = stacked_vn is not None
    slots = 1 if aliased else n_layers
    first_slot = layer if aliased else 0
    windows = [((tm, d), _BF16)] * 3 + [((slots, tm, d), _F32), ((1, groups, chunk, chunk), _BF16), ((1, chunk, d), _F32)]
    in_specs = [
        pl.BlockSpec((tm, d), lambda i: (i, 0)),
        pl.BlockSpec((tm, d), lambda i: (i, 1)),
        pl.BlockSpec((1, d), lambda i: (0, 0)),
        pl.BlockSpec((1, d), lambda i: (0, 0)),
        pl.BlockSpec((1, groups, chunk, chunk), lambda i: (jnp.where(i < n_pt, 0, 1), 0, 0, 0)),
        pl.BlockSpec((1, chunk, d), lambda i: (jnp.where(i < n_pt, 0, 1), 0, 0)),
    ]
    operands = [z, z, ln_g.reshape(1, d), ln_b.reshape(1, d), ws_all, bias_all]
    if aliased:
        in_specs.append(pl.BlockSpec(memory_space=pl.ANY))
        operands.append(stacked_vn)
    return pl.pallas_call(
        functools.partial(_mix_gmlp_kernel, groups, chunk, aliased),
        grid=(m // tm,),
        in_specs=in_specs,
        out_specs=[
            pl.BlockSpec((tm, d), lambda i: (i, 0)),
            pl.BlockSpec((slots, tm, d), lambda i: (first_slot, jnp.maximum(i - n_pt, 0), 0)),
        ],
        out_shape=[jax.ShapeDtypeStruct((m, d), _BF16), jax.ShapeDtypeStruct((n_layers, n_sample, d), _F32)],
        scratch_shapes=[pltpu.VMEM((tm, d), _BF16)],
        input_output_aliases={len(operands) - 1: 1} if aliased else {},
        compiler_params=_params(("arbitrary",), windows, _nbytes((tm, d), _BF16)),
        name="mix_gmlp",
    )(*operands)


def _mix_conv3_kernel(n_pt, seq_tiles, dseq, bg_ref, cx_ref, past_ref, cw_ref, y_ref, state_ref, ext_ref):
    i = pl.program_id(0)
    tm, d = y_ref.shape
    halo = _SUBLANES
    seq_start = lax.rem(i, seq_tiles) == 0
    keep = past_ref.shape[1]

    def conv(l, cols, w, sample, b, carry):
        r0 = pl.multiple_of(b * _ROWS, _ROWS)
        rows = pl.ds(r0, _ROWS)
        cur = ext_ref[l, pl.ds(r0 + halo, _ROWS), :]
        prev1 = ext_ref[l, pl.ds(r0 + halo - 1, _ROWS), :]
        prev2 = ext_ref[l, pl.ds(r0 + halo - 2, _ROWS), :]
        if sample:
            pos = lax.broadcasted_iota(jnp.int32, (_ROWS, 1), 0) & (dseq - 1)
            seqs = [b * (_ROWS // dseq) + q for q in range(_ROWS // dseq)]

            def past_row(k):
                return jnp.concatenate(
                    [jnp.broadcast_to(past_ref[s, pl.ds(k, 1), cols], (dseq, _LANES)) for s in seqs], axis=0)
            prev1 = jnp.where(pos >= 1, prev1, past_row(1))
            prev2 = jnp.where(pos >= 2, prev2, jnp.where(pos == 0, past_row(0), past_row(1)))
            for q, s in enumerate(seqs):
                state_ref[s, :, cols] = ext_ref[l, pl.ds(r0 + halo + (q + 1) * dseq - keep, keep), :]
        acc = w[0] * prev2 + w[1] * prev1 + w[2] * cur
        y_ref[rows, cols] = (bg_ref[rows, cols].astype(_F32) * acc).astype(y_ref.dtype)
        return carry

    def lane_block(l, carry):
        cols = pl.ds(pl.multiple_of(l * _LANES, _LANES), _LANES)

        @pl.when(seq_start)
        def _():
            ext_ref[l, 0:halo, :] = jnp.zeros((halo, _LANES), _F32)

        @pl.when(jnp.logical_not(seq_start))
        def _():
            ext_ref[l, 0:halo, :] = ext_ref[l, tm:tm + halo, :]

        ext_ref[l, halo:halo + tm, :] = cx_ref[:, cols]
        w = [jnp.broadcast_to(cw_ref[k:k + 1, cols], (_ROWS, _LANES)) for k in range(3)]

        @pl.when(i < n_pt)
        def _():
            lax.fori_loop(0, tm // _ROWS, functools.partial(conv, l, cols, w, False), 0,
                          unroll=_ROW_BLOCKS_IN_FLIGHT)

        @pl.when(i >= n_pt)
        def _():
            lax.fori_loop(0, tm // _ROWS, functools.partial(conv, l, cols, w, True), 0,
                          unroll=_ROW_BLOCKS_IN_FLIGHT)
        return carry
    lax.fori_loop(0, d // _LANES, lane_block, 0)


def _mix_conv3(bg, cx, past, layer, conv_w, n_prompt, seq, dseq):
    m, d = cx.shape
    keep = past.shape[2]
    assert conv_w.shape[0] == keep + 1 == 3 and keep <= dseq and _ROWS % dseq == 0
    tm = _TM_MIX
    n_pt = n_prompt // tm
    n_seq = tm // dseq
    windows = [((tm, d), _BF16)] * 2 + [((tm, d), _F32)] + [((n_seq, _SUBLANES, d), _F32)] * 2
    seq_block = lambda i: jnp.maximum(i - n_pt, 0)
    return pl.pallas_call(
        functools.partial(_mix_conv3_kernel, n_pt, seq // tm, dseq),
        grid=(m // tm,),
        in_specs=[
            pl.BlockSpec((tm, d), lambda i: (i, 0)),
            pl.BlockSpec((tm, d), lambda i: (i, 0)),
            pl.BlockSpec((None, n_seq, keep, d), lambda i: (layer, seq_block(i), 0, 0)),
            pl.BlockSpec(conv_w.shape, lambda i: (0, 0)),
        ],
        out_specs=[pl.BlockSpec((tm, d), lambda i: (i, 0)),
                   pl.BlockSpec((n_seq, keep, d), lambda i: (seq_block(i), 0, 0))],
        out_shape=[jax.ShapeDtypeStruct((m, d), _BF16), jax.ShapeDtypeStruct((past.shape[1], keep, d), _F32)],
        scratch_shapes=[pltpu.VMEM((d // _LANES, tm + _SUBLANES, _LANES), _F32)],
        compiler_params=_params(("arbitrary",), windows, _nbytes((tm + _SUBLANES, d), _F32)),
        name="mix_conv3",
    )(bg, cx, past, conv_w)


def _depthwise_taps(width, fill, slab_of, ext_ref, cw_ref, c_ref, n_blocks, in_row, out_row):
    d = c_ref.shape[1]

    def lane_block(l, carry):
        cols = pl.ds(pl.multiple_of(l * _LANES, _LANES), _LANES)
        fill(l, cols)
        slab = slab_of(l)
        w = [jnp.broadcast_to(cw_ref[k:k + 1, cols], (_SUBLANES, _LANES)) for k in range(width)]

        def row_block(b, carry2):
            base = in_row(b)
            partial = [None] * _TAP_CHAINS
            for k in range(width):
                term = w[k] * ext_ref[slab, pl.ds(base + k, _SUBLANES), :]
                c = k % _TAP_CHAINS
                partial[c] = term if partial[c] is None else partial[c] + term
            while len(partial) > 1:
                partial = [a + b_ for a, b_ in zip(partial[0::2], partial[1::2])] + (
                    [partial[-1]] if len(partial) % 2 else [])
            c_ref[pl.ds(pl.multiple_of(out_row(b), _SUBLANES), _SUBLANES), cols] = partial[0]
            return carry2
        lax.fori_loop(0, n_blocks, row_block, 0, unroll=_TAP_BLOCKS_IN_FLIGHT)
        return carry
    lax.fori_loop(0, d // _LANES, lane_block, 0)


def _conformer_tail(c_ref, cb_ref, lng_ref, lnb_ref, y_ref):
    def tail(rows):
        c = _ln_rows(c_ref[rows, :] + cb_ref[...], lng_ref[...], lnb_ref[...])
        y_ref[rows, :] = (c * jax.nn.sigmoid(c)).astype(y_ref.dtype)
    _for_row_blocks(y_ref.shape[0], tail)


def _mix_conv31_kernel(width, n_pt, seq_tiles, dseq, glu_ref, past_ref, cw_ref, cb_ref, lng_ref, lnb_ref, y_ref,
                       state_ref, ext_ref, sext_ref, c_ref):
    i = pl.program_id(0)
    tm, d = y_ref.shape
    halo = _CONV_HALO
    lead = halo - (width - 1)

    @pl.when(i < n_pt)
    def _():
        def fill(l, cols):
            @pl.when(lax.rem(i, seq_tiles) == 0)
            def _():
                ext_ref[l, 0:halo, :] = jnp.zeros((halo, _LANES), _F32)

            @pl.when(lax.rem(i, seq_tiles) != 0)
            def _():
                ext_ref[l, 0:halo, :] = ext_ref[l, tm:tm + halo, :]

            ext_ref[l, halo:halo + tm, :] = glu_ref[:, cols]

        _depthwise_taps(width, fill, lambda l: l, ext_ref, cw_ref, c_ref, tm // _SUBLANES,
                        lambda b: b * _SUBLANES + lead, lambda b: b * _SUBLANES)

    @pl.when(i >= n_pt)
    def _():
        keep = width - 1

        def lane_block(l, carry):
            cols = pl.ds(pl.multiple_of(l * _LANES, _LANES), _LANES)
            sext_ref[0, 0:tm, :] = glu_ref[:, cols]
            w = [jnp.broadcast_to(cw_ref[k:k + 1, cols], (_SUBLANES, _LANES)) for k in range(width)]

            def sequences(q, carry2):
                s0 = pl.multiple_of(q * _SUBLANES, _SUBLANES)

                def conv_row(m):
                    if m < keep:
                        return past_ref[m, pl.ds(s0, _SUBLANES), cols]
                    return sext_ref[0, pl.ds(s0 * dseq + (m - keep), _SUBLANES, stride=dseq), :]

                for t in range(dseq):
                    partial = [None] * _TAP_CHAINS
                    for k in range(width):
                        term = w[k] * conv_row(t + k)
                        c = k % _TAP_CHAINS
                        partial[c] = term if partial[c] is None else partial[c] + term
                    acc = partial[0]
                    for p in partial[1:]:
                        acc = acc + p
                    sext_ref[0, pl.ds(tm + s0 * dseq + t, _SUBLANES, stride=dseq), :] = acc
                for m in range(keep):
                    state_ref[m, pl.ds(s0, _SUBLANES), cols] = conv_row(m + dseq)
                return carry2
            lax.fori_loop(0, tm // dseq // _SUBLANES, sequences, 0)
            c_ref[:, cols] = sext_ref[0, tm:2 * tm, :]
            return carry
        lax.fori_loop(0, d // _LANES, lane_block, 0)

    _conformer_tail(c_ref, cb_ref, lng_ref, lnb_ref, y_ref)


def _mix_conv31(glu, past, layer, conv_w, conv_b, ln_g, ln_b, n_prompt, seq, dseq):
    m, d = glu.shape
    width = conv_w.shape[0]
    keep = width - 1
    assert keep <= _CONV_HALO and dseq == _SUBLANES
    tm = _TM_CONV
    n_pt = n_prompt // tm
    n_seq = tm // dseq
    assert n_seq % _SUBLANES == 0
    row = pl.BlockSpec((1, d), lambda i: (0, 0))
    scratch = [((d // _LANES, tm + _CONV_HALO, _LANES), _F32), ((1, 2 * tm, _LANES), _F32), ((tm, d), _F32)]
    windows = [((tm, d), _F32), ((keep, n_seq, d), _F32), ((tm, d), _BF16), ((keep, n_seq, d), _F32)]
    seq_block = lambda i: jnp.maximum(i - n_pt, 0)
    y, state = pl.pallas_call(
        functools.partial(_mix_conv31_kernel, width, n_pt, seq // tm, dseq),
        grid=(m // tm,),
        in_specs=[pl.BlockSpec((tm, d), lambda i: (i, 0)),
                  pl.BlockSpec((None, keep, n_seq, d), lambda i: (layer, 0, seq_block(i), 0)),
                  pl.BlockSpec(conv_w.shape, lambda i: (0, 0)), row, row, row],
        out_specs=[pl.BlockSpec((tm, d), lambda i: (i, 0)),
                   pl.BlockSpec((keep, n_seq, d), lambda i: (0, seq_block(i), 0))],
        out_shape=[jax.ShapeDtypeStruct((m, d), _BF16), jax.ShapeDtypeStruct((keep, past.shape[1], d), _F32)],
        scratch_shapes=[pltpu.VMEM(s, dt) for s, dt in scratch],
        compiler_params=_params(("arbitrary",), windows, sum(_nbytes(s, dt) for s, dt in scratch)),
        name="mix_conv31",
    )(glu, jnp.swapaxes(past, 1, 2), conv_w, conv_b.reshape(1, d), ln_g.reshape(1, d), ln_b.reshape(1, d))
    return y, jnp.swapaxes(state, 0, 1)


def _outproj_kernel(has_bias, y_ref, w_ref, *refs):
    if has_bias:
        b_ref, x_ref, xo_ref, wbf_ref = refs
    else:
        x_ref, xo_ref, wbf_ref = refs
    (w,) = _resident_weights([w_ref], [wbf_ref])
    acc = _dot(y_ref[...], w)
    if has_bias:
        acc = acc + b_ref[...]
    xo_ref[...] = x_ref[...] + acc


def _outproj(y, layer, w, bias, x):
    m, k = y.shape
    n = w.shape[2]
    tm, tn = _TM, _TN_OUT
    nb = n // tn
    in_specs = [pl.BlockSpec((tm, k), lambda i, j: (i, 0)),
                pl.BlockSpec((None, k, tn), _resident_weight_block(layer, 0, nb))]
    operands = [y, w]
    if bias is not None:
        in_specs.append(pl.BlockSpec((None, 1, tn), lambda i, j: (layer, 0, j)))
        operands.append(bias.reshape(bias.shape[0], 1, n))
    in_specs.append(pl.BlockSpec((tm, tn), lambda i, j: (i, j)))
    operands.append(x)
    windows = [((tm, k), _BF16), ((k, tn), _F32), ((tm, tn), _F32), ((tm, tn), _F32)]
    return pl.pallas_call(
        functools.partial(_outproj_kernel, bias is not None),
        grid=(m // tm, nb),
        in_specs=in_specs,
        out_specs=pl.BlockSpec((tm, tn), lambda i, j: (i, j)),
        out_shape=jax.ShapeDtypeStruct((m, n), _F32),
        scratch_shapes=[pltpu.VMEM((nb, k, tn), _BF16)],
        compiler_params=_params(("arbitrary", "arbitrary"), windows, _nbytes((nb, k, tn), _BF16)),
        name="outproj",
    )(*operands)


def _ffn_kernel(final, split_tile, x_hbm, g_ref, wu_ref, wd_ref, *refs):
    if final:
        gf_ref, o_head_hbm, o_tail_hbm, acc_ref, h_ref, in_sem, out_sem = refs
    else:
        o_head_hbm, acc_ref, h_ref, in_sem, out_sem = refs
    i, f = pl.program_id(0), pl.program_id(1)
    n_tiles, n_steps = pl.num_programs(0), pl.num_programs(1)
    tm = acc_ref.shape[1]
    slot = lax.rem(i, 2)

    def x_copy(tile, s):
        return pltpu.make_async_copy(x_hbm.at[pl.ds(tile * tm, tm)], acc_ref.at[s], in_sem.at[s])

    def o_copy(tile, s):
        def for_each(act):
            if final:
                @pl.when(tile < split_tile)
                def _():
                    act(pltpu.make_async_copy(acc_ref.at[s], o_head_hbm.at[pl.ds(tile * tm, tm)], out_sem.at[s]))

                @pl.when(tile >= split_tile)
                def _():
                    act(pltpu.make_async_copy(acc_ref.at[s], o_tail_hbm.at[pl.ds((tile - split_tile) * tm, tm)],
                                              out_sem.at[s]))
            else:
                act(pltpu.make_async_copy(acc_ref.at[s], o_head_hbm.at[pl.ds(tile * tm, tm)], out_sem.at[s]))
        return for_each

    def start(copy):
        copy.start()

    def wait(copy):
        copy.wait()

    @pl.when(f == 0)
    def _():
        @pl.when(i == 0)
        def _():
            x_copy(0, 0).start()
        x_copy(i, slot).wait()

        def norm(rows):
            h_ref[rows, :] = _rms_rows(acc_ref[slot, rows, :], g_ref[...]).astype(h_ref.dtype)
        _for_row_blocks(tm, norm, unroll=_FFN_NORM_BLOCKS_IN_FLIGHT)

    @pl.when(jnp.logical_and(f == _FFN_PREFETCH_STEP, i + 1 < n_tiles))
    def _():
        @pl.when(i >= 1)
        def _():
            o_copy(i - 1, 1 - slot)(wait)
        x_copy(i + 1, 1 - slot).start()

    a = jnp.maximum(_dot(h_ref[...], wu_ref[...].astype(_BF16)), 0.0)
    acc_ref[slot] += _dot((a * a).astype(_BF16), wd_ref[...].astype(_BF16))

    @pl.when(f == n_steps - 1)
    def _():
        if final:
            def finish(rows):
                acc_ref[slot, rows, :] = _rms_rows(acc_ref[slot, rows, :], gf_ref[...])
            _for_row_blocks(tm, finish, rows=_ROWS_IN_PLACE, unroll=1)
        o_copy(i, slot)(start)

        @pl.when(i == n_tiles - 1)
        def _():
            o_copy(i, slot)(wait)

            @pl.when(i >= 1)
            def _():
                o_copy(i - 1, 1 - slot)(wait)


def _ffn(x, g, layer, w_up, w_down, g_final=None, split=None):
    rows, d = x.shape
    dff = w_up.shape[2]
    tm, tf = _TM, _TF
    assert dff // tf > _FFN_PREFETCH_STEP >= 1 and rows % tm == 0
    final = g_final is not None
    if final:
        assert split % tm == 0
        out_shape = [jax.ShapeDtypeStruct((split, d), _F32), jax.ShapeDtypeStruct((rows - split, d), _F32)]
    else:
        out_shape = [jax.ShapeDtypeStruct((rows, d), _F32)]
    in_specs = [pl.BlockSpec(memory_space=pl.ANY),
                pl.BlockSpec((None, 1, d), lambda i, f: (layer, 0, 0)),
                pl.BlockSpec((None, d, tf), lambda i, f: (layer, 0, f)),
                pl.BlockSpec((None, tf, d), lambda i, f: (layer, f, 0))]
    operands = [x, g.reshape(g.shape[0], 1, d), w_up, w_down]
    if final:
        in_specs.append(pl.BlockSpec((1, d), lambda i, f: (0, 0)))
        operands.append(g_final.reshape(1, d))
    windows = [((d, tf), _F32), ((tf, d), _F32)]
    scratch = [((2, tm, d), _F32), ((tm, d), _BF16)]
    outs = pl.pallas_call(
        functools.partial(_ffn_kernel, final, split // tm if final else None),
        grid=(rows // tm, dff // tf),
        in_specs=in_specs,
        out_specs=[pl.BlockSpec(memory_space=pl.ANY)] * len(out_shape),
        out_shape=out_shape,
        scratch_shapes=[pltpu.VMEM(s, dt) for s, dt in scratch]
                       + [pltpu.SemaphoreType.DMA((2,)), pltpu.SemaphoreType.DMA((2,))],
        compiler_params=_params(("arbitrary", "arbitrary"), windows, sum(_nbytes(s, dt) for s, dt in scratch)),
        name="ffn",
    )(*operands)
    return outs if final else outs[0]


def _gating_operands(w_s, b_s, dseq, d):
    groups, chunk, _ = w_s.shape
    t = jnp.arange(chunk)
    causal = t[:, None] >= t[None, :]
    same_seq = (t[:, None] // dseq) == (t[None, :] // dseq)
    reps = chunk // dseq
    ws_prompt = jnp.where(causal, w_s, 0)
    ws_sample = jnp.where(causal & same_seq, jnp.tile(w_s[:, :dseq, :dseq], (1, reps, reps)), 0)
    gd = d // groups
    bias_prompt = jnp.repeat(b_s.T, gd, axis=1)
    bias_sample = jnp.repeat(jnp.tile(b_s[:, :dseq].T, (reps, 1)), gd, axis=1)
    return jnp.stack([ws_prompt, ws_sample]).astype(_BF16), jnp.stack([bias_prompt, bias_sample])


def _prompt_conv_state(t, bsz, seq, keep):
    return jnp.stack([lax.slice_in_dim(t, (b + 1) * seq - keep, (b + 1) * seq, axis=0) for b in range(bsz)])


def kernel(x_prompt, x_sample, state_b_conv, state_c_conv, norm_mix_g, norm_ffn_g, final_norm_g, a_w_in, a_ln_g, a_ln_b, a_w_s, a_b_s, a_w_out, b_w_in, b_conv_w, b_w_out, c_w_pw1, c_b_pw1, c_conv_w, c_conv_b, c_ln_g, c_ln_b, c_w_pw2, c_b_pw2, ffn_w_up, ffn_w_down):
    bsz, seq, d = x_prompt.shape
    dbsz, dseq, _ = x_sample.shape
    n_p, n_s = bsz * seq, dbsz * dseq
    depth = norm_mix_g.shape[0]
    chunk = a_w_s.shape[-1]
    assert seq % chunk == 0 and chunk % dseq == 0 and seq & (seq - 1) == 0 and dseq & (dseq - 1) == 0
    assert seq % _TM_MIX == 0 and n_s % _TM_MIX == 0 and seq % _TM_CONV == 0 and n_s % _TM_CONV == 0
    assert n_p % _TM == 0 and n_s % _TM == 0 and (n_p + n_s) % _TM_IN == 0

    x = jnp.concatenate([x_prompt.reshape(n_p, d), x_sample.reshape(n_s, d)], axis=0)
    a_v, b_prompt, b_sample, c_prompt, c_sample = None, [], [], [], []
    for i in range(depth):
        j, kind = divmod(i, _N_MIXERS)
        if kind == 0:
            z = _inproj(_gelu_epilogue, x, norm_mix_g, i, a_w_in, None, j, 0, 1, 2 * d, _TM_IN, _TN_IN, _BF16,
                        "inproj_gelu")
            ws_all, bias_all = _gating_operands(a_w_s[j], a_b_s[j], dseq, d)
            y, a_v = _mix_gmlp(z, a_ln_g[j], a_ln_b[j], ws_all, bias_all, n_p, n_s, j, a_w_in.shape[0], a_v)
            x = _outproj(y, j, a_w_out, None, x)
        elif kind == 1:
            bg, h = _inproj(_plain_epilogue, x, norm_mix_g, i, b_w_in, None, j, 0, 1, d, _TM_IN, _TN_IN, _BF16,
                            "inproj_gate", emit_h=True)
            cx = _inproj(_product_epilogue, h, None, None, b_w_in, None, j, 1, 2, d, _TM_IN_NORMED, _TN_IN_PAIR, _F32,
                         "inproj_product")
            y, sample_state = _mix_conv3(bg, cx, state_b_conv, j, b_conv_w[j], n_p, seq, dseq)
            b_prompt.append(_prompt_conv_state(cx, bsz, seq, state_b_conv.shape[2]))
            b_sample.append(sample_state)
            x = _outproj(y, j, b_w_out, None, x)
        else:
            glu = _inproj(_glu_epilogue, x, norm_mix_g, i, c_w_pw1, c_b_pw1, j, 0, 2, d, _TM_IN, _TN_IN_PAIR, _F32,
                          "inproj_glu")
            y, sample_state = _mix_conv31(glu, state_c_conv, j, c_conv_w[j], c_conv_b[j], c_ln_g[j], c_ln_b[j],
                                          n_p, seq, dseq)
            c_prompt.append(_prompt_conv_state(glu, bsz, seq, state_c_conv.shape[2]))
            c_sample.append(sample_state)
            x = _outproj(y, j, c_w_pw2, c_b_pw2, x)
        if i + 1 < depth:
            x = _ffn(x, norm_ffn_g, i, ffn_w_up, ffn_w_down)
        else:
            y_prompt, y_sample = _ffn(x, norm_ffn_g, i, ffn_w_up, ffn_w_down, final_norm_g, n_p)
    return (y_prompt.reshape(bsz, seq, d), y_sample.reshape(dbsz, dseq, d), a_v.reshape(-1, dbsz, dseq, d),
            jnp.stack(b_prompt), jnp.stack(b_sample), jnp.stack(c_prompt), jnp.stack(c_sample))
```
